```python
import jax, jax.numpy as jnp
from jax import lax
import numpy as np

D_MODEL = 1024
BATCH = 8
SEQ = 4096
DEPTH = 4

GRID_W = 64
CTX_LEN = 256
Q_BLOCK = 128
ROPE_BASE = 10000.0
EPS = 1e-6

FOURIER_GROUPS = 4
FOURIER_GROUP_DIM = D_MODEL // 8
FOURIER_WIDTH = FOURIER_GROUPS * FOURIER_GROUP_DIM
MLA_HEADS = 8
MLA_NOPE_DIM = 64
MLA_ROPE_DIM = 32
MLA_V_DIM = 64
MLA_QK_DIM = MLA_NOPE_DIM + MLA_ROPE_DIM
MLA_Q_RANK = D_MODEL // 4
MLA_KV_RANK = D_MODEL // 8
MLA_SCALE = MLA_QK_DIM ** -0.5
EVEN_IN_WIDTH = FOURIER_WIDTH + MLA_Q_RANK + MLA_KV_RANK + MLA_ROPE_DIM
EVEN_OUT_WIDTH = FOURIER_WIDTH + MLA_HEADS * MLA_V_DIM

GQA_HEADS = 16
GQA_KV_HEADS = 4
GQA_HEAD_DIM = 64
GQA_GROUP = GQA_HEADS // GQA_KV_HEADS
GQA_Q_WIDTH = GQA_HEADS * GQA_HEAD_DIM
GQA_KV_WIDTH = GQA_KV_HEADS * GQA_HEAD_DIM
GQA_SCALE = GQA_HEAD_DIM ** -0.5

N_EXPERTS = 16
N_EXPERT_GROUPS = 4
EXPERTS_PER_GROUP = N_EXPERTS // N_EXPERT_GROUPS
TOP_K = 2
D_EXPERT = 512

kernel_name = 'hybrid_fourier_mla_gqa_grouped_moe_prefix_dit'


def rms_norm(x, g):
    xf = x.astype(jnp.float32)
    y = xf * lax.rsqrt(jnp.mean(xf * xf, axis=-1, keepdims=True) + EPS)
    return (y * g.astype(jnp.float32)).astype(x.dtype)


def axial_rope_tables(n, dim):
    rows = n // GRID_W
    row_id = jnp.repeat(jnp.arange(rows, dtype=jnp.float32), GRID_W)
    col_id = jnp.tile(jnp.arange(GRID_W, dtype=jnp.float32), rows)
    n_freq = dim // 4
    inv = ROPE_BASE ** (-jnp.arange(n_freq, dtype=jnp.float32) / n_freq)
    ang = jnp.concatenate([row_id[:, None] * inv, col_id[:, None] * inv], axis=-1)
    return jnp.cos(ang), jnp.sin(ang)


def apply_rope(x, cos, sin):
    extra = (1,) * (x.ndim - 3)
    cos = cos.reshape(cos.shape[:1] + extra + cos.shape[1:]).astype(x.dtype)
    sin = sin.reshape(sin.shape[:1] + extra + sin.shape[1:]).astype(x.dtype)
    half = x.shape[-1] // 2
    x1, x2 = x[..., :half], x[..., half:]
    return jnp.concatenate([x1 * cos - x2 * sin, x1 * sin + x2 * cos], axis=-1)


def attend_blocks(q, k, v):
    B, n = q.shape[:2]
    nb = n // Q_BLOCK
    qb = jnp.swapaxes(q.reshape((B, nb, Q_BLOCK) + q.shape[2:]), 0, 1)

    def one_block(q_blk):
        s = jnp.einsum('bqkgd,bnkd->bkgqn', q_blk, k, preferred_element_type=jnp.float32)
        p = jax.nn.softmax(s, axis=-1).astype(v.dtype)
        return jnp.einsum('bkgqn,bnkd->bqkgd', p, v)

    out = lax.map(one_block, qb)
    return jnp.swapaxes(out, 0, 1).reshape((B, n) + out.shape[3:])


def fourier_mix(u):
    B, n, _ = u.shape
    ug = u.reshape(B, n, FOURIER_GROUPS, FOURIER_GROUP_DIM).astype(jnp.float32)
    f = jnp.fft.fft2(ug, axes=(1, 3), norm='ortho')
    return jnp.real(f).reshape(B, n, FOURIER_WIDTH).astype(u.dtype)


def even_mixer(hx, hy, w_in, q_norm_g, w_qb, kv_norm_g, w_kvb, w_out, need_ctx):
    B, n, _ = hx.shape
    L = hy.shape[1]
    o1 = FOURIER_WIDTH
    o2 = o1 + MLA_Q_RANK
    o3 = o2 + MLA_KV_RANK
    cos, sin = axial_rope_tables(n, MLA_ROPE_DIM)

    def queries(pq, rotate):
        m = pq.shape[1]
        q = (rms_norm(pq, q_norm_g) @ w_qb).reshape(B, m, MLA_HEADS, MLA_QK_DIM)
        q_nope, q_rope = q[..., :MLA_NOPE_DIM], q[..., MLA_NOPE_DIM:]
        if rotate:
            q_rope = apply_rope(q_rope, cos, sin)
        return (jnp.concatenate([q_nope, q_rope], axis=-1) * MLA_SCALE)[:, :, :, None, :]

    def keys_values(pkv, pr, rotate):
        m = pkv.shape[1]
        kv = (rms_norm(pkv, kv_norm_g) @ w_kvb).reshape(B, m, MLA_HEADS, MLA_NOPE_DIM + MLA_V_DIM)
        if rotate:
            pr = apply_rope(pr, cos, sin)
        k_rope = jnp.broadcast_to(pr[:, :, None, :], (B, m, MLA_HEADS, MLA_ROPE_DIM))
        k = jnp.concatenate([kv[..., :MLA_NOPE_DIM], k_rope], axis=-1)
        return k, kv[..., MLA_NOPE_DIM:]

    px = hx @ w_in
    qx = queries(px[..., o1:o2], True)
    kx, vx = keys_values(px[..., o2:o3], px[..., o3:], True)
    if need_ctx:
        py = hy @ w_in
        fy = py[..., :o1]
        qy = queries(py[..., o1:o2], False)
        py_kv = py[..., o2:]
    else:
        py_kv = hy @ w_in[:, o2:]
    ky, vy = keys_values(py_kv[..., :MLA_KV_RANK], py_kv[..., MLA_KV_RANK:], False)
    k = jnp.concatenate([ky, kx], axis=1)
    v = jnp.concatenate([vy, vx], axis=1)
    ax = attend_blocks(qx, k, v).reshape(B, n, MLA_HEADS * MLA_V_DIM)
    ox = jnp.concatenate([fourier_mix(px[..., :o1]), ax], axis=-1) @ w_out
    if not need_ctx:
        return ox, None
    ay = attend_blocks(qy, ky, vy).reshape(B, L, MLA_HEADS * MLA_V_DIM)
    oy = jnp.concatenate([fourier_mix(fy), ay], axis=-1) @ w_out
    return ox, oy


def odd_mixer(hx, hy, w_qkv, q_norm_g, k_norm_g, w_out, need_ctx):
    B, n, _ = hx.shape
    L = hy.shape[1]
    cos, sin = axial_rope_tables(n, GQA_HEAD_DIM)
    px = hx @ w_qkv
    qx = rms_norm(px[..., :GQA_Q_WIDTH].reshape(B, n, GQA_HEADS, GQA_HEAD_DIM), q_norm_g)
    kx = rms_norm(px[..., GQA_Q_WIDTH:GQA_Q_WIDTH + GQA_KV_WIDTH].reshape(B, n, GQA_KV_HEADS, GQA_HEAD_DIM), k_norm_g)
    vx = px[..., GQA_Q_WIDTH + GQA_KV_WIDTH:].reshape(B, n, GQA_KV_HEADS, GQA_HEAD_DIM)
    qx = apply_rope(qx, cos, sin) * GQA_SCALE
    kx = apply_rope(kx, cos, sin)
    if need_ctx:
        py = hy @ w_qkv
        qy = rms_norm(py[..., :GQA_Q_WIDTH].reshape(B, L, GQA_HEADS, GQA_HEAD_DIM), q_norm_g) * GQA_SCALE
        py_kv = py[..., GQA_Q_WIDTH:]
    else:
        py_kv = hy @ w_qkv[:, GQA_Q_WIDTH:]
    ky = rms_norm(py_kv[..., :GQA_KV_WIDTH].reshape(B, L, GQA_KV_HEADS, GQA_HEAD_DIM), k_norm_g)
    vy = py_kv[..., GQA_KV_WIDTH:].reshape(B, L, GQA_KV_HEADS, GQA_HEAD_DIM)
    k = jnp.concatenate([ky, kx], axis=1)
    v = jnp.concatenate([vy, vx], axis=1)
    qx = qx.reshape(B, n, GQA_KV_HEADS, GQA_GROUP, GQA_HEAD_DIM)
    ox = attend_blocks(qx, k, v).reshape(B, n, GQA_Q_WIDTH) @ w_out
    if not need_ctx:
        return ox, None
    qy = qy.reshape(B, L, GQA_KV_HEADS, GQA_GROUP, GQA_HEAD_DIM)
    oy = attend_blocks(qy, ky, vy).reshape(B, L, GQA_Q_WIDTH) @ w_out
    return ox, oy


def route(t, router_w, router_b):
    s = jax.nn.sigmoid((t @ router_w).astype(jnp.float32))
    sel = s + router_b.astype(jnp.float32)
    T = s.shape[0]
    group_score = lax.top_k(sel.reshape(T, N_EXPERT_GROUPS, EXPERTS_PER_GROUP), TOP_K)[0].sum(-1)
    best_group = jnp.argmax(group_score, axis=-1)
    in_group = (jnp.arange(N_EXPERTS) // EXPERTS_PER_GROUP)[None, :] == best_group[:, None]
    _, idx = lax.top_k(jnp.where(in_group, sel, -jnp.inf), TOP_K)
    w = jnp.take_along_axis(s, idx, axis=-1)
    w = w / jnp.sum(w, axis=-1, keepdims=True)
    return jnp.sum(jax.nn.one_hot(idx, N_EXPERTS, dtype=jnp.float32) * w[..., None], axis=1)


def moe(h, router_w, router_b, w_gate, w_up, w_down):
    shp = h.shape
    t = h.reshape(-1, shp[-1])
    combine = route(t, router_w, router_b).astype(h.dtype)
    out = jnp.zeros_like(t)
    for e in range(N_EXPERTS):
        a = jax.nn.silu(t @ w_gate[e]) * (t @ w_up[e])
        out = out + combine[:, e:e + 1] * (a @ w_down[e])
    return out.reshape(shp)


def setup_inputs(seed: int = 0) -> dict:
    key = jax.random.key(seed)
    ks = jax.random.split(key, 24)
    f32 = jnp.float32
    n_even = (DEPTH + 1) // 2
    n_odd = DEPTH // 2

    def w(k, shape, fan_in, gain=1.0):
        return jax.random.normal(k, shape, f32) * (gain * fan_in ** -0.5)

    def g(k, shape):
        return 1.0 + 0.01 * jax.random.normal(k, shape, f32)

    return {
        'x': jax.random.normal(ks[0], (BATCH, SEQ, D_MODEL), f32),
        'c': jax.random.normal(ks[1], (BATCH, D_MODEL), f32),
        'ctx': jax.random.normal(ks[2], (BATCH, CTX_LEN, D_MODEL), f32),
        'c_ctx': jax.random.normal(ks[3], (D_MODEL,), f32),
        'ada_w': w(ks[4], (DEPTH, D_MODEL, 6 * D_MODEL), D_MODEL, 0.5),
        'ada_b': 0.02 * jax.random.normal(ks[5], (DEPTH, 6 * D_MODEL), f32),
        'norm_mix_g': g(ks[6], (DEPTH, D_MODEL)),
        'norm_ffn_g': g(ks[7], (DEPTH, D_MODEL)),
        'ev_w_in': w(ks[8], (n_even, D_MODEL, EVEN_IN_WIDTH), D_MODEL),
        'ev_q_norm_g': g(ks[9], (n_even, MLA_Q_RANK)),
        'ev_w_qb': w(ks[10], (n_even, MLA_Q_RANK, MLA_HEADS * MLA_QK_DIM), MLA_Q_RANK),
        'ev_kv_norm_g': g(ks[11], (n_even, MLA_KV_RANK)),
        'ev_w_kvb': w(ks[12], (n_even, MLA_KV_RANK, MLA_HEADS * (MLA_NOPE_DIM + MLA_V_DIM)), MLA_KV_RANK),
        'ev_w_out': w(ks[13], (n_even, EVEN_OUT_WIDTH, D_MODEL), EVEN_OUT_WIDTH),
        'od_w_qkv': w(ks[14], (n_odd, D_MODEL, GQA_Q_WIDTH + 2 * GQA_KV_WIDTH), D_MODEL),
        'od_q_norm_g': g(ks[15], (n_odd, GQA_HEAD_DIM)),
        'od_k_norm_g': g(ks[16], (n_odd, GQA_HEAD_DIM)),
        'od_w_out': w(ks[17], (n_odd, GQA_Q_WIDTH, D_MODEL), GQA_Q_WIDTH),
        'router_w': w(ks[18], (D_MODEL, N_EXPERTS), D_MODEL),
        'router_b': 0.01 * jax.random.normal(ks[19], (N_EXPERTS,), f32),
        'exp_w_gate': w(ks[20], (DEPTH, N_EXPERTS, D_MODEL, D_EXPERT), D_MODEL),
        'exp_w_up': w(ks[21], (DEPTH, N_EXPERTS, D_MODEL, D_EXPERT), D_MODEL),
        'exp_w_down': w(ks[22], (DEPTH, N_EXPERTS, D_EXPERT, D_MODEL), D_EXPERT),
        'final_norm_g': g(ks[23], (D_MODEL,)),
    }


def reference(x, c, ctx, c_ctx, ada_w, ada_b, norm_mix_g, norm_ffn_g, ev_w_in, ev_q_norm_g, ev_w_qb,
              ev_kv_norm_g, ev_w_kvb, ev_w_out, od_w_qkv, od_q_norm_g, od_k_norm_g, od_w_out,
              router_w, router_b, exp_w_gate, exp_w_up, exp_w_down, final_norm_g):
    y = ctx
    sc_x = jax.nn.silu(c)
    sc_y = jax.nn.silu(c_ctx)
    for i in range(DEPTH):
        need_ctx = i < DEPTH - 1
        mod_x = (sc_x @ ada_w[i] + ada_b[i])[:, None, :]
        mod_y = sc_y @ ada_w[i] + ada_b[i]
        sh1x, sc1x, g1x, sh2x, sc2x, g2x = jnp.split(mod_x, 6, axis=-1)
        sh1y, sc1y, g1y, sh2y, sc2y, g2y = jnp.split(mod_y, 6, axis=-1)

        hx = rms_norm(x, norm_mix_g[i]) * (1.0 + sc1x) + sh1x
        hy = rms_norm(y, norm_mix_g[i]) * (1.0 + sc1y) + sh1y
        if i % 2 == 0:
            j = i // 2
            ox, oy = even_mixer(hx, hy, ev_w_in[j], ev_q_norm_g[j], ev_w_qb[j], ev_kv_norm_g[j],
                                ev_w_kvb[j], ev_w_out[j], need_ctx)
        else:
            j = i // 2
            ox, oy = odd_mixer(hx, hy, od_w_qkv[j], od_q_norm_g[j], od_k_norm_g[j], od_w_out[j], need_ctx)
        x = x + g1x * ox
        if need_ctx:
            y = y + g1y * oy

        hx = rms_norm(x, norm_ffn_g[i]) * (1.0 + sc2x) + sh2x
        x = x + g2x * moe(hx, router_w, router_b, exp_w_gate[i], exp_w_up[i], exp_w_down[i])
        if need_ctx:
            hy = rms_norm(y, norm_ffn_g[i]) * (1.0 + sc2y) + sh2y
            y = y + g2y * moe(hy, router_w, router_b, exp_w_gate[i], exp_w_up[i], exp_w_down[i])
    return rms_norm(x, final_norm_g)
```

```python
import functools
import math

import jax
import jax.numpy as jnp
from jax import lax
from jax.experimental import pallas as pl
from jax.experimental.pallas import tpu as pltpu

F32 = jnp.float32
BF16 = jnp.bfloat16

TM = 256
TQ = 256
TKC = 512
TMM = 256
TN_DFT = 512
EPS = 1e-6
LOG2E = 1.4426950408889634
ROPE_BASE = 10000.0
GRID_W = 64

FOURIER_GROUPS = 4
FOURIER_GROUP_DIM = 128
FOURIER_WIDTH = 512
MLA_HEADS = 8
MLA_NOPE_DIM = 64
MLA_ROPE_DIM = 32
MLA_V_DIM = 64
MLA_QK_DIM = 96
MLA_Q_RANK = 256
MLA_KV_RANK = 128
MLA_SCALE = MLA_QK_DIM ** -0.5
GQA_HEADS = 16
GQA_KV_HEADS = 4
GQA_HEAD_DIM = 64
GQA_GROUP = 4
GQA_SCALE = GQA_HEAD_DIM ** -0.5
N_EXPERTS = 16
N_GROUPS = 4
N_CLASSES = 24
D_EXPERT = 512
DK = 128
DV = 64
VMEM_LIMIT = 56 * 1024 * 1024

_PAIR_LO = (0, 0, 0, 1, 1, 2)
_PAIR_HI = (1, 2, 3, 2, 3, 3)


def _nt(a, b):
    return lax.dot_general(a, b, (((1,), (1,)), ((), ())), preferred_element_type=F32)


def _tn(a, b):
    return lax.dot_general(a, b, (((0,), (0,)), ((), ())), preferred_element_type=F32)


def _dot(a, b):
    return jnp.dot(a, b, preferred_element_type=F32)


class _Layout:
    def __init__(self, B, S, C):
        assert C == TM and S % TM == 0 and S % TN_DFT == 0
        self.B, self.S, self.C = B, S, C
        self.SB = S // TM
        self.NL = B * self.SB
        self.NT = self.NL + B
        self.TOKP = self.NT * TM

    def mrow(self, u):
        return jnp.where(u < self.NL, u // self.SB, self.B)

    def ropeblk(self, u):
        return jnp.where(u < self.NL, 1 + u % self.SB, 0)


def _cparams():
    return pltpu.CompilerParams(vmem_limit_bytes=VMEM_LIMIT)


def _full(a):
    return pl.BlockSpec(a.shape, lambda *_: (0,) * a.ndim)


def _mod_kernel(s_ref, w_ref, b_ref, o_ref):
    s = s_ref[...]
    a = s * jax.nn.sigmoid(s)
    ah = a.astype(BF16)
    al = (a - ah.astype(F32)).astype(BF16)
    w = w_ref[0]
    wh = w.astype(BF16)
    wl = (w - wh.astype(F32)).astype(BF16)
    o_ref[0] = _dot(ah, wh) + _dot(ah, wl) + _dot(al, wh) + b_ref[0]


def _modulation(cond, ada_w, ada_b):
    depth, d, w6 = ada_w.shape
    tn = 768
    return pl.pallas_call(
        _mod_kernel,
        grid=(depth, w6 // tn),
        in_specs=[
            pl.BlockSpec((16, d), lambda i, j: (0, 0)),
            pl.BlockSpec((1, d, tn), lambda i, j: (i, 0, j)),
            pl.BlockSpec((1, 1, tn), lambda i, j: (i, 0, j)),
        ],
        out_specs=pl.BlockSpec((1, 16, tn), lambda i, j: (i, 0, j)),
        out_shape=jax.ShapeDtypeStruct((depth, 16, w6), F32),
        compiler_params=_cparams(),
        name="modulation",
    )(cond, ada_w, ada_b.reshape(depth, 1, w6))


def _norm_mod(z, g, sc, sh):
    ms = jnp.mean(z * z, axis=-1, keepdims=True)
    return (z * lax.rsqrt(ms + EPS) * g) * (1.0 + sc) + sh


def _pre_even_kernel(z_ref, mod_ref, g_ref, wf_ref, wqa_ref, wkvat_ref, wkva_ref, wpr_ref,
                     gq_ref, wqb_ref, gkvc_ref, gkvr_ref, wv_ref, wk_ref, cs_ref,
                     ct_ref, st_ref, t1_ref, t2_ref,
                     ab_ref, q_ref, k_ref, v_ref):
    h = _norm_mod(z_ref[...], g_ref[...], mod_ref[0, 1:2, :], mod_ref[0, 0:1, :])
    hb = h.astype(BF16)

    ft = _nt(wf_ref[...], hb).astype(BF16)
    gd = FOURIER_GROUP_DIM
    for g in range(FOURIER_GROUPS):
        ab = _dot(cs_ref[...], ft[g * gd:(g + 1) * gd, :])
        ab_ref[g * gd:(g + 1) * gd, :] = ab[:gd].astype(BF16)
        ab_ref[FOURIER_WIDTH + g * gd:FOURIER_WIDTH + (g + 1) * gd, :] = ab[gd:].astype(BF16)

    pq = _nt(wqa_ref[...], hb)
    rs = lax.rsqrt(jnp.mean(pq * pq, axis=0, keepdims=True) + EPS)
    qn = (pq * rs * gq_ref[...]).astype(BF16)
    qt = _dot(wqb_ref[...], qn) * (MLA_SCALE * LOG2E)
    cos = ct_ref[...]
    sin = st_ref[...]
    hr = MLA_ROPE_DIM // 2
    for hd in range(MLA_HEADS):
        o = hd * MLA_QK_DIM
        r = hd * DK
        x1 = qt[o + MLA_NOPE_DIM:o + MLA_NOPE_DIM + hr, :]
        x2 = qt[o + MLA_NOPE_DIM + hr:o + MLA_QK_DIM, :]
        q_ref[r:r + MLA_NOPE_DIM, :] = qt[o:o + MLA_NOPE_DIM, :].astype(BF16)
        q_ref[r + MLA_NOPE_DIM:r + MLA_NOPE_DIM + hr, :] = (x1 * cos - x2 * sin).astype(BF16)
        q_ref[r + MLA_NOPE_DIM + hr:r + MLA_QK_DIM, :] = (x1 * sin + x2 * cos).astype(BF16)
        q_ref[r + MLA_QK_DIM:r + DK, :] = jnp.zeros((DK - MLA_QK_DIM, TM), BF16)

    pkvt = _nt(wkvat_ref[...], hb)
    rst = lax.rsqrt(jnp.mean(pkvt * pkvt, axis=0, keepdims=True) + EPS)
    kvnt = (pkvt * rst * gkvc_ref[...]).astype(BF16)
    v_ref[...] = _dot(wv_ref[...], kvnt).astype(BF16)

    pkv = _dot(hb, wkva_ref[...])
    rsn = lax.rsqrt(jnp.mean(pkv * pkv, axis=-1, keepdims=True) + EPS)
    kvn = (pkv * rsn * gkvr_ref[...]).astype(BF16)
    pr = _dot(hb, wpr_ref[...])
    lhs = jnp.concatenate([kvn, (pr * t1_ref[...]).astype(BF16), (pr * t2_ref[...]).astype(BF16)], axis=1)
    kf = _dot(lhs, wk_ref[...])
    for hd in range(MLA_HEADS):
        k_ref[hd] = kf[:, hd * DK:(hd + 1) * DK].astype(BF16)


def _pre_even(lay, z, mod, g_mix, w, tabs):
    d = z.shape[1]
    tokp = lay.TOKP
    col = lambda rows: pl.BlockSpec((rows, TM), lambda u: (0, u))
    hr = MLA_ROPE_DIM // 2
    in_specs = [
        pl.BlockSpec((TM, d), lambda u: (u, 0)),
        pl.BlockSpec((1, 6, d), lambda u: (lay.mrow(u), 0, 0)),
        _full(g_mix),
        _full(w["wf"]), _full(w["wqa"]), _full(w["wkvat"]), _full(w["wkva"]), _full(w["wpr"]),
        _full(w["gq"]), _full(w["wqb"]), _full(w["gkvc"]), _full(w["gkvr"]), _full(w["wv"]), _full(w["wk"]),
        _full(tabs["cs"]),
        pl.BlockSpec((hr, TM), lambda u: (0, lay.ropeblk(u))),
        pl.BlockSpec((hr, TM), lambda u: (0, lay.ropeblk(u))),
        pl.BlockSpec((TM, 128), lambda u: (lay.ropeblk(u), 0)),
        pl.BlockSpec((TM, 128), lambda u: (lay.ropeblk(u), 0)),
    ]
    out_specs = [
        col(2 * FOURIER_WIDTH),
        col(MLA_HEADS * DK),
        pl.BlockSpec((MLA_HEADS, TM, DK), lambda u: (0, u, 0)),
        col(MLA_HEADS * DV),
    ]
    out_shape = [
        jax.ShapeDtypeStruct((2 * FOURIER_WIDTH, tokp), BF16),
        jax.ShapeDtypeStruct((MLA_HEADS * DK, tokp), BF16),
        jax.ShapeDtypeStruct((MLA_HEADS, tokp, DK), BF16),
        jax.ShapeDtypeStruct((MLA_HEADS * DV, tokp), BF16),
    ]
    return pl.pallas_call(
        _pre_even_kernel, grid=(lay.NT,), in_specs=in_specs, out_specs=out_specs, out_shape=out_shape,
        compiler_params=_cparams(), name="pre_even",
    )(z, mod, g_mix, w["wf"], w["wqa"], w["wkvat"], w["wkva"], w["wpr"], w["gq"], w["wqb"], w["gkvc"],
      w["gkvr"], w["wv"], w["wk"], tabs["cs"], tabs["mla_ct"], tabs["mla_st"], tabs["mla_t1"], tabs["mla_t2"])


def _pre_odd_kernel(z_ref, mod_ref, g_ref, wq_ref, wk_ref, wv_ref, gq_ref, gk_ref, ct_ref, st_ref,
                    q_ref, k_ref, v_ref):
    h = _norm_mod(z_ref[...], g_ref[...], mod_ref[0, 1:2, :], mod_ref[0, 0:1, :])
    hb = h.astype(BF16)
    cos = ct_ref[...]
    sin = st_ref[...]
    hd2 = GQA_HEAD_DIM // 2

    def norm_rope(xt, gain, scale):
        rs = lax.rsqrt(jnp.mean(xt * xt, axis=0, keepdims=True) + EPS)
        xn = xt * rs * gain
        x1 = xn[:hd2]
        x2 = xn[hd2:]
        return (x1 * cos - x2 * sin) * scale, (x1 * sin + x2 * cos) * scale

    qt = _nt(wq_ref[...], hb)
    for hd in range(GQA_HEADS):
        o = hd * GQA_HEAD_DIM
        r1, r2 = norm_rope(qt[o:o + GQA_HEAD_DIM], gq_ref[...], GQA_SCALE * LOG2E)
        q_ref[o:o + hd2, :] = r1.astype(BF16)
        q_ref[o + hd2:o + GQA_HEAD_DIM, :] = r2.astype(BF16)

    kt = _nt(wk_ref[...], hb)
    zero = jnp.zeros((DK - GQA_HEAD_DIM, TM), F32)
    for hd in range(GQA_KV_HEADS):
        o = hd * GQA_HEAD_DIM
        r1, r2 = norm_rope(kt[o:o + GQA_HEAD_DIM], gk_ref[...], 1.0)
        khead = jnp.concatenate([r1, r2, zero], axis=0)
        k_ref[hd] = khead.T.astype(BF16)

    v_ref[...] = _nt(wv_ref[...], hb).astype(BF16)


def _pre_odd(lay, z, mod, g_mix, w, tabs):
    d = z.shape[1]
    tokp = lay.TOKP
    col = lambda rows: pl.BlockSpec((rows, TM), lambda u: (0, u))
    hd2 = GQA_HEAD_DIM // 2
    in_specs = [
        pl.BlockSpec((TM, d), lambda u: (u, 0)),
        pl.BlockSpec((1, 6, d), lambda u: (lay.mrow(u), 0, 0)),
        _full(g_mix), _full(w["wq"]), _full(w["wk"]), _full(w["wv"]), _full(w["gq"]), _full(w["gk"]),
        pl.BlockSpec((hd2, TM), lambda u: (0, lay.ropeblk(u))),
        pl.BlockSpec((hd2, TM), lambda u: (0, lay.ropeblk(u))),
    ]
    out_specs = [
        col(GQA_HEADS * GQA_HEAD_DIM),
        pl.BlockSpec((GQA_KV_HEADS, TM, DK), lambda u: (0, u, 0)),
        col(GQA_KV_HEADS * DV),
    ]
    out_shape = [
        jax.ShapeDtypeStruct((GQA_HEADS * GQA_HEAD_DIM, tokp), BF16),
        jax.ShapeDtypeStruct((GQA_KV_HEADS, tokp, DK), BF16),
        jax.ShapeDtypeStruct((GQA_KV_HEADS * DV, tokp), BF16),
    ]
    return pl.pallas_call(
        _pre_odd_kernel, grid=(lay.NT,), in_specs=in_specs, out_specs=out_specs, out_shape=out_shape,
        compiler_params=_cparams(), name="pre_odd",
    )(z, mod, g_mix, w["wq"], w["wk"], w["wv"], w["gq"], w["gk"], tabs["gqa_ct"], tabs["gqa_st"])


def _attn_kernel(q_ref, kc_ref, kl_ref, vc_ref, vl_ref, o_ref, s0_ref, s1_ref, *, nq, kb, grp, dq, S, C):
    qi = pl.program_id(2)
    hb = kb * grp
    sbufs = (s0_ref, s1_ref)
    nchunk = S // TKC

    def scores(hh, buf, with_latent):
        kv = hh // grp
        q = q_ref[hh * dq:(hh + 1) * dq, :]
        sc = _dot(kc_ref[kv, :, :dq], q)
        buf[0:C, :] = sc
        m = jnp.max(sc, axis=0, keepdims=True)
        if with_latent:
            for c in range(nchunk):
                s = _dot(kl_ref[kv, c * TKC:(c + 1) * TKC, :dq], q)
                buf[C + c * TKC:C + (c + 1) * TKC, :] = s
                m = jnp.maximum(m, jnp.max(s, axis=0, keepdims=True))
        return m

    def values(hh, buf, m, with_latent):
        kv = hh // grp
        vrows = slice(kv * DV, (kv + 1) * DV)
        p = jnp.exp2(buf[0:C, :] - m)
        l = jnp.sum(p, axis=0, keepdims=True)
        o = _dot(vc_ref[vrows, :], p.astype(BF16))
        if with_latent:
            for c in range(nchunk):
                p = jnp.exp2(buf[C + c * TKC:C + (c + 1) * TKC, :] - m)
                l = l + jnp.sum(p, axis=0, keepdims=True)
                o = o + _dot(vl_ref[vrows, c * TKC:(c + 1) * TKC], p.astype(BF16))
        o_ref[hh * DV:(hh + 1) * DV, :] = (o * (1.0 / l)).astype(o_ref.dtype)

    def run(with_latent):
        m = scores(0, sbufs[0], with_latent)
        for hh in range(hb):
            m_next = scores(hh + 1, sbufs[(hh + 1) % 2], with_latent) if hh + 1 < hb else None
            values(hh, sbufs[hh % 2], m, with_latent)
            m = m_next

    @pl.when(qi < nq)
    def _():
        run(True)

    @pl.when(qi == nq)
    def _():
        run(False)


def _attention(lay, qt, k, vt, *, n_kv, grp, dq, kb):
    B, S, C = lay.B, lay.S, lay.C
    hb = kb * grp
    nq = S // TQ

    def qcol(b, qi):
        return jnp.where(qi == nq, lay.NL + b, b * nq + qi)

    in_specs = [
        pl.BlockSpec((hb * dq, TQ), lambda b, h, qi: (h, qcol(b, qi))),
        pl.BlockSpec((kb, C, DK), lambda b, h, qi: (h, lay.NL + b, 0)),
        pl.BlockSpec((kb, S, DK), lambda b, h, qi: (h, b, 0)),
        pl.BlockSpec((kb * DV, C), lambda b, h, qi: (h, lay.NL + b)),
        pl.BlockSpec((kb * DV, S), lambda b, h, qi: (h, b)),
    ]
    return pl.pallas_call(
        functools.partial(_attn_kernel, nq=nq, kb=kb, grp=grp, dq=dq, S=S, C=C),
        grid=(B, n_kv // kb, nq + 1),
        in_specs=in_specs,
        out_specs=pl.BlockSpec((hb * DV, TQ), lambda b, h, qi: (h, qcol(b, qi))),
        out_shape=jax.ShapeDtypeStruct((n_kv * grp * DV, lay.TOKP), BF16),
        scratch_shapes=[pltpu.VMEM((C + S, TQ), F32), pltpu.VMEM((C + S, TQ), F32)],
        compiler_params=_cparams(),
        name="attention",
    )(qt, k, k, vt, vt)


def _dft_kernel(a_ref, b_ref, c_ref, s_ref, o_ref):
    o_ref[...] = (_dot(a_ref[...], c_ref[...]) - _dot(b_ref[...], s_ref[...])).astype(o_ref.dtype)


def _seq_dft(lay, abt, cn, sn, cc, sc):
    B, S, C = lay.B, lay.S, lay.C
    fw = FOURIER_WIDTH
    nj = S // TN_DFT
    ylat = pl.pallas_call(
        _dft_kernel,
        grid=(nj, B),
        in_specs=[
            pl.BlockSpec((fw, S), lambda j, b: (0, b)),
            pl.BlockSpec((fw, S), lambda j, b: (1, b)),
            pl.BlockSpec((S, TN_DFT), lambda j, b: (0, j)),
            pl.BlockSpec((S, TN_DFT), lambda j, b: (0, j)),
        ],
        out_specs=pl.BlockSpec((fw, TN_DFT), lambda j, b: (0, b * nj + j)),
        out_shape=jax.ShapeDtypeStruct((fw, B * S), BF16),
        compiler_params=_cparams(),
        name="seq_dft",
    )(abt, abt, cn, sn)
    yctx = pl.pallas_call(
        _dft_kernel,
        grid=(B,),
        in_specs=[
            pl.BlockSpec((fw, C), lambda b: (0, lay.NL + b)),
            pl.BlockSpec((fw, C), lambda b: (1, lay.NL + b)),
            pl.BlockSpec((C, C), lambda b: (0, 0)),
            pl.BlockSpec((C, C), lambda b: (0, 0)),
        ],
        out_specs=pl.BlockSpec((fw, C), lambda b: (0, b)),
        out_shape=jax.ShapeDtypeStruct((fw, B * C), BF16),
        compiler_params=_cparams(),
        name="ctx_dft",
    )(abt, abt, cc, sc)
    return ylat, yctx


def _route_rows(lg, rbias):
    s = jax.nn.sigmoid(lg)
    sel = s + rbias
    rows = [sel[e:e + 1, :] for e in range(N_EXPERTS)]
    best = None
    bg = None
    for g in range(N_GROUPS):
        v = rows[4 * g:4 * g + 4]
        sc = None
        for i in range(4):
            for j in range(i + 1, 4):
                ps = v[i] + v[j]
                sc = ps if sc is None else jnp.maximum(sc, ps)
        if best is None:
            best, bg = sc, jnp.zeros_like(sc)
        else:
            upd = sc > best
            bg = jnp.where(upd, float(g), bg)
            best = jnp.where(upd, sc, best)
    v = []
    for i in range(4):
        acc = rows[i]
        for g in range(1, N_GROUPS):
            acc = jnp.where(bg == float(g), rows[4 * g + i], acc)
        v.append(acc)
    i0 = jnp.zeros_like(bg)
    b0 = v[0]
    for i in range(1, 4):
        upd = v[i] > b0
        i0 = jnp.where(upd, float(i), i0)
        b0 = jnp.where(upd, v[i], b0)
    w = [jnp.where(i0 == float(i), -jnp.inf, v[i]) for i in range(4)]
    i1 = jnp.zeros_like(bg)
    b1 = w[0]
    for i in range(1, 4):
        upd = w[i] > b1
        i1 = jnp.where(upd, float(i), i1)
        b1 = jnp.where(upd, w[i], b1)
    lo = jnp.minimum(i0, i1)
    hi = jnp.maximum(i0, i1)
    pbase = jnp.where(lo == 0.0, 0.0, jnp.where(lo == 1.0, 3.0, 5.0))
    return bg * 6.0 + pbase + hi - lo - 1.0


def _post_kernel(z_ref, mod_ref, a_ref, actx_ref, b_ref, wa_ref, wb_ref, g_ref, rwh_ref, rwl_ref, rb_ref, ut_ref,
                 zo_ref, h2_ref, info_ref, cnt_ref, *, nl_split):
    a = a_ref[...]
    if nl_split is not None:
        a = jnp.where(pl.program_id(0) < nl_split, a, actx_ref[...])
    attn = _tn(a, wa_ref[...]) + _tn(b_ref[...], wb_ref[...])
    x1 = z_ref[...] + mod_ref[0, 2:3, :] * attn
    zo_ref[...] = x1
    h = _norm_mod(x1, g_ref[...], mod_ref[0, 4:5, :], mod_ref[0, 3:4, :])
    for k in range(8):
        h2_ref[:, k, :] = h[:, k * 128:(k + 1) * 128]
    hh = h.astype(BF16)
    hl = (h - hh.astype(F32)).astype(BF16)
    lg = _nt(rwh_ref[...], hh) + _nt(rwh_ref[...], hl) + _nt(rwl_ref[...], hh)
    cls = _route_rows(lg, rb_ref[...])
    ohf = (lax.broadcasted_iota(jnp.int32, (32, TM), 0).astype(F32) == cls).astype(F32)
    rank_all = _dot(ohf.astype(BF16), ut_ref[...])
    info_ref[0:1, :] = cls
    info_ref[1:2, :] = jnp.sum(ohf * rank_all, axis=0, keepdims=True)
    info_ref[2:8, :] = jnp.zeros((6, TM), F32)
    cnt_ref[0] = jnp.broadcast_to(jnp.sum(ohf, axis=1, keepdims=True), (32, 128))


def _post(lay, z, mod, src_a, src_a_ctx, src_b, blk_a, blk_b, wa, wb, g_ffn, rwh, rwl, rbias, ut):
    d = z.shape[1]
    nt = lay.NT
    fw = FOURIER_WIDTH
    if src_a_ctx is None:
        nl_split = None
        src_a_ctx = jnp.zeros((fw, TM), BF16)
        spec_a = pl.BlockSpec((fw, TM), lambda u: (blk_a, u))
        spec_actx = pl.BlockSpec((fw, TM), lambda u: (0, 0))
    else:
        nl_split = lay.NL
        spec_a = pl.BlockSpec((fw, TM), lambda u: (blk_a, jnp.minimum(u, lay.NL - 1)))
        spec_actx = pl.BlockSpec((fw, TM), lambda u: (0, jnp.maximum(u - lay.NL, 0)))
    in_specs = [
        pl.BlockSpec((TM, d), lambda u: (u, 0)),
        pl.BlockSpec((1, 6, d), lambda u: (lay.mrow(u), 0, 0)),
        spec_a,
        spec_actx,
        pl.BlockSpec((fw, TM), lambda u: (blk_b, u)),
        _full(wa), _full(wb), _full(g_ffn), _full(rwh), _full(rwl), _full(rbias), _full(ut),
    ]
    out_specs = [
        pl.BlockSpec((TM, d), lambda u: (u, 0)),
        pl.BlockSpec((TM, 8, 128), lambda u: (u, 0, 0)),
        pl.BlockSpec((8, TM), lambda u: (0, u)),
        pl.BlockSpec((1, 32, 128), lambda u: (u, 0, 0)),
    ]
    out_shape = [
        jax.ShapeDtypeStruct((lay.TOKP, d), F32),
        jax.ShapeDtypeStruct((lay.TOKP, 8, 128), F32),
        jax.ShapeDtypeStruct((8, lay.TOKP), F32),
        jax.ShapeDtypeStruct((nt, 32, 128), F32),
    ]
    return pl.pallas_call(
        functools.partial(_post_kernel, nl_split=nl_split),
        grid=(nt,), in_specs=in_specs, out_specs=out_specs, out_shape=out_shape,
        compiler_params=_cparams(), name="post",
    )(z, mod, src_a, src_a_ctx, src_b, wa, wb, g_ffn, rwh, rwl, rbias, ut)


def _dispatch_kernel(pos_ref, h2_ref, xs_in_ref, xs_ref, sem):
    del xs_in_ref
    row0 = pl.program_id(0) * TM

    def copy(r):
        return pltpu.make_async_copy(h2_ref.at[row0 + r], xs_ref.at[pos_ref[0, 0, r]], sem)

    def start(r, c):
        copy(r).start()
        return c

    def wait(r, c):
        copy(r).wait()
        return c

    lax.fori_loop(0, TM, start, 0)
    lax.fori_loop(0, TM, wait, 0)


def _dispatch(lay, pos3, h2, xs0):
    return pl.pallas_call(
        _dispatch_kernel,
        grid=(lay.NT,),
        in_specs=[
            pl.BlockSpec((1, 1, TM), lambda u: (u, 0, 0), memory_space=pltpu.SMEM),
            pl.BlockSpec(memory_space=pl.ANY),
            pl.BlockSpec(memory_space=pl.ANY),
        ],
        out_specs=pl.BlockSpec(memory_space=pl.ANY),
        out_shape=jax.ShapeDtypeStruct(xs0.shape, xs0.dtype),
        scratch_shapes=[pltpu.SemaphoreType.DMA(())],
        input_output_aliases={2: 0},
        name="moe_dispatch",
    )(pos3, h2, xs0)


def _expert_kernel(ea_ref, eb_ref, nu_ref, x_ref, rw_ref, wga_ref, wgb_ref, wda_ref, wdb_ref, y_ref):
    i = pl.program_id(0)

    @pl.when(i < nu_ref[0])
    def _():
        x = jnp.concatenate([x_ref[:, k, :] for k in range(8)], axis=1)
        xb = x.astype(BF16)
        s = jax.nn.sigmoid(_dot(xb, rw_ref[...]))
        lane = lax.broadcasted_iota(jnp.int32, s.shape, 1)
        sa = jnp.sum(jnp.where(lane == ea_ref[i], s, 0.0), axis=1, keepdims=True)
        sb = jnp.sum(jnp.where(lane == eb_ref[i], s, 0.0), axis=1, keepdims=True)
        inv = 1.0 / (sa + sb)

        def hidden(w_ref, gate):
            gu = _dot(xb, w_ref[0])
            g = gu[:, :D_EXPERT]
            return (g * jax.nn.sigmoid(g) * gu[:, D_EXPERT:] * gate).astype(BF16)

        y = _dot(hidden(wga_ref, sa * inv), wda_ref[0]) + _dot(hidden(wgb_ref, sb * inv), wdb_ref[0])
        for k in range(8):
            y_ref[:, k, :] = y[:, k * 128:(k + 1) * 128]

    @pl.when(i >= nu_ref[0])
    def _():
        y_ref[...] = jnp.zeros(y_ref.shape, F32)


def _experts(xs, ea, eb, nused, rw, wgu, wd):
    pmax = xs.shape[0]
    ntile = pmax // TMM
    d = wd.shape[2]
    grid_spec = pltpu.PrefetchScalarGridSpec(
        num_scalar_prefetch=3,
        grid=(ntile,),
        in_specs=[
            pl.BlockSpec((TMM, 8, 128), lambda i, ea, eb, nu: (jnp.minimum(i, nu[0] - 1), 0, 0)),
            pl.BlockSpec(rw.shape, lambda i, ea, eb, nu: (0, 0)),
            pl.BlockSpec((1, d, 2 * D_EXPERT), lambda i, ea, eb, nu: (ea[i], 0, 0)),
            pl.BlockSpec((1, d, 2 * D_EXPERT), lambda i, ea, eb, nu: (eb[i], 0, 0)),
            pl.BlockSpec((1, D_EXPERT, d), lambda i, ea, eb, nu: (ea[i], 0, 0)),
            pl.BlockSpec((1, D_EXPERT, d), lambda i, ea, eb, nu: (eb[i], 0, 0)),
        ],
        out_specs=pl.BlockSpec((TMM, 8, 128), lambda i, ea, eb, nu: (i, 0, 0)),
    )
    return pl.pallas_call(
        _expert_kernel, grid_spec=grid_spec,
        out_shape=jax.ShapeDtypeStruct(xs.shape, F32),
        compiler_params=_cparams(), name="moe_experts",
    )(ea, eb, nused, xs, rw, wgu, wgu, wd, wd)


def _combine_kernel(pos_ref, ys_ref, z_ref, mod_ref, gf_ref, o_ref, buf, sem, *, final):
    def copy(r):
        return pltpu.make_async_copy(ys_ref.at[pos_ref[0, 0, r]], buf.at[r], sem)

    def start(r, c):
        copy(r).start()
        return c

    def wait(r, c):
        copy(r).wait()
        return c

    lax.fori_loop(0, TM, start, 0)
    lax.fori_loop(0, TM, wait, 0)
    y = jnp.concatenate([buf[:, k, :] for k in range(8)], axis=1)
    x2 = z_ref[...] + mod_ref[0, 5:6, :] * y
    if final:
        ms = jnp.mean(x2 * x2, axis=-1, keepdims=True)
        x2 = x2 * lax.rsqrt(ms + EPS) * gf_ref[...]
    o_ref[...] = x2


def _combine(lay, pos3, ys, z, mod, g_final, final):
    d = z.shape[1]
    nt = lay.NL if final else lay.NT
    return pl.pallas_call(
        functools.partial(_combine_kernel, final=final),
        grid=(nt,),
        in_specs=[
            pl.BlockSpec((1, 1, TM), lambda u: (u, 0, 0), memory_space=pltpu.SMEM),
            pl.BlockSpec(memory_space=pl.ANY),
            pl.BlockSpec((TM, d), lambda u: (u, 0)),
            pl.BlockSpec((1, 6, d), lambda u: (lay.mrow(u), 0, 0)),
            _full(g_final),
        ],
        out_specs=pl.BlockSpec((TM, d), lambda u: (u, 0)),
        out_shape=jax.ShapeDtypeStruct((nt * TM, d), F32),
        scratch_shapes=[pltpu.VMEM((TM, 8, 128), F32), pltpu.SemaphoreType.DMA(())],
        compiler_params=_cparams(), name="moe_combine",
    )(pos3, ys, z, mod, g_final)


def _plan(info, cnt, nt, ntile_max):
    cls = info[0].astype(jnp.int32).reshape(nt, TM)
    rank = info[1].astype(jnp.int32).reshape(nt, TM)
    cnt = cnt[:, :N_CLASSES, 0].astype(jnp.int32)
    tot = jnp.sum(cnt, axis=0)
    ntile_c = (tot + TMM - 1) // TMM
    tile_end = jnp.cumsum(ntile_c)
    class_off = (tile_end - ntile_c) * TMM
    tile_off = jnp.cumsum(cnt, axis=0) - cnt
    base = class_off[None, :] + tile_off
    onehot = cls[:, :, None] == jnp.arange(N_CLASSES, dtype=jnp.int32)[None, None, :]
    pos = jnp.sum(jnp.where(onehot, base[:, None, :], 0), axis=-1) + rank
    nused = tile_end[-1]
    ti = jnp.minimum(jnp.arange(ntile_max, dtype=jnp.int32), nused - 1)
    tcls = jnp.sum((ti[:, None] >= tile_end[None, :]).astype(jnp.int32), axis=1)
    grp, pair = tcls // 6, tcls % 6
    lo = jnp.asarray(_PAIR_LO, jnp.int32)
    hi = jnp.asarray(_PAIR_HI, jnp.int32)
    onep = pair[:, None] == jnp.arange(6, dtype=jnp.int32)[None, :]
    ea = 4 * grp + jnp.sum(jnp.where(onep, lo[None, :], 0), axis=1)
    eb = 4 * grp + jnp.sum(jnp.where(onep, hi[None, :], 0), axis=1)
    return pos.reshape(nt, 1, TM), ea, eb, nused.reshape(1)


def _dft_mats(n, scale):
    j = lax.broadcasted_iota(jnp.int32, (n, n), 0)
    k = lax.broadcasted_iota(jnp.int32, (n, n), 1)
    ang = ((j * k) % n).astype(F32) * (2.0 * math.pi / n)
    return (jnp.cos(ang) * scale).astype(BF16), (jnp.sin(ang) * scale).astype(BF16)


def _rope_angles(S, dim):
    rows = S // GRID_W
    row_id = jnp.repeat(jnp.arange(rows, dtype=F32), GRID_W)
    col_id = jnp.tile(jnp.arange(GRID_W, dtype=F32), rows)
    n_freq = dim // 4
    inv = ROPE_BASE ** (-jnp.arange(n_freq, dtype=F32) / n_freq)
    return jnp.concatenate([row_id[:, None] * inv, col_id[:, None] * inv], axis=-1)


def _tables(lay):
    S, C = lay.S, lay.C
    tabs = {}
    cc, sc = _dft_mats(FOURIER_GROUP_DIM, FOURIER_GROUP_DIM ** -0.5)
    tabs["cs"] = jnp.concatenate([cc, sc], axis=0)
    tabs["cn"], tabs["sn"] = _dft_mats(S, S ** -0.5)
    tabs["cctx"], tabs["sctx"] = _dft_mats(C, C ** -0.5)
    for name, dim in (("mla", MLA_ROPE_DIM), ("gqa", GQA_HEAD_DIM)):
        ang = _rope_angles(S, dim)
        cos = jnp.concatenate([jnp.ones((C, dim // 2), F32), jnp.cos(ang)], axis=0)
        sin = jnp.concatenate([jnp.zeros((C, dim // 2), F32), jnp.sin(ang)], axis=0)
        tabs[name + "_ct"] = cos.T
        tabs[name + "_st"] = sin.T
        if name == "mla":
            pad = jnp.zeros((C + S, 128 - dim), F32)
            tabs["mla_t1"] = jnp.concatenate([cos, cos, pad], axis=1)
            tabs["mla_t2"] = jnp.concatenate([sin, -sin, pad], axis=1)
    tabs["ut"] = (lax.broadcasted_iota(jnp.int32, (TM, TM), 0)
                  < lax.broadcasted_iota(jnp.int32, (TM, TM), 1)).astype(BF16)
    return tabs


def _col(g, rows):
    return jnp.broadcast_to(g.astype(F32)[:, None], (rows, TM))


def _even_weights(w_in, q_norm_g, w_qb, kv_norm_g, w_kvb, w_out):
    d = w_in.shape[0]
    o1 = FOURIER_WIDTH
    o2 = o1 + MLA_Q_RANK
    o3 = o2 + MLA_KV_RANK
    bf = lambda a: a.astype(BF16)
    w = {}
    w["wf"] = bf(w_in[:, :o1].T)
    w["wqa"] = bf(w_in[:, o1:o2].T)
    w["wkvat"] = bf(w_in[:, o2:o3].T)
    w["wkva"] = bf(w_in[:, o2:o3])
    w["wpr"] = bf(jnp.concatenate([w_in[:, o3:], jnp.zeros((d, 128 - MLA_ROPE_DIM), F32)], axis=1))
    w["gq"] = _col(q_norm_g, MLA_Q_RANK)
    w["wqb"] = bf(w_qb.T)
    w["gkvc"] = _col(kv_norm_g, MLA_KV_RANK)
    w["gkvr"] = kv_norm_g.astype(F32)[None, :]
    kvb = w_kvb.reshape(MLA_KV_RANK, MLA_HEADS, MLA_NOPE_DIM + MLA_V_DIM)
    w["wv"] = bf(kvb[:, :, MLA_NOPE_DIM:].reshape(MLA_KV_RANK, MLA_HEADS * MLA_V_DIM).T)
    w1 = jnp.concatenate([kvb[:, :, :MLA_NOPE_DIM],
                          jnp.zeros((MLA_KV_RANK, MLA_HEADS, DK - MLA_NOPE_DIM), F32)], axis=2)
    w1 = w1.reshape(MLA_KV_RANK, MLA_HEADS * DK)
    eye = jnp.eye(MLA_ROPE_DIM, dtype=F32)
    place = jnp.zeros((128, DK), F32).at[:MLA_ROPE_DIM, MLA_NOPE_DIM:MLA_QK_DIM].set(eye)
    swap = jnp.roll(eye, MLA_ROPE_DIM // 2, axis=1)
    place_s = jnp.zeros((128, DK), F32).at[:MLA_ROPE_DIM, MLA_NOPE_DIM:MLA_QK_DIM].set(swap)
    w["wk"] = bf(jnp.concatenate([w1, jnp.tile(place, (1, MLA_HEADS)), jnp.tile(place_s, (1, MLA_HEADS))], axis=0))
    w["wout_a"] = bf(w_out[:o1])
    w["wout_b"] = bf(w_out[o1:])
    return w


def _odd_weights(w_qkv, q_norm_g, k_norm_g, w_out):
    bf = lambda a: a.astype(BF16)
    qw = GQA_HEADS * GQA_HEAD_DIM
    kw = GQA_KV_HEADS * GQA_HEAD_DIM
    w = {}
    w["wq"] = bf(w_qkv[:, :qw].T)
    w["wk"] = bf(w_qkv[:, qw:qw + kw].T)
    w["wv"] = bf(w_qkv[:, qw + kw:].T)
    w["gq"] = _col(q_norm_g, GQA_HEAD_DIM)
    w["gk"] = _col(k_norm_g, GQA_HEAD_DIM)
    w["wout_a"] = bf(w_out[:FOURIER_WIDTH])
    w["wout_b"] = bf(w_out[FOURIER_WIDTH:])
    return w


def kernel(x, c, ctx, c_ctx, ada_w, ada_b, norm_mix_g, norm_ffn_g, ev_w_in, ev_q_norm_g, ev_w_qb, ev_kv_norm_g,
           ev_w_kvb, ev_w_out, od_w_qkv, od_q_norm_g, od_k_norm_g, od_w_out, router_w, router_b, exp_w_gate,
           exp_w_up, exp_w_down, final_norm_g):
    B, S, d = x.shape
    C = ctx.shape[1]
    depth = ada_w.shape[0]
    lay = _Layout(B, S, C)
    assert B + 1 <= 16
    tabs = _tables(lay)

    cond = jnp.concatenate([c, c_ctx[None, :], jnp.zeros((16 - B - 1, d), F32)], axis=0)
    mods = _modulation(cond, ada_w, ada_b).reshape(depth, 16, 6, d)

    z = jnp.concatenate([x.reshape(B * S, d), ctx.reshape(B * C, d)], axis=0)

    rwt = router_w.T.astype(F32)
    rwh = rwt.astype(BF16)
    rwl = (rwt - rwh.astype(F32)).astype(BF16)
    rbias = jnp.broadcast_to(router_b.astype(F32)[:, None], (N_EXPERTS, TM))
    rw_nat = jnp.concatenate([router_w, jnp.zeros((d, 128 - N_EXPERTS), F32)], axis=1).astype(BF16)
    wgu_all = jnp.concatenate([exp_w_gate, exp_w_up], axis=-1).astype(BF16)
    wd_all = exp_w_down.astype(BF16)
    ntile_max = lay.NT * TM // TMM + N_CLASSES

    out = None
    for i in range(depth):
        last = i == depth - 1
        mod = mods[i]
        g_mix = norm_mix_g[i][None, :]
        g_ffn = norm_ffn_g[i][None, :]
        j = i // 2
        if i % 2 == 0:
            w = _even_weights(ev_w_in[j], ev_q_norm_g[j], ev_w_qb[j], ev_kv_norm_g[j], ev_w_kvb[j], ev_w_out[j])
            abt, qt, k, vt = _pre_even(lay, z, mod, g_mix, w, tabs)
            ot = _attention(lay, qt, k, vt, n_kv=MLA_HEADS, grp=1, dq=DK, kb=4)
            src_a, src_a_ctx = _seq_dft(lay, abt, tabs["cn"], tabs["sn"], tabs["cctx"], tabs["sctx"])
            src_b, blk_a, blk_b = ot, 0, 0
        else:
            w = _odd_weights(od_w_qkv[j], od_q_norm_g[j], od_k_norm_g[j], od_w_out[j])
            qt, k, vt = _pre_odd(lay, z, mod, g_mix, w, tabs)
            ot = _attention(lay, qt, k, vt, n_kv=GQA_KV_HEADS, grp=GQA_GROUP, dq=GQA_HEAD_DIM, kb=2)
            src_a, src_a_ctx, src_b, blk_a, blk_b = ot, None, ot, 0, 1
        z1, h2, info, cnt = _post(lay, z, mod, src_a, src_a_ctx, src_b, blk_a, blk_b, w["wout_a"], w["wout_b"],
                                  g_ffn, rwh, rwl, rbias, tabs["ut"])
        pos3, ea, eb, nused = _plan(info, cnt, lay.NT, ntile_max)
        xs0 = jnp.zeros((ntile_max * TMM, 8, 128), F32)
        xs = _dispatch(lay, pos3, h2, xs0)
        ys = _experts(xs, ea, eb, nused, rw_nat, wgu_all[i], wd_all[i])
        if last:
            out = _combine(lay, pos3, ys, z1, mod, final_norm_g[None, :], True)
        else:
            z = _combine(lay, pos3, ys, z1, mod, final_norm_g[None, :], False)
    return out.reshape(B, S, d)
```

```python
import functools
import math

import jax
import jax.numpy as jnp
from jax import lax
from jax.experimental import pallas as pl
from jax.experimental.pallas import tpu as pltpu

F32 = jnp.float32
BF16 = jnp.bfloat16

TM = 256
TQ = 256
TKC = 512
TMM = 256
TN_DFT = 512
EPS = 1e-6
LOG2E = 1.4426950408889634
ROPE_BASE = 10000.0
GRID_W = 64

FOURIER_GROUPS = 4
FOURIER_GROUP_DIM = 128
FOURIER_WIDTH = 512
MLA_HEADS = 8
MLA_NOPE_DIM = 64
MLA_ROPE_DIM = 32
MLA_V_DIM = 64
MLA_QK_DIM = 96
MLA_Q_RANK = 256
MLA_KV_RANK = 128
MLA_SCALE = MLA_QK_DIM ** -0.5
GQA_HEADS = 16
GQA_KV_HEADS = 4
GQA_HEAD_DIM = 64
GQA_GROUP = 4
GQA_SCALE = GQA_HEAD_DIM ** -0.5
N_EXPERTS = 16
N_GROUPS = 4
N_CLASSES = 24
D_EXPERT = 512
DK = 128
DV = 64
VMEM_LIMIT = 56 * 1024 * 1024

_PAIR_LO = (0, 0, 0, 1, 1, 2)
_PAIR_HI = (1, 2, 3, 2, 3, 3)


def _nt(a, b):
    return lax.dot_general(a, b, (((1,), (1,)), ((), ())), preferred_element_type=F32)


def _tn(a, b):
    return lax.dot_general(a, b, (((0,), (0,)), ((), ())), preferred_element_type=F32)


def _dot(a, b):
    return jnp.dot(a, b, preferred_element_type=F32)


class _Layout:
    def __init__(self, B, S, C):
        assert C == TM and S % TM == 0 and S % TN_DFT == 0
        self.B, self.S, self.C = B, S, C
        self.SB = S // TM
        self.NL = B * self.SB
        self.NT = self.NL + B
        self.TOKP = self.NT * TM

    def mrow(self, u):
        return jnp.where(u < self.NL, u // self.SB, self.B)

    def ropeblk(self, u):
        return jnp.where(u < self.NL, 1 + u % self.SB, 0)


def _cparams():
    return pltpu.CompilerParams(vmem_limit_bytes=VMEM_LIMIT)


def _full(a):
    return pl.BlockSpec(a.shape, lambda *_: (0,) * a.ndim)


def _mod_kernel(s_ref, w_ref, b_ref, o_ref):
    s = s_ref[...]
    a = s * jax.nn.sigmoid(s)
    ah = a.astype(BF16)
    al = (a - ah.astype(F32)).astype(BF16)
    w = w_ref[0]
    wh = w.astype(BF16)
    wl = (w - wh.astype(F32)).astype(BF16)
    o_ref[0] = _dot(ah, wh) + _dot(ah, wl) + _dot(al, wh) + b_ref[0]


def _modulation(cond, ada_w, ada_b):
    depth, d, w6 = ada_w.shape
    tn = 768
    return pl.pallas_call(
        _mod_kernel,
        grid=(depth, w6 // tn),
        in_specs=[
            pl.BlockSpec((16, d), lambda i, j: (0, 0)),
            pl.BlockSpec((1, d, tn), lambda i, j: (i, 0, j)),
            pl.BlockSpec((1, 1, tn), lambda i, j: (i, 0, j)),
        ],
        out_specs=pl.BlockSpec((1, 16, tn), lambda i, j: (i, 0, j)),
        out_shape=jax.ShapeDtypeStruct((depth, 16, w6), F32),
        compiler_params=_cparams(),
        name="modulation",
    )(cond, ada_w, ada_b.reshape(depth, 1, w6))


def _norm_mod(z, g, sc, sh):
    ms = jnp.mean(z * z, axis=-1, keepdims=True)
    return (z * lax.rsqrt(ms + EPS) * g) * (1.0 + sc) + sh


def _rows_to_lanes(ref):
    return jnp.concatenate([ref[:, k, :] for k in range(8)], axis=1)


def _lanes_to_rows(ref, val):
    for k in range(8):
        ref[:, k, :] = val[:, k * 128:(k + 1) * 128]


def _zin_specs(lay, zin, d):
    row = pl.BlockSpec((TM, d), lambda u: (u, 0))
    if len(zin) == 1:
        return [row], [], []
    specs = [row, pl.BlockSpec((TM, 8, 128), lambda u: (u, 0, 0)),
             pl.BlockSpec((1, 6, d), lambda u: (lay.mrow(u), 0, 0))]
    return specs, [row], [jax.ShapeDtypeStruct((lay.TOKP, d), F32)]


def _zin_value(zin_refs, zout_refs):
    if len(zin_refs) == 1:
        return zin_refs[0][...]
    z_ref, y_ref, modp_ref = zin_refs
    z = z_ref[...] + modp_ref[0, 5:6, :] * _rows_to_lanes(y_ref)
    zout_refs[0][...] = z
    return z


def _pre_even_kernel(*refs, n_zin):
    (mod_ref, g_ref, wf_ref, wqa_ref, wkvat_ref, wkva_ref, wpr_ref, gq_ref, wqb_ref, gkvc_ref, gkvr_ref, wv_ref,
     wk_ref, cs_ref, ct_ref, st_ref, t1_ref, t2_ref, ab_ref, q_ref, k_ref, v_ref) = refs[n_zin:n_zin + 22]
    z = _zin_value(refs[:n_zin], refs[n_zin + 22:])
    h = _norm_mod(z, g_ref[...], mod_ref[0, 1:2, :], mod_ref[0, 0:1, :])
    hb = h.astype(BF16)

    ft = _nt(wf_ref[...], hb).astype(BF16)
    gd = FOURIER_GROUP_DIM
    for g in range(FOURIER_GROUPS):
        ab = _dot(cs_ref[...], ft[g * gd:(g + 1) * gd, :])
        ab_ref[g * gd:(g + 1) * gd, :] = ab[:gd].astype(BF16)
        ab_ref[FOURIER_WIDTH + g * gd:FOURIER_WIDTH + (g + 1) * gd, :] = ab[gd:].astype(BF16)

    pq = _nt(wqa_ref[...], hb)
    rs = lax.rsqrt(jnp.mean(pq * pq, axis=0, keepdims=True) + EPS)
    qn = (pq * rs * gq_ref[...]).astype(BF16)
    qt = _dot(wqb_ref[...], qn) * (MLA_SCALE * LOG2E)
    cos = ct_ref[...]
    sin = st_ref[...]
    hr = MLA_ROPE_DIM // 2
    for hd in range(MLA_HEADS):
        o = hd * MLA_QK_DIM
        r = hd * DK
        x1 = qt[o + MLA_NOPE_DIM:o + MLA_NOPE_DIM + hr, :]
        x2 = qt[o + MLA_NOPE_DIM + hr:o + MLA_QK_DIM, :]
        q_ref[r:r + MLA_NOPE_DIM, :] = qt[o:o + MLA_NOPE_DIM, :].astype(BF16)
        q_ref[r + MLA_NOPE_DIM:r + MLA_NOPE_DIM + hr, :] = (x1 * cos - x2 * sin).astype(BF16)
        q_ref[r + MLA_NOPE_DIM + hr:r + MLA_QK_DIM, :] = (x1 * sin + x2 * cos).astype(BF16)
        q_ref[r + MLA_QK_DIM:r + DK, :] = jnp.zeros((DK - MLA_QK_DIM, TM), BF16)

    pkvt = _nt(wkvat_ref[...], hb)
    rst = lax.rsqrt(jnp.mean(pkvt * pkvt, axis=0, keepdims=True) + EPS)
    kvnt = (pkvt * rst * gkvc_ref[...]).astype(BF16)
    v_ref[...] = _dot(wv_ref[...], kvnt).astype(BF16)

    pkv = _dot(hb, wkva_ref[...])
    rsn = lax.rsqrt(jnp.mean(pkv * pkv, axis=-1, keepdims=True) + EPS)
    kvn = (pkv * rsn * gkvr_ref[...]).astype(BF16)
    pr = _dot(hb, wpr_ref[...])
    lhs = jnp.concatenate([kvn, (pr * t1_ref[...]).astype(BF16), (pr * t2_ref[...]).astype(BF16)], axis=1)
    kf = _dot(lhs, wk_ref[...])
    for hd in range(MLA_HEADS):
        k_ref[hd] = kf[:, hd * DK:(hd + 1) * DK].astype(BF16)


def _pre_even(lay, zin, mod, g_mix, w, tabs):
    d = mod.shape[-1]
    tokp = lay.TOKP
    col = lambda rows: pl.BlockSpec((rows, TM), lambda u: (0, u))
    hr = MLA_ROPE_DIM // 2
    zin_specs, zout_specs, zout_shape = _zin_specs(lay, zin, d)
    in_specs = zin_specs + [
        pl.BlockSpec((1, 6, d), lambda u: (lay.mrow(u), 0, 0)),
        _full(g_mix),
        _full(w["wf"]), _full(w["wqa"]), _full(w["wkvat"]), _full(w["wkva"]), _full(w["wpr"]),
        _full(w["gq"]), _full(w["wqb"]), _full(w["gkvc"]), _full(w["gkvr"]), _full(w["wv"]), _full(w["wk"]),
        _full(tabs["cs"]),
        pl.BlockSpec((hr, TM), lambda u: (0, lay.ropeblk(u))),
        pl.BlockSpec((hr, TM), lambda u: (0, lay.ropeblk(u))),
        pl.BlockSpec((TM, 128), lambda u: (lay.ropeblk(u), 0)),
        pl.BlockSpec((TM, 128), lambda u: (lay.ropeblk(u), 0)),
    ]
    out_specs = [
        col(2 * FOURIER_WIDTH),
        col(MLA_HEADS * DK),
        pl.BlockSpec((MLA_HEADS, TM, DK), lambda u: (0, u, 0)),
        col(MLA_HEADS * DV),
    ]
    out_shape = [
        jax.ShapeDtypeStruct((2 * FOURIER_WIDTH, tokp), BF16),
        jax.ShapeDtypeStruct((MLA_HEADS * DK, tokp), BF16),
        jax.ShapeDtypeStruct((MLA_HEADS, tokp, DK), BF16),
        jax.ShapeDtypeStruct((MLA_HEADS * DV, tokp), BF16),
    ]
    return pl.pallas_call(
        functools.partial(_pre_even_kernel, n_zin=len(zin)),
        grid=(lay.NT,), in_specs=in_specs, out_specs=out_specs + zout_specs, out_shape=out_shape + zout_shape,
        compiler_params=_cparams(), name="pre_even",
    )(*zin, mod, g_mix, w["wf"], w["wqa"], w["wkvat"], w["wkva"], w["wpr"], w["gq"], w["wqb"], w["gkvc"],
      w["gkvr"], w["wv"], w["wk"], tabs["cs"], tabs["mla_ct"], tabs["mla_st"], tabs["mla_t1"], tabs["mla_t2"])


def _pre_odd_kernel(*refs, n_zin):
    (mod_ref, g_ref, wq_ref, wk_ref, wv_ref, gq_ref, gk_ref, ct_ref, st_ref,
     q_ref, k_ref, v_ref) = refs[n_zin:n_zin + 12]
    z = _zin_value(refs[:n_zin], refs[n_zin + 12:])
    h = _norm_mod(z, g_ref[...], mod_ref[0, 1:2, :], mod_ref[0, 0:1, :])
    hb = h.astype(BF16)
    cos = ct_ref[...]
    sin = st_ref[...]
    hd2 = GQA_HEAD_DIM // 2

    def norm_rope(xt, gain, scale):
        rs = lax.rsqrt(jnp.mean(xt * xt, axis=0, keepdims=True) + EPS)
        xn = xt * rs * gain
        x1 = xn[:hd2]
        x2 = xn[hd2:]
        return (x1 * cos - x2 * sin) * scale, (x1 * sin + x2 * cos) * scale

    qt = _nt(wq_ref[...], hb)
    for hd in range(GQA_HEADS):
        o = hd * GQA_HEAD_DIM
        r1, r2 = norm_rope(qt[o:o + GQA_HEAD_DIM], gq_ref[...], GQA_SCALE * LOG2E)
        q_ref[o:o + hd2, :] = r1.astype(BF16)
        q_ref[o + hd2:o + GQA_HEAD_DIM, :] = r2.astype(BF16)

    kt = _nt(wk_ref[...], hb)
    zero = jnp.zeros((DK - GQA_HEAD_DIM, TM), F32)
    for hd in range(GQA_KV_HEADS):
        o = hd * GQA_HEAD_DIM
        r1, r2 = norm_rope(kt[o:o + GQA_HEAD_DIM], gk_ref[...], 1.0)
        khead = jnp.concatenate([r1, r2, zero], axis=0)
        k_ref[hd] = khead.T.astype(BF16)

    v_ref[...] = _nt(wv_ref[...], hb).astype(BF16)


def _pre_odd(lay, zin, mod, g_mix, w, tabs):
    d = mod.shape[-1]
    tokp = lay.TOKP
    col = lambda rows: pl.BlockSpec((rows, TM), lambda u: (0, u))
    hd2 = GQA_HEAD_DIM // 2
    zin_specs, zout_specs, zout_shape = _zin_specs(lay, zin, d)
    in_specs = zin_specs + [
        pl.BlockSpec((1, 6, d), lambda u: (lay.mrow(u), 0, 0)),
        _full(g_mix), _full(w["wq"]), _full(w["wk"]), _full(w["wv"]), _full(w["gq"]), _full(w["gk"]),
        pl.BlockSpec((hd2, TM), lambda u: (0, lay.ropeblk(u))),
        pl.BlockSpec((hd2, TM), lambda u: (0, lay.ropeblk(u))),
    ]
    out_specs = [
        col(GQA_HEADS * GQA_HEAD_DIM),
        pl.BlockSpec((GQA_KV_HEADS, TM, DK), lambda u: (0, u, 0)),
        col(GQA_KV_HEADS * DV),
    ]
    out_shape = [
        jax.ShapeDtypeStruct((GQA_HEADS * GQA_HEAD_DIM, tokp), BF16),
        jax.ShapeDtypeStruct((GQA_KV_HEADS, tokp, DK), BF16),
        jax.ShapeDtypeStruct((GQA_KV_HEADS * DV, tokp), BF16),
    ]
    return pl.pallas_call(
        functools.partial(_pre_odd_kernel, n_zin=len(zin)),
        grid=(lay.NT,), in_specs=in_specs, out_specs=out_specs + zout_specs, out_shape=out_shape + zout_shape,
        compiler_params=_cparams(), name="pre_odd",
    )(*zin, mod, g_mix, w["wq"], w["wk"], w["wv"], w["gq"], w["gk"], tabs["gqa_ct"], tabs["gqa_st"])


def _attn_kernel(q_ref, kc_ref, kl_ref, vc_ref, vl_ref, o_ref, s0_ref, s1_ref, *, nq, kb, grp, dq, S, C):
    qi = pl.program_id(2)
    hb = kb * grp
    sbufs = (s0_ref, s1_ref)
    nchunk = S // TKC

    def scores(hh, buf, with_latent):
        kv = hh // grp
        q = q_ref[hh * dq:(hh + 1) * dq, :]
        sc = _dot(kc_ref[kv, :, :dq], q)
        buf[0:C, :] = sc
        m = jnp.max(sc, axis=0, keepdims=True)
        if with_latent:
            for c in range(nchunk):
                s = _dot(kl_ref[kv, c * TKC:(c + 1) * TKC, :dq], q)
                buf[C + c * TKC:C + (c + 1) * TKC, :] = s
                m = jnp.maximum(m, jnp.max(s, axis=0, keepdims=True))
        return m

    def values(hh, buf, m, with_latent):
        kv = hh // grp
        vrows = slice(kv * DV, (kv + 1) * DV)
        p = jnp.exp2(buf[0:C, :] - m)
        l = jnp.sum(p, axis=0, keepdims=True)
        o = _dot(vc_ref[vrows, :], p.astype(BF16))
        if with_latent:
            for c in range(nchunk):
                p = jnp.exp2(buf[C + c * TKC:C + (c + 1) * TKC, :] - m)
                l = l + jnp.sum(p, axis=0, keepdims=True)
                o = o + _dot(vl_ref[vrows, c * TKC:(c + 1) * TKC], p.astype(BF16))
        o_ref[hh * DV:(hh + 1) * DV, :] = (o * (1.0 / l)).astype(o_ref.dtype)

    def run(with_latent):
        m = scores(0, sbufs[0], with_latent)
        for hh in range(hb):
            m_next = scores(hh + 1, sbufs[(hh + 1) % 2], with_latent) if hh + 1 < hb else None
            values(hh, sbufs[hh % 2], m, with_latent)
            m = m_next

    @pl.when(qi < nq)
    def _():
        run(True)

    @pl.when(qi == nq)
    def _():
        run(False)


def _attention(lay, qt, k, vt, *, n_kv, grp, dq, kb):
    B, S, C = lay.B, lay.S, lay.C
    hb = kb * grp
    nq = S // TQ

    def qcol(b, qi):
        return jnp.where(qi == nq, lay.NL + b, b * nq + qi)

    in_specs = [
        pl.BlockSpec((hb * dq, TQ), lambda b, h, qi: (h, qcol(b, qi))),
        pl.BlockSpec((kb, C, DK), lambda b, h, qi: (h, lay.NL + b, 0)),
        pl.BlockSpec((kb, S, DK), lambda b, h, qi: (h, b, 0)),
        pl.BlockSpec((kb * DV, C), lambda b, h, qi: (h, lay.NL + b)),
        pl.BlockSpec((kb * DV, S), lambda b, h, qi: (h, b)),
    ]
    return pl.pallas_call(
        functools.partial(_attn_kernel, nq=nq, kb=kb, grp=grp, dq=dq, S=S, C=C),
        grid=(B, n_kv // kb, nq + 1),
        in_specs=in_specs,
        out_specs=pl.BlockSpec((hb * DV, TQ), lambda b, h, qi: (h, qcol(b, qi))),
        out_shape=jax.ShapeDtypeStruct((n_kv * grp * DV, lay.TOKP), BF16),
        scratch_shapes=[pltpu.VMEM((C + S, TQ), F32), pltpu.VMEM((C + S, TQ), F32)],
        compiler_params=_cparams(),
        name="attention",
    )(qt, k, k, vt, vt)


def _dft_kernel(a_ref, b_ref, c_ref, s_ref, o_ref):
    o_ref[...] = (_dot(a_ref[...], c_ref[...]) - _dot(b_ref[...], s_ref[...])).astype(o_ref.dtype)


def _seq_dft(lay, abt, cn, sn, cc, sc):
    B, S, C = lay.B, lay.S, lay.C
    fw = FOURIER_WIDTH
    nj = S // TN_DFT
    ylat = pl.pallas_call(
        _dft_kernel,
        grid=(nj, B),
        in_specs=[
            pl.BlockSpec((fw, S), lambda j, b: (0, b)),
            pl.BlockSpec((fw, S), lambda j, b: (1, b)),
            pl.BlockSpec((S, TN_DFT), lambda j, b: (0, j)),
            pl.BlockSpec((S, TN_DFT), lambda j, b: (0, j)),
        ],
        out_specs=pl.BlockSpec((fw, TN_DFT), lambda j, b: (0, b * nj + j)),
        out_shape=jax.ShapeDtypeStruct((fw, B * S), BF16),
        compiler_params=_cparams(),
        name="seq_dft",
    )(abt, abt, cn, sn)
    yctx = pl.pallas_call(
        _dft_kernel,
        grid=(B,),
        in_specs=[
            pl.BlockSpec((fw, C), lambda b: (0, lay.NL + b)),
            pl.BlockSpec((fw, C), lambda b: (1, lay.NL + b)),
            pl.BlockSpec((C, C), lambda b: (0, 0)),
            pl.BlockSpec((C, C), lambda b: (0, 0)),
        ],
        out_specs=pl.BlockSpec((fw, C), lambda b: (0, b)),
        out_shape=jax.ShapeDtypeStruct((fw, B * C), BF16),
        compiler_params=_cparams(),
        name="ctx_dft",
    )(abt, abt, cc, sc)
    return ylat, yctx


def _route_rows(lg, rbias):
    s = jax.nn.sigmoid(lg)
    sel = s + rbias
    rows = [sel[e:e + 1, :] for e in range(N_EXPERTS)]
    best = None
    bg = None
    for g in range(N_GROUPS):
        v = rows[4 * g:4 * g + 4]
        sc = None
        for i in range(4):
            for j in range(i + 1, 4):
                ps = v[i] + v[j]
                sc = ps if sc is None else jnp.maximum(sc, ps)
        if best is None:
            best, bg = sc, jnp.zeros_like(sc)
        else:
            upd = sc > best
            bg = jnp.where(upd, float(g), bg)
            best = jnp.where(upd, sc, best)
    v = []
    for i in range(4):
        acc = rows[i]
        for g in range(1, N_GROUPS):
            acc = jnp.where(bg == float(g), rows[4 * g + i], acc)
        v.append(acc)
    i0 = jnp.zeros_like(bg)
    b0 = v[0]
    for i in range(1, 4):
        upd = v[i] > b0
        i0 = jnp.where(upd, float(i), i0)
        b0 = jnp.where(upd, v[i], b0)
    w = [jnp.where(i0 == float(i), -jnp.inf, v[i]) for i in range(4)]
    i1 = jnp.zeros_like(bg)
    b1 = w[0]
    for i in range(1, 4):
        upd = w[i] > b1
        i1 = jnp.where(upd, float(i), i1)
        b1 = jnp.where(upd, w[i], b1)
    lo = jnp.minimum(i0, i1)
    hi = jnp.maximum(i0, i1)
    pbase = jnp.where(lo == 0.0, 0.0, jnp.where(lo == 1.0, 3.0, 5.0))
    return bg * 6.0 + pbase + hi - lo - 1.0


def _post_kernel(z_ref, mod_ref, a_ref, actx_ref, b_ref, wa_ref, wb_ref, g_ref, rwh_ref, rwl_ref, rb_ref, ut_ref,
                 zo_ref, h2_ref, info_ref, cnt_ref, *, nl_split):
    a = a_ref[...]
    if nl_split is not None:
        a = jnp.where(pl.program_id(0) < nl_split, a, actx_ref[...])
    attn = _tn(a, wa_ref[...]) + _tn(b_ref[...], wb_ref[...])
    x1 = z_ref[...] + mod_ref[0, 2:3, :] * attn
    zo_ref[...] = x1
    h = _norm_mod(x1, g_ref[...], mod_ref[0, 4:5, :], mod_ref[0, 3:4, :])
    _lanes_to_rows(h2_ref, h)
    hh = h.astype(BF16)
    hl = (h - hh.astype(F32)).astype(BF16)
    lg = _nt(rwh_ref[...], hh) + _nt(rwh_ref[...], hl) + _nt(rwl_ref[...], hh)
    cls = _route_rows(lg, rb_ref[...])
    ohf = (lax.broadcasted_iota(jnp.int32, (32, TM), 0).astype(F32) == cls).astype(F32)
    rank_all = _dot(ohf.astype(BF16), ut_ref[...])
    info_ref[0:1, :] = cls
    info_ref[1:2, :] = jnp.sum(ohf * rank_all, axis=0, keepdims=True)
    info_ref[2:8, :] = jnp.zeros((6, TM), F32)
    cnt_ref[0] = jnp.broadcast_to(jnp.sum(ohf, axis=1, keepdims=True), (32, 128))


def _post(lay, z, mod, src_a, src_a_ctx, src_b, blk_a, blk_b, wa, wb, g_ffn, rwh, rwl, rbias, ut):
    d = z.shape[1]
    nt = lay.NT
    fw = FOURIER_WIDTH
    if src_a_ctx is None:
        nl_split = None
        src_a_ctx = jnp.zeros((fw, TM), BF16)
        spec_a = pl.BlockSpec((fw, TM), lambda u: (blk_a, u))
        spec_actx = pl.BlockSpec((fw, TM), lambda u: (0, 0))
    else:
        nl_split = lay.NL
        spec_a = pl.BlockSpec((fw, TM), lambda u: (blk_a, jnp.minimum(u, lay.NL - 1)))
        spec_actx = pl.BlockSpec((fw, TM), lambda u: (0, jnp.maximum(u - lay.NL, 0)))
    in_specs = [
        pl.BlockSpec((TM, d), lambda u: (u, 0)),
        pl.BlockSpec((1, 6, d), lambda u: (lay.mrow(u), 0, 0)),
        spec_a,
        spec_actx,
        pl.BlockSpec((fw, TM), lambda u: (blk_b, u)),
        _full(wa), _full(wb), _full(g_ffn), _full(rwh), _full(rwl), _full(rbias), _full(ut),
    ]
    out_specs = [
        pl.BlockSpec((TM, d), lambda u: (u, 0)),
        pl.BlockSpec((TM, 8, 128), lambda u: (u, 0, 0)),
        pl.BlockSpec((8, TM), lambda u: (0, u)),
        pl.BlockSpec((1, 32, 128), lambda u: (u, 0, 0)),
    ]
    out_shape = [
        jax.ShapeDtypeStruct((lay.TOKP, d), F32),
        jax.ShapeDtypeStruct((lay.TOKP, 8, 128), F32),
        jax.ShapeDtypeStruct((8, lay.TOKP), F32),
        jax.ShapeDtypeStruct((nt, 32, 128), F32),
    ]
    return pl.pallas_call(
        functools.partial(_post_kernel, nl_split=nl_split),
        grid=(nt,), in_specs=in_specs, out_specs=out_specs, out_shape=out_shape,
        compiler_params=_cparams(), name="post",
    )(z, mod, src_a, src_a_ctx, src_b, wa, wb, g_ffn, rwh, rwl, rbias, ut)


def _invperm_kernel(pos_ref, tok_ref):
    u = pl.program_id(0)

    @pl.when(u == 0)
    def _():
        def clear(p, c):
            tok_ref[p] = 0
            return c
        lax.fori_loop(0, tok_ref.shape[0], clear, 0, unroll=8)

    def put(r, c):
        tok_ref[pos_ref[0, 0, r]] = u * TM + r
        return c
    lax.fori_loop(0, TM, put, 0, unroll=8)


def _invperm(lay, pos3, pmax):
    return pl.pallas_call(
        _invperm_kernel,
        grid=(lay.NT,),
        in_specs=[pl.BlockSpec((1, 1, TM), lambda u: (u, 0, 0), memory_space=pltpu.SMEM)],
        out_specs=pl.BlockSpec(memory_space=pltpu.SMEM),
        out_shape=jax.ShapeDtypeStruct((pmax,), jnp.int32),
        name="moe_invperm",
    )(pos3)


def _expert_kernel(ea_ref, eb_ref, nu_ref, nv_ref, tok_ref, tokn_ref, h2_ref, rw_ref, wga_ref, wgb_ref, wda_ref,
                   wdb_ref, ys_ref, xbuf, ybuf, gsem, ssem):
    i = pl.program_id(0)
    nu = nu_ref[0]

    def gather_start(t_ref, slot):
        def body(r, c):
            pltpu.make_async_copy(h2_ref.at[t_ref[0, 0, r]], xbuf.at[slot, r], gsem.at[slot]).start()
            return c
        lax.fori_loop(0, TMM, body, 0, unroll=8)

    def gather_wait(slot):
        pltpu.make_async_copy(h2_ref.at[pl.ds(0, TMM)], xbuf.at[slot], gsem.at[slot]).wait()

    def scatter_start(n):
        def body(r, c):
            pltpu.make_async_copy(ybuf.at[r], ys_ref.at[tok_ref[0, 0, r]], ssem).start()
            return c
        lax.fori_loop(0, n, body, 0)

    def scatter_wait(n):
        k = TMM
        while k >= 1:
            @pl.when((n & k) != 0)
            def _(k=k):
                pltpu.make_async_copy(ybuf.at[pl.ds(0, k)], ys_ref.at[pl.ds(0, k)], ssem).wait()
            k //= 2

    @pl.when(i < nu)
    def _():
        slot = i % 2

        @pl.when(i == 0)
        def _():
            gather_start(tok_ref, 0)

        gather_wait(slot)

        @pl.when(i + 1 < nu)
        def _():
            gather_start(tokn_ref, 1 - slot)

        x = _rows_to_lanes(xbuf.at[slot])
        xb = x.astype(BF16)
        s = jax.nn.sigmoid(_dot(xb, rw_ref[...]))
        lane = lax.broadcasted_iota(jnp.int32, s.shape, 1)
        sa = jnp.sum(jnp.where(lane == ea_ref[i], s, 0.0), axis=1, keepdims=True)
        sb = jnp.sum(jnp.where(lane == eb_ref[i], s, 0.0), axis=1, keepdims=True)
        inv = 1.0 / (sa + sb)

        def hidden(w_ref, gate):
            gu = _dot(xb, w_ref[0])
            g = gu[:, :D_EXPERT]
            return (g * jax.nn.sigmoid(g) * gu[:, D_EXPERT:] * gate).astype(BF16)

        y = _dot(hidden(wga_ref, sa * inv), wda_ref[0]) + _dot(hidden(wgb_ref, sb * inv), wdb_ref[0])

        @pl.when(i > 0)
        def _():
            scatter_wait(nv_ref[i - 1])

        _lanes_to_rows(ybuf, y)
        scatter_start(nv_ref[i])

        @pl.when(i == nu - 1)
        def _():
            scatter_wait(nv_ref[i])


def _experts(lay, tok3, h2, ea, eb, nused, nvalid, rw, wgu, wd):
    ntile = tok3.shape[0]
    d = wd.shape[2]
    grid_spec = pltpu.PrefetchScalarGridSpec(
        num_scalar_prefetch=4,
        grid=(ntile,),
        in_specs=[
            pl.BlockSpec((1, 1, TMM), lambda i, *_: (i, 0, 0), memory_space=pltpu.SMEM),
            pl.BlockSpec((1, 1, TMM), lambda i, *_: (jnp.minimum(i + 1, ntile - 1), 0, 0), memory_space=pltpu.SMEM),
            pl.BlockSpec(memory_space=pl.ANY),
            pl.BlockSpec(rw.shape, lambda i, *_: (0, 0)),
            pl.BlockSpec((1, d, 2 * D_EXPERT), lambda i, ea, eb, nu, nv: (ea[i], 0, 0)),
            pl.BlockSpec((1, d, 2 * D_EXPERT), lambda i, ea, eb, nu, nv: (eb[i], 0, 0)),
            pl.BlockSpec((1, D_EXPERT, d), lambda i, ea, eb, nu, nv: (ea[i], 0, 0)),
            pl.BlockSpec((1, D_EXPERT, d), lambda i, ea, eb, nu, nv: (eb[i], 0, 0)),
        ],
        out_specs=pl.BlockSpec(memory_space=pl.ANY),
        scratch_shapes=[
            pltpu.VMEM((2, TMM, 8, 128), F32),
            pltpu.VMEM((TMM, 8, 128), F32),
            pltpu.SemaphoreType.DMA((2,)),
            pltpu.SemaphoreType.DMA(()),
        ],
    )
    return pl.pallas_call(
        _expert_kernel, grid_spec=grid_spec,
        out_shape=jax.ShapeDtypeStruct((lay.TOKP, 8, 128), F32),
        compiler_params=_cparams(), name="moe_experts",
    )(ea, eb, nused, nvalid, tok3, tok3, h2, rw, wgu, wgu, wd, wd)


def _final_kernel(z_ref, y_ref, mod_ref, gf_ref, o_ref):
    x2 = z_ref[...] + mod_ref[0, 5:6, :] * _rows_to_lanes(y_ref)
    ms = jnp.mean(x2 * x2, axis=-1, keepdims=True)
    o_ref[...] = x2 * lax.rsqrt(ms + EPS) * gf_ref[...]


def _final(lay, z, ys, mod, g_final):
    d = z.shape[1]
    return pl.pallas_call(
        _final_kernel,
        grid=(lay.NL,),
        in_specs=[
            pl.BlockSpec((TM, d), lambda u: (u, 0)),
            pl.BlockSpec((TM, 8, 128), lambda u: (u, 0, 0)),
            pl.BlockSpec((1, 6, d), lambda u: (lay.mrow(u), 0, 0)),
            _full(g_final),
        ],
        out_specs=pl.BlockSpec((TM, d), lambda u: (u, 0)),
        out_shape=jax.ShapeDtypeStruct((lay.NL * TM, d), F32),
        compiler_params=_cparams(), name="final_norm",
    )(z, ys, mod, g_final)


def _plan(info, cnt, nt, ntile_max):
    cls = info[0].astype(jnp.int32).reshape(nt, TM)
    rank = info[1].astype(jnp.int32).reshape(nt, TM)
    cnt = cnt[:, :N_CLASSES, 0].astype(jnp.int32)
    tot = jnp.sum(cnt, axis=0)
    ntile_c = (tot + TMM - 1) // TMM
    tile_end = jnp.cumsum(ntile_c)
    class_off = (tile_end - ntile_c) * TMM
    tile_off = jnp.cumsum(cnt, axis=0) - cnt
    base = class_off[None, :] + tile_off
    onehot = cls[:, :, None] == jnp.arange(N_CLASSES, dtype=jnp.int32)[None, None, :]
    pos = jnp.sum(jnp.where(onehot, base[:, None, :], 0), axis=-1) + rank
    nused = tile_end[-1]
    ti = jnp.minimum(jnp.arange(ntile_max, dtype=jnp.int32), nused - 1)
    tcls = jnp.sum((ti[:, None] >= tile_end[None, :]).astype(jnp.int32), axis=1)
    onec = tcls[:, None] == jnp.arange(N_CLASSES, dtype=jnp.int32)[None, :]
    class_end = jnp.sum(jnp.where(onec, (class_off + tot)[None, :], 0), axis=1)
    nvalid = jnp.clip(class_end - ti * TMM, 0, TMM)
    grp, pair = tcls // 6, tcls % 6
    lo = jnp.asarray(_PAIR_LO, jnp.int32)
    hi = jnp.asarray(_PAIR_HI, jnp.int32)
    onep = pair[:, None] == jnp.arange(6, dtype=jnp.int32)[None, :]
    ea = 4 * grp + jnp.sum(jnp.where(onep, lo[None, :], 0), axis=1)
    eb = 4 * grp + jnp.sum(jnp.where(onep, hi[None, :], 0), axis=1)
    return pos.reshape(nt, 1, TM), ea, eb, nused.reshape(1), nvalid


def _dft_mats(n, scale):
    j = lax.broadcasted_iota(jnp.int32, (n, n), 0)
    k = lax.broadcasted_iota(jnp.int32, (n, n), 1)
    ang = ((j * k) % n).astype(F32) * (2.0 * math.pi / n)
    return (jnp.cos(ang) * scale).astype(BF16), (jnp.sin(ang) * scale).astype(BF16)


def _rope_angles(S, dim):
    rows = S // GRID_W
    row_id = jnp.repeat(jnp.arange(rows, dtype=F32), GRID_W)
    col_id = jnp.tile(jnp.arange(GRID_W, dtype=F32), rows)
    n_freq = dim // 4
    inv = ROPE_BASE ** (-jnp.arange(n_freq, dtype=F32) / n_freq)
    return jnp.concatenate([row_id[:, None] * inv, col_id[:, None] * inv], axis=-1)


def _tables(lay):
    S, C = lay.S, lay.C
    tabs = {}
    cc, sc = _dft_mats(FOURIER_GROUP_DIM, FOURIER_GROUP_DIM ** -0.5)
    tabs["cs"] = jnp.concatenate([cc, sc], axis=0)
    tabs["cn"], tabs["sn"] = _dft_mats(S, S ** -0.5)
    tabs["cctx"], tabs["sctx"] = _dft_mats(C, C ** -0.5)
    for name, dim in (("mla", MLA_ROPE_DIM), ("gqa", GQA_HEAD_DIM)):
        ang = _rope_angles(S, dim)
        cos = jnp.concatenate([jnp.ones((C, dim // 2), F32), jnp.cos(ang)], axis=0)
        sin = jnp.concatenate([jnp.zeros((C, dim // 2), F32), jnp.sin(ang)], axis=0)
        tabs[name + "_ct"] = cos.T
        tabs[name + "_st"] = sin.T
        if name == "mla":
            pad = jnp.zeros((C + S, 128 - dim), F32)
            tabs["mla_t1"] = jnp.concatenate([cos, cos, pad], axis=1)
            tabs["mla_t2"] = jnp.concatenate([sin, -sin, pad], axis=1)
    tabs["ut"] = (lax.broadcasted_iota(jnp.int32, (TM, TM), 0)
                  < lax.broadcasted_iota(jnp.int32, (TM, TM), 1)).astype(BF16)
    return tabs


def _col(g, rows):
    return jnp.broadcast_to(g.astype(F32)[:, None], (rows, TM))


def _even_weights(w_in, q_norm_g, w_qb, kv_norm_g, w_kvb, w_out):
    d = w_in.shape[0]
    o1 = FOURIER_WIDTH
    o2 = o1 + MLA_Q_RANK
    o3 = o2 + MLA_KV_RANK
    bf = lambda a: a.astype(BF16)
    w = {}
    w["wf"] = bf(w_in[:, :o1].T)
    w["wqa"] = bf(w_in[:, o1:o2].T)
    w["wkvat"] = bf(w_in[:, o2:o3].T)
    w["wkva"] = bf(w_in[:, o2:o3])
    w["wpr"] = bf(jnp.concatenate([w_in[:, o3:], jnp.zeros((d, 128 - MLA_ROPE_DIM), F32)], axis=1))
    w["gq"] = _col(q_norm_g, MLA_Q_RANK)
    w["wqb"] = bf(w_qb.T)
    w["gkvc"] = _col(kv_norm_g, MLA_KV_RANK)
    w["gkvr"] = kv_norm_g.astype(F32)[None, :]
    kvb = w_kvb.reshape(MLA_KV_RANK, MLA_HEADS, MLA_NOPE_DIM + MLA_V_DIM)
    w["wv"] = bf(kvb[:, :, MLA_NOPE_DIM:].reshape(MLA_KV_RANK, MLA_HEADS * MLA_V_DIM).T)
    w1 = jnp.concatenate([kvb[:, :, :MLA_NOPE_DIM],
                          jnp.zeros((MLA_KV_RANK, MLA_HEADS, DK - MLA_NOPE_DIM), F32)], axis=2)
    w1 = w1.reshape(MLA_KV_RANK, MLA_HEADS * DK)
    eye = jnp.eye(MLA_ROPE_DIM, dtype=F32)
    place = jnp.zeros((128, DK), F32).at[:MLA_ROPE_DIM, MLA_NOPE_DIM:MLA_QK_DIM].set(eye)
    swap = jnp.roll(eye, MLA_ROPE_DIM // 2, axis=1)
    place_s = jnp.zeros((128, DK), F32).at[:MLA_ROPE_DIM, MLA_NOPE_DIM:MLA_QK_DIM].set(swap)
    w["wk"] = bf(jnp.concatenate([w1, jnp.tile(place, (1, MLA_HEADS)), jnp.tile(place_s, (1, MLA_HEADS))], axis=0))
    w["wout_a"] = bf(w_out[:o1])
    w["wout_b"] = bf(w_out[o1:])
    return w


def _odd_weights(w_qkv, q_norm_g, k_norm_g, w_out):
    bf = lambda a: a.astype(BF16)
    qw = GQA_HEADS * GQA_HEAD_DIM
    kw = GQA_KV_HEADS * GQA_HEAD_DIM
    w = {}
    w["wq"] = bf(w_qkv[:, :qw].T)
    w["wk"] = bf(w_qkv[:, qw:qw + kw].T)
    w["wv"] = bf(w_qkv[:, qw + kw:].T)
    w["gq"] = _col(q_norm_g, GQA_HEAD_DIM)
    w["gk"] = _col(k_norm_g, GQA_HEAD_DIM)
    w["wout_a"] = bf(w_out[:FOURIER_WIDTH])
    w["wout_b"] = bf(w_out[FOURIER_WIDTH:])
    return w


def kernel(x, c, ctx, c_ctx, ada_w, ada_b, norm_mix_g, norm_ffn_g, ev_w_in, ev_q_norm_g, ev_w_qb, ev_kv_norm_g,
           ev_w_kvb, ev_w_out, od_w_qkv, od_q_norm_g, od_k_norm_g, od_w_out, router_w, router_b, exp_w_gate,
           exp_w_up, exp_w_down, final_norm_g):
    B, S, d = x.shape
    C = ctx.shape[1]
    depth = ada_w.shape[0]
    lay = _Layout(B, S, C)
    assert B + 1 <= 16
    tabs = _tables(lay)

    cond = jnp.concatenate([c, c_ctx[None, :], jnp.zeros((16 - B - 1, d), F32)], axis=0)
    mods = _modulation(cond, ada_w, ada_b).reshape(depth, 16, 6, d)

    z = jnp.concatenate([x.reshape(B * S, d), ctx.reshape(B * C, d)], axis=0)

    rwt = router_w.T.astype(F32)
    rwh = rwt.astype(BF16)
    rwl = (rwt - rwh.astype(F32)).astype(BF16)
    rbias = jnp.broadcast_to(router_b.astype(F32)[:, None], (N_EXPERTS, TM))
    rw_nat = jnp.concatenate([router_w, jnp.zeros((d, 128 - N_EXPERTS), F32)], axis=1).astype(BF16)
    wgu_all = jnp.concatenate([exp_w_gate, exp_w_up], axis=-1).astype(BF16)
    wd_all = exp_w_down.astype(BF16)
    ntile_max = lay.NT * TM // TMM + N_CLASSES

    zin = (z,)
    for i in range(depth):
        mod = mods[i]
        g_mix = norm_mix_g[i][None, :]
        g_ffn = norm_ffn_g[i][None, :]
        j = i // 2
        if i % 2 == 0:
            w = _even_weights(ev_w_in[j], ev_q_norm_g[j], ev_w_qb[j], ev_kv_norm_g[j], ev_w_kvb[j], ev_w_out[j])
            abt, qt, k, vt, *znew = _pre_even(lay, zin, mod, g_mix, w, tabs)
            ot = _attention(lay, qt, k, vt, n_kv=MLA_HEADS, grp=1, dq=DK, kb=4)
            src_a, src_a_ctx = _seq_dft(lay, abt, tabs["cn"], tabs["sn"], tabs["cctx"], tabs["sctx"])
            src_b, blk_a, blk_b = ot, 0, 0
        else:
            w = _odd_weights(od_w_qkv[j], od_q_norm_g[j], od_k_norm_g[j], od_w_out[j])
            qt, k, vt, *znew = _pre_odd(lay, zin, mod, g_mix, w, tabs)
            ot = _attention(lay, qt, k, vt, n_kv=GQA_KV_HEADS, grp=GQA_GROUP, dq=GQA_HEAD_DIM, kb=2)
            src_a, src_a_ctx, src_b, blk_a, blk_b = ot, None, ot, 0, 1
        z = znew[0] if znew else zin[0]
        z1, h2, info, cnt = _post(lay, z, mod, src_a, src_a_ctx, src_b, blk_a, blk_b, w["wout_a"], w["wout_b"],
                                  g_ffn, rwh, rwl, rbias, tabs["ut"])
        pos3, ea, eb, nused, nvalid = _plan(info, cnt, lay.NT, ntile_max)
        tok3 = _invperm(lay, pos3, ntile_max * TMM).reshape(ntile_max, 1, TMM)
        ys = _experts(lay, tok3, h2, ea, eb, nused, nvalid, rw_nat, wgu_all[i], wd_all[i])
        zin = (z1, ys, mod)
    out = _final(lay, zin[0], zin[1], zin[2], final_norm_g[None, :])
    return out.reshape(B, S, d)
```

```python
import functools
import math

import jax
import jax.numpy as jnp
from jax import lax
from jax.experimental import pallas as pl
from jax.experimental.pallas import tpu as pltpu

F32 = jnp.float32
BF16 = jnp.bfloat16

TM = 512
TQ = 256
TKC = 512
TMM = 256
TN_DFT = 512
EPS = 1e-6
LOG2E = 1.4426950408889634
ROPE_BASE = 10000.0
GRID_W = 64

FOURIER_GROUPS = 4
FOURIER_GROUP_DIM = 128
FOURIER_WIDTH = 512
MLA_HEADS = 8
MLA_NOPE_DIM = 64
MLA_ROPE_DIM = 32
MLA_V_DIM = 64
MLA_QK_DIM = 96
MLA_Q_RANK = 256
MLA_KV_RANK = 128
MLA_SCALE = MLA_QK_DIM ** -0.5
GQA_HEADS = 16
GQA_KV_HEADS = 4
GQA_HEAD_DIM = 64
GQA_GROUP = 4
GQA_SCALE = GQA_HEAD_DIM ** -0.5
N_EXPERTS = 16
N_GROUPS = 4
N_CLASSES = 24
D_EXPERT = 512
DK = 128
DV = 64
VMEM_LIMIT = 56 * 1024 * 1024

_PAIR_LO = (0, 0, 0, 1, 1, 2)
_PAIR_HI = (1, 2, 3, 2, 3, 3)


def _nt(a, b):
    return lax.dot_general(a, b, (((1,), (1,)), ((), ())), preferred_element_type=F32)


def _tn(a, b):
    return lax.dot_general(a, b, (((0,), (0,)), ((), ())), preferred_element_type=F32)


def _dot(a, b):
    return jnp.dot(a, b, preferred_element_type=F32)


class _Layout:
    def __init__(self, B, S, C):
        assert C == TQ and TM % C == 0 and (B * C) % TM == 0 and S % TM == 0 and S % TN_DFT == 0 and S % TKC == 0
        self.B, self.S, self.C = B, S, C
        self.SB = S // TM
        self.NL = B * self.SB
        self.NT = self.NL + B * C // TM
        self.TOKP = self.NT * TM
        self.CB = B * S // C

    def mrow(self, u):
        return jnp.where(u < self.NL, u // self.SB, self.B)

    def ropeblk(self, u):
        return jnp.where(u < self.NL, 1 + u % self.SB, 0)


def _cparams():
    return pltpu.CompilerParams(vmem_limit_bytes=VMEM_LIMIT)


def _full(a):
    return pl.BlockSpec(a.shape, lambda *_: (0,) * a.ndim)


def _mod_kernel(s_ref, w_ref, b_ref, o_ref):
    s = s_ref[...]
    a = s * jax.nn.sigmoid(s)
    ah = a.astype(BF16)
    al = (a - ah.astype(F32)).astype(BF16)
    w = w_ref[0]
    wh = w.astype(BF16)
    wl = (w - wh.astype(F32)).astype(BF16)
    o_ref[0] = _dot(ah, wh) + _dot(ah, wl) + _dot(al, wh) + b_ref[0]


def _modulation(cond, ada_w, ada_b):
    depth, d, w6 = ada_w.shape
    tn = 768
    return pl.pallas_call(
        _mod_kernel,
        grid=(depth, w6 // tn),
        in_specs=[
            pl.BlockSpec((16, d), lambda i, j: (0, 0)),
            pl.BlockSpec((1, d, tn), lambda i, j: (i, 0, j)),
            pl.BlockSpec((1, 1, tn), lambda i, j: (i, 0, j)),
        ],
        out_specs=pl.BlockSpec((1, 16, tn), lambda i, j: (i, 0, j)),
        out_shape=jax.ShapeDtypeStruct((depth, 16, w6), F32),
        compiler_params=_cparams(),
        name="modulation",
    )(cond, ada_w, ada_b.reshape(depth, 1, w6))


def _norm_mod(z, g, sc, sh):
    ms = jnp.mean(z * z, axis=-1, keepdims=True)
    return (z * lax.rsqrt(ms + EPS) * g) * (1.0 + sc) + sh


def _rows_to_lanes(ref):
    return jnp.concatenate([ref[:, k, :] for k in range(8)], axis=1)


def _lanes_to_rows(ref, val):
    for k in range(8):
        ref[:, k, :] = val[:, k * 128:(k + 1) * 128]


def _zin_specs(lay, zin, d):
    row = pl.BlockSpec((TM, d), lambda u: (u, 0))
    if len(zin) == 1:
        return [row], [], []
    specs = [row, pl.BlockSpec((TM, 8, 128), lambda u: (u, 0, 0)),
             pl.BlockSpec((1, 6, d), lambda u: (lay.mrow(u), 0, 0))]
    return specs, [row], [jax.ShapeDtypeStruct((lay.TOKP, d), F32)]


def _zin_value(zin_refs, zout_refs):
    if len(zin_refs) == 1:
        return zin_refs[0][...]
    z_ref, y_ref, modp_ref = zin_refs
    z = z_ref[...] + modp_ref[0, 5:6, :] * _rows_to_lanes(y_ref)
    zout_refs[0][...] = z
    return z


def _pre_even_kernel(*refs, n_zin):
    (mod_ref, g_ref, wf_ref, wqa_ref, wkvat_ref, wkva_ref, wpr_ref, gq_ref, wqb_ref, gkvc_ref, gkvr_ref, wv_ref,
     wk_ref, cs_ref, ct_ref, st_ref, t1_ref, t2_ref, ab_ref, q_ref, k_ref, v_ref) = refs[n_zin:n_zin + 22]
    z = _zin_value(refs[:n_zin], refs[n_zin + 22:])
    h = _norm_mod(z, g_ref[...], mod_ref[0, 1:2, :], mod_ref[0, 0:1, :])
    hb = h.astype(BF16)

    ft = _nt(wf_ref[...], hb).astype(BF16)
    gd = FOURIER_GROUP_DIM
    for g in range(FOURIER_GROUPS):
        ab = _dot(cs_ref[...], ft[g * gd:(g + 1) * gd, :])
        ab_ref[g * gd:(g + 1) * gd, :] = ab[:gd].astype(BF16)
        ab_ref[FOURIER_WIDTH + g * gd:FOURIER_WIDTH + (g + 1) * gd, :] = ab[gd:].astype(BF16)

    pq = _nt(wqa_ref[...], hb)
    rs = lax.rsqrt(jnp.mean(pq * pq, axis=0, keepdims=True) + EPS)
    qn = (pq * rs * gq_ref[...]).astype(BF16)
    qt = _dot(wqb_ref[...], qn) * (MLA_SCALE * LOG2E)
    cos = ct_ref[...]
    sin = st_ref[...]
    hr = MLA_ROPE_DIM // 2
    for hd in range(MLA_HEADS):
        o = hd * MLA_QK_DIM
        r = hd * DK
        x1 = qt[o + MLA_NOPE_DIM:o + MLA_NOPE_DIM + hr, :]
        x2 = qt[o + MLA_NOPE_DIM + hr:o + MLA_QK_DIM, :]
        q_ref[r:r + MLA_NOPE_DIM, :] = qt[o:o + MLA_NOPE_DIM, :].astype(BF16)
        q_ref[r + MLA_NOPE_DIM:r + MLA_NOPE_DIM + hr, :] = (x1 * cos - x2 * sin).astype(BF16)
        q_ref[r + MLA_NOPE_DIM + hr:r + MLA_QK_DIM, :] = (x1 * sin + x2 * cos).astype(BF16)
        q_ref[r + MLA_QK_DIM:r + DK, :] = jnp.zeros((DK - MLA_QK_DIM, TM), BF16)

    pkvt = _nt(wkvat_ref[...], hb)
    rst = lax.rsqrt(jnp.mean(pkvt * pkvt, axis=0, keepdims=True) + EPS)
    kvnt = (pkvt * rst * gkvc_ref[...]).astype(BF16)
    v_ref[...] = _dot(wv_ref[...], kvnt).astype(BF16)

    pkv = _dot(hb, wkva_ref[...])
    rsn = lax.rsqrt(jnp.mean(pkv * pkv, axis=-1, keepdims=True) + EPS)
    kvn = (pkv * rsn * gkvr_ref[...]).astype(BF16)
    pr = _dot(hb, wpr_ref[...])
    lhs = jnp.concatenate([kvn, (pr * t1_ref[...]).astype(BF16), (pr * t2_ref[...]).astype(BF16)], axis=1)
    kf = _dot(lhs, wk_ref[...])
    for hd in range(MLA_HEADS):
        k_ref[hd] = kf[:, hd * DK:(hd + 1) * DK].astype(BF16)


def _pre_even(lay, zin, mod, g_mix, w, tabs):
    d = mod.shape[-1]
    tokp = lay.TOKP
    col = lambda rows: pl.BlockSpec((rows, TM), lambda u: (0, u))
    hr = MLA_ROPE_DIM // 2
    zin_specs, zout_specs, zout_shape = _zin_specs(lay, zin, d)
    in_specs = zin_specs + [
        pl.BlockSpec((1, 6, d), lambda u: (lay.mrow(u), 0, 0)),
        _full(g_mix),
        _full(w["wf"]), _full(w["wqa"]), _full(w["wkvat"]), _full(w["wkva"]), _full(w["wpr"]),
        _full(w["gq"]), _full(w["wqb"]), _full(w["gkvc"]), _full(w["gkvr"]), _full(w["wv"]), _full(w["wk"]),
        _full(tabs["cs"]),
        pl.BlockSpec((hr, TM), lambda u: (0, lay.ropeblk(u))),
        pl.BlockSpec((hr, TM), lambda u: (0, lay.ropeblk(u))),
        pl.BlockSpec((TM, 128), lambda u: (lay.ropeblk(u), 0)),
        pl.BlockSpec((TM, 128), lambda u: (lay.ropeblk(u), 0)),
    ]
    out_specs = [
        col(2 * FOURIER_WIDTH),
        col(MLA_HEADS * DK),
        pl.BlockSpec((MLA_HEADS, TM, DK), lambda u: (0, u, 0)),
        col(MLA_HEADS * DV),
    ]
    out_shape = [
        jax.ShapeDtypeStruct((2 * FOURIER_WIDTH, tokp), BF16),
        jax.ShapeDtypeStruct((MLA_HEADS * DK, tokp), BF16),
        jax.ShapeDtypeStruct((MLA_HEADS, tokp, DK), BF16),
        jax.ShapeDtypeStruct((MLA_HEADS * DV, tokp), BF16),
    ]
    return pl.pallas_call(
        functools.partial(_pre_even_kernel, n_zin=len(zin)),
        grid=(lay.NT,), in_specs=in_specs, out_specs=out_specs + zout_specs, out_shape=out_shape + zout_shape,
        compiler_params=_cparams(), name="pre_even",
    )(*zin, mod, g_mix, w["wf"], w["wqa"], w["wkvat"], w["wkva"], w["wpr"], w["gq"], w["wqb"], w["gkvc"],
      w["gkvr"], w["wv"], w["wk"], tabs["cs"], tabs["mla_ct"], tabs["mla_st"], tabs["mla_t1"], tabs["mla_t2"])


def _pre_odd_kernel(*refs, n_zin):
    (mod_ref, g_ref, wq_ref, wk_ref, wv_ref, gq_ref, gk_ref, ct_ref, st_ref,
     q_ref, k_ref, v_ref) = refs[n_zin:n_zin + 12]
    z = _zin_value(refs[:n_zin], refs[n_zin + 12:])
    h = _norm_mod(z, g_ref[...], mod_ref[0, 1:2, :], mod_ref[0, 0:1, :])
    hb = h.astype(BF16)
    cos = ct_ref[...]
    sin = st_ref[...]
    hd2 = GQA_HEAD_DIM // 2

    def norm_rope(xt, gain, scale):
        rs = lax.rsqrt(jnp.mean(xt * xt, axis=0, keepdims=True) + EPS)
        xn = xt * rs * gain
        x1 = xn[:hd2]
        x2 = xn[hd2:]
        return (x1 * cos - x2 * sin) * scale, (x1 * sin + x2 * cos) * scale

    qt = _nt(wq_ref[...], hb)
    for hd in range(GQA_HEADS):
        o = hd * GQA_HEAD_DIM
        r1, r2 = norm_rope(qt[o:o + GQA_HEAD_DIM], gq_ref[...], GQA_SCALE * LOG2E)
        q_ref[o:o + hd2, :] = r1.astype(BF16)
        q_ref[o + hd2:o + GQA_HEAD_DIM, :] = r2.astype(BF16)

    kt = _nt(wk_ref[...], hb)
    zero = jnp.zeros((DK - GQA_HEAD_DIM, TM), F32)
    for hd in range(GQA_KV_HEADS):
        o = hd * GQA_HEAD_DIM
        r1, r2 = norm_rope(kt[o:o + GQA_HEAD_DIM], gk_ref[...], 1.0)
        khead = jnp.concatenate([r1, r2, zero], axis=0)
        k_ref[hd] = khead.T.astype(BF16)

    v_ref[...] = _nt(wv_ref[...], hb).astype(BF16)


def _pre_odd(lay, zin, mod, g_mix, w, tabs):
    d = mod.shape[-1]
    tokp = lay.TOKP
    col = lambda rows: pl.BlockSpec((rows, TM), lambda u: (0, u))
    hd2 = GQA_HEAD_DIM // 2
    zin_specs, zout_specs, zout_shape = _zin_specs(lay, zin, d)
    in_specs = zin_specs + [
        pl.BlockSpec((1, 6, d), lambda u: (lay.mrow(u), 0, 0)),
        _full(g_mix), _full(w["wq"]), _full(w["wk"]), _full(w["wv"]), _full(w["gq"]), _full(w["gk"]),
        pl.BlockSpec((hd2, TM), lambda u: (0, lay.ropeblk(u))),
        pl.BlockSpec((hd2, TM), lambda u: (0, lay.ropeblk(u))),
    ]
    out_specs = [
        col(GQA_HEADS * GQA_HEAD_DIM),
        pl.BlockSpec((GQA_KV_HEADS, TM, DK), lambda u: (0, u, 0)),
        col(GQA_KV_HEADS * DV),
    ]
    out_shape = [
        jax.ShapeDtypeStruct((GQA_HEADS * GQA_HEAD_DIM, tokp), BF16),
        jax.ShapeDtypeStruct((GQA_KV_HEADS, tokp, DK), BF16),
        jax.ShapeDtypeStruct((GQA_KV_HEADS * DV, tokp), BF16),
    ]
    return pl.pallas_call(
        functools.partial(_pre_odd_kernel, n_zin=len(zin)),
        grid=(lay.NT,), in_specs=in_specs, out_specs=out_specs + zout_specs, out_shape=out_shape + zout_shape,
        compiler_params=_cparams(), name="pre_odd",
    )(*zin, mod, g_mix, w["wq"], w["wk"], w["wv"], w["gq"], w["gk"], tabs["gqa_ct"], tabs["gqa_st"])


def _attn_kernel(q_ref, kc_ref, kl_ref, vc_ref, vl_ref, o_ref, s0_ref, s1_ref, *, nq, kb, grp, dq, S, C):
    qi = pl.program_id(2)
    hb = kb * grp
    sbufs = (s0_ref, s1_ref)
    nchunk = S // TKC

    def scores(hh, buf, with_latent):
        kv = hh // grp
        q = q_ref[hh * dq:(hh + 1) * dq, :]
        sc = _dot(kc_ref[kv, :, :dq], q)
        buf[0:C, :] = sc
        m = jnp.max(sc, axis=0, keepdims=True)
        if with_latent:
            for c in range(nchunk):
                s = _dot(kl_ref[kv, c * TKC:(c + 1) * TKC, :dq], q)
                buf[C + c * TKC:C + (c + 1) * TKC, :] = s
                m = jnp.maximum(m, jnp.max(s, axis=0, keepdims=True))
        return m

    def values(hh, buf, m, with_latent):
        kv = hh // grp
        vrows = slice(kv * DV, (kv + 1) * DV)
        p = jnp.exp2(buf[0:C, :] - m)
        l = jnp.sum(p, axis=0, keepdims=True)
        o = _dot(vc_ref[vrows, :], p.astype(BF16))
        if with_latent:
            for c in range(nchunk):
                p = jnp.exp2(buf[C + c * TKC:C + (c + 1) * TKC, :] - m)
                l = l + jnp.sum(p, axis=0, keepdims=True)
                o = o + _dot(vl_ref[vrows, c * TKC:(c + 1) * TKC], p.astype(BF16))
        o_ref[hh * DV:(hh + 1) * DV, :] = (o * (1.0 / l)).astype(o_ref.dtype)

    def run(with_latent):
        m = scores(0, sbufs[0], with_latent)
        for hh in range(hb):
            m_next = scores(hh + 1, sbufs[(hh + 1) % 2], with_latent) if hh + 1 < hb else None
            values(hh, sbufs[hh % 2], m, with_latent)
            m = m_next

    @pl.when(qi < nq)
    def _():
        run(True)

    @pl.when(qi == nq)
    def _():
        run(False)


def _attention(lay, qt, k, vt, *, n_kv, grp, dq, kb):
    B, S, C = lay.B, lay.S, lay.C
    hb = kb * grp
    nq = S // TQ

    def qcol(b, qi):
        return jnp.where(qi == nq, lay.CB + b, b * nq + qi)

    in_specs = [
        pl.BlockSpec((hb * dq, TQ), lambda b, h, qi: (h, qcol(b, qi))),
        pl.BlockSpec((kb, C, DK), lambda b, h, qi: (h, lay.CB + b, 0)),
        pl.BlockSpec((kb, S, DK), lambda b, h, qi: (h, b, 0)),
        pl.BlockSpec((kb * DV, C), lambda b, h, qi: (h, lay.CB + b)),
        pl.BlockSpec((kb * DV, S), lambda b, h, qi: (h, b)),
    ]
    return pl.pallas_call(
        functools.partial(_attn_kernel, nq=nq, kb=kb, grp=grp, dq=dq, S=S, C=C),
        grid=(B, n_kv // kb, nq + 1),
        in_specs=in_specs,
        out_specs=pl.BlockSpec((hb * DV, TQ), lambda b, h, qi: (h, qcol(b, qi))),
        out_shape=jax.ShapeDtypeStruct((n_kv * grp * DV, lay.TOKP), BF16),
        scratch_shapes=[pltpu.VMEM((C + S, TQ), F32), pltpu.VMEM((C + S, TQ), F32)],
        compiler_params=_cparams(),
        name="attention",
    )(qt, k, k, vt, vt)


def _dft_kernel(a_ref, b_ref, c_ref, s_ref, o_ref):
    o_ref[...] = (_dot(a_ref[...], c_ref[...]) - _dot(b_ref[...], s_ref[...])).astype(o_ref.dtype)


def _seq_dft(lay, abt, cn, sn, cc, sc):
    B, S, C = lay.B, lay.S, lay.C
    fw = FOURIER_WIDTH
    nj = S // TN_DFT
    ylat = pl.pallas_call(
        _dft_kernel,
        grid=(nj, B),
        in_specs=[
            pl.BlockSpec((fw, S), lambda j, b: (0, b)),
            pl.BlockSpec((fw, S), lambda j, b: (1, b)),
            pl.BlockSpec((S, TN_DFT), lambda j, b: (0, j)),
            pl.BlockSpec((S, TN_DFT), lambda j, b: (0, j)),
        ],
        out_specs=pl.BlockSpec((fw, TN_DFT), lambda j, b: (0, b * nj + j)),
        out_shape=jax.ShapeDtypeStruct((fw, B * S), BF16),
        compiler_params=_cparams(),
        name="seq_dft",
    )(abt, abt, cn, sn)
    yctx = pl.pallas_call(
        _dft_kernel,
        grid=(B,),
        in_specs=[
            pl.BlockSpec((fw, C), lambda b: (0, lay.CB + b)),
            pl.BlockSpec((fw, C), lambda b: (1, lay.CB + b)),
            pl.BlockSpec((C, C), lambda b: (0, 0)),
            pl.BlockSpec((C, C), lambda b: (0, 0)),
        ],
        out_specs=pl.BlockSpec((fw, C), lambda b: (0, b)),
        out_shape=jax.ShapeDtypeStruct((fw, B * C), BF16),
        compiler_params=_cparams(),
        name="ctx_dft",
    )(abt, abt, cc, sc)
    return ylat, yctx


def _route_rows(lg, rbias):
    s = jax.nn.sigmoid(lg)
    sel = s + rbias
    rows = [sel[e:e + 1, :] for e in range(N_EXPERTS)]
    best = None
    bg = None
    for g in range(N_GROUPS):
        v = rows[4 * g:4 * g + 4]
        sc = None
        for i in range(4):
            for j in range(i + 1, 4):
                ps = v[i] + v[j]
                sc = ps if sc is None else jnp.maximum(sc, ps)
        if best is None:
            best, bg = sc, jnp.zeros_like(sc)
        else:
            upd = sc > best
            bg = jnp.where(upd, float(g), bg)
            best = jnp.where(upd, sc, best)
    v = []
    for i in range(4):
        acc = rows[i]
        for g in range(1, N_GROUPS):
            acc = jnp.where(bg == float(g), rows[4 * g + i], acc)
        v.append(acc)
    i0 = jnp.zeros_like(bg)
    b0 = v[0]
    for i in range(1, 4):
        upd = v[i] > b0
        i0 = jnp.where(upd, float(i), i0)
        b0 = jnp.where(upd, v[i], b0)
    w = [jnp.where(i0 == float(i), -jnp.inf, v[i]) for i in range(4)]
    i1 = jnp.zeros_like(bg)
    b1 = w[0]
    for i in range(1, 4):
        upd = w[i] > b1
        i1 = jnp.where(upd, float(i), i1)
        b1 = jnp.where(upd, w[i], b1)
    lo = jnp.minimum(i0, i1)
    hi = jnp.maximum(i0, i1)
    pbase = jnp.where(lo == 0.0, 0.0, jnp.where(lo == 1.0, 3.0, 5.0))
    return bg * 6.0 + pbase + hi - lo - 1.0


def _post_kernel(z_ref, mod_ref, a_ref, actx_ref, b_ref, wa_ref, wb_ref, g_ref, rwh_ref, rwl_ref, rb_ref, ut_ref,
                 zo_ref, h2_ref, info_ref, cnt_ref, *, nl_split):
    a = a_ref[...]
    if nl_split is not None:
        a = jnp.where(pl.program_id(0) < nl_split, a, actx_ref[...])
    attn = _tn(a, wa_ref[...]) + _tn(b_ref[...], wb_ref[...])
    x1 = z_ref[...] + mod_ref[0, 2:3, :] * attn
    zo_ref[...] = x1
    h = _norm_mod(x1, g_ref[...], mod_ref[0, 4:5, :], mod_ref[0, 3:4, :])
    _lanes_to_rows(h2_ref, h)
    hh = h.astype(BF16)
    hl = (h - hh.astype(F32)).astype(BF16)
    lg = _nt(rwh_ref[...], hh) + _nt(rwh_ref[...], hl) + _nt(rwl_ref[...], hh)
    cls = _route_rows(lg, rb_ref[...])
    ohf = (lax.broadcasted_iota(jnp.int32, (32, TM), 0).astype(F32) == cls).astype(F32)
    rank_all = _dot(ohf.astype(BF16), ut_ref[...])
    info_ref[0:1, :] = cls
    info_ref[1:2, :] = jnp.sum(ohf * rank_all, axis=0, keepdims=True)
    info_ref[2:8, :] = jnp.zeros((6, TM), F32)
    cnt_ref[0] = jnp.broadcast_to(jnp.sum(ohf, axis=1, keepdims=True), (32, 128))


def _post(lay, z, mod, src_a, src_a_ctx, src_b, blk_a, blk_b, wa, wb, g_ffn, rwh, rwl, rbias, ut):
    d = z.shape[1]
    nt = lay.NT
    fw = FOURIER_WIDTH
    if src_a_ctx is None:
        nl_split = None
        src_a_ctx = jnp.zeros((fw, TM), BF16)
        spec_a = pl.BlockSpec((fw, TM), lambda u: (blk_a, u))
        spec_actx = pl.BlockSpec((fw, TM), lambda u: (0, 0))
    else:
        nl_split = lay.NL
        spec_a = pl.BlockSpec((fw, TM), lambda u: (blk_a, jnp.minimum(u, lay.NL - 1)))
        spec_actx = pl.BlockSpec((fw, TM), lambda u: (0, jnp.maximum(u - lay.NL, 0)))
    in_specs = [
        pl.BlockSpec((TM, d), lambda u: (u, 0)),
        pl.BlockSpec((1, 6, d), lambda u: (lay.mrow(u), 0, 0)),
        spec_a,
        spec_actx,
        pl.BlockSpec((fw, TM), lambda u: (blk_b, u)),
        _full(wa), _full(wb), _full(g_ffn), _full(rwh), _full(rwl), _full(rbias), _full(ut),
    ]
    out_specs = [
        pl.BlockSpec((TM, d), lambda u: (u, 0)),
        pl.BlockSpec((TM, 8, 128), lambda u: (u, 0, 0)),
        pl.BlockSpec((8, TM), lambda u: (0, u)),
        pl.BlockSpec((1, 32, 128), lambda u: (u, 0, 0)),
    ]
    out_shape = [
        jax.ShapeDtypeStruct((lay.TOKP, d), F32),
        jax.ShapeDtypeStruct((lay.TOKP, 8, 128), F32),
        jax.ShapeDtypeStruct((8, lay.TOKP), F32),
        jax.ShapeDtypeStruct((nt, 32, 128), F32),
    ]
    return pl.pallas_call(
        functools.partial(_post_kernel, nl_split=nl_split),
        grid=(nt,), in_specs=in_specs, out_specs=out_specs, out_shape=out_shape,
        compiler_params=_cparams(), name="post",
    )(z, mod, src_a, src_a_ctx, src_b, wa, wb, g_ffn, rwh, rwl, rbias, ut)


def _invperm_kernel(pos_ref, tok_ref):
    def clear(p, c):
        tok_ref[p] = 0
        return c
    lax.fori_loop(0, tok_ref.shape[0], clear, 0, unroll=16)

    def put(t, c):
        tok_ref[pos_ref[t]] = t
        return c
    lax.fori_loop(0, pos_ref.shape[0], put, 0, unroll=16)


def _invperm(pos, pmax):
    return pl.pallas_call(
        _invperm_kernel,
        in_specs=[pl.BlockSpec(memory_space=pltpu.SMEM)],
        out_specs=pl.BlockSpec(memory_space=pltpu.SMEM),
        out_shape=jax.ShapeDtypeStruct((pmax,), jnp.int32),
        name="moe_invperm",
    )(pos)


def _expert_kernel(ea_ref, eb_ref, nu_ref, nv_ref, tok_ref, tokn_ref, h2_ref, rw_ref, wga_ref, wgb_ref, wda_ref,
                   wdb_ref, ys_ref, xbuf, ybuf, gsem, ssem):
    i = pl.program_id(0)
    nu = nu_ref[0]

    def gather_start(t_ref, slot):
        for r in range(TMM):
            pltpu.make_async_copy(h2_ref.at[t_ref[0, 0, r]], xbuf.at[slot, r], gsem.at[slot]).start(priority=0)

    def gather_wait(slot):
        pltpu.make_async_copy(h2_ref.at[pl.ds(0, TMM)], xbuf.at[slot], gsem.at[slot]).wait()

    def scatter_start(n):
        def body(r, c):
            pltpu.make_async_copy(ybuf.at[r], ys_ref.at[tok_ref[0, 0, r]], ssem).start(priority=1)
            return c
        lax.fori_loop(0, n, body, 0)

    def scatter_wait(n):
        k = TMM
        while k >= 1:
            @pl.when((n & k) != 0)
            def _(k=k):
                pltpu.make_async_copy(ybuf.at[pl.ds(0, k)], ys_ref.at[pl.ds(0, k)], ssem).wait()
            k //= 2

    @pl.when(i < nu)
    def _():
        slot = i % 2

        @pl.when(i == 0)
        def _():
            gather_start(tok_ref, 0)

        gather_wait(slot)
        gather_start(tokn_ref, 1 - slot)

        x = _rows_to_lanes(xbuf.at[slot])
        xb = x.astype(BF16)
        s = jax.nn.sigmoid(_dot(xb, rw_ref[...]))
        lane = lax.broadcasted_iota(jnp.int32, s.shape, 1)
        sa = jnp.sum(jnp.where(lane == ea_ref[i], s, 0.0), axis=1, keepdims=True)
        sb = jnp.sum(jnp.where(lane == eb_ref[i], s, 0.0), axis=1, keepdims=True)
        inv = 1.0 / (sa + sb)

        def hidden(w_ref, gate):
            gu = _dot(xb, w_ref[0])
            g = gu[:, :D_EXPERT]
            return (g * jax.nn.sigmoid(g) * gu[:, D_EXPERT:] * gate).astype(BF16)

        y = _dot(hidden(wga_ref, sa * inv), wda_ref[0]) + _dot(hidden(wgb_ref, sb * inv), wdb_ref[0])

        @pl.when(i > 0)
        def _():
            scatter_wait(nv_ref[i - 1])

        _lanes_to_rows(ybuf, y)
        scatter_start(nv_ref[i])

        @pl.when(i == nu - 1)
        def _():
            scatter_wait(nv_ref[i])
            gather_wait(1 - slot)


def _experts(lay, tok3, h2, ea, eb, nused, nvalid, rw, wgu, wd):
    ntile = tok3.shape[0]
    d = wd.shape[2]
    grid_spec = pltpu.PrefetchScalarGridSpec(
        num_scalar_prefetch=4,
        grid=(ntile,),
        in_specs=[
            pl.BlockSpec((1, 1, TMM), lambda i, *_: (i, 0, 0), memory_space=pltpu.SMEM),
            pl.BlockSpec((1, 1, TMM), lambda i, *_: (jnp.minimum(i + 1, ntile - 1), 0, 0), memory_space=pltpu.SMEM),
            pl.BlockSpec(memory_space=pl.ANY),
            pl.BlockSpec(rw.shape, lambda i, *_: (0, 0)),
            pl.BlockSpec((1, d, 2 * D_EXPERT), lambda i, ea, eb, nu, nv: (ea[i], 0, 0)),
            pl.BlockSpec((1, d, 2 * D_EXPERT), lambda i, ea, eb, nu, nv: (eb[i], 0, 0)),
            pl.BlockSpec((1, D_EXPERT, d), lambda i, ea, eb, nu, nv: (ea[i], 0, 0)),
            pl.BlockSpec((1, D_EXPERT, d), lambda i, ea, eb, nu, nv: (eb[i], 0, 0)),
        ],
        out_specs=pl.BlockSpec(memory_space=pl.ANY),
        scratch_shapes=[
            pltpu.VMEM((2, TMM, 8, 128), F32),
            pltpu.VMEM((TMM, 8, 128), F32),
            pltpu.SemaphoreType.DMA((2,)),
            pltpu.SemaphoreType.DMA(()),
        ],
    )
    return pl.pallas_call(
        _expert_kernel, grid_spec=grid_spec,
        out_shape=jax.ShapeDtypeStruct((lay.TOKP, 8, 128), F32),
        compiler_params=_cparams(), name="moe_experts",
    )(ea, eb, nused, nvalid, tok3, tok3, h2, rw, wgu, wgu, wd, wd)


def _final_kernel(z_ref, y_ref, mod_ref, gf_ref, o_ref):
    x2 = z_ref[...] + mod_ref[0, 5:6, :] * _rows_to_lanes(y_ref)
    ms = jnp.mean(x2 * x2, axis=-1, keepdims=True)
    o_ref[...] = x2 * lax.rsqrt(ms + EPS) * gf_ref[...]


def _final(lay, z, ys, mod, g_final):
    d = z.shape[1]
    return pl.pallas_call(
        _final_kernel,
        grid=(lay.NL,),
        in_specs=[
            pl.BlockSpec((TM, d), lambda u: (u, 0)),
            pl.BlockSpec((TM, 8, 128), lambda u: (u, 0, 0)),
            pl.BlockSpec((1, 6, d), lambda u: (lay.mrow(u), 0, 0)),
            _full(g_final),
        ],
        out_specs=pl.BlockSpec((TM, d), lambda u: (u, 0)),
        out_shape=jax.ShapeDtypeStruct((lay.NL * TM, d), F32),
        compiler_params=_cparams(), name="final_norm",
    )(z, ys, mod, g_final)


def _plan(info, cnt, nt, ntile_max):
    cls = info[0].astype(jnp.int32).reshape(nt, TM)
    rank = info[1].astype(jnp.int32).reshape(nt, TM)
    cnt = cnt[:, :N_CLASSES, 0].astype(jnp.int32)
    tot = jnp.sum(cnt, axis=0)
    ntile_c = (tot + TMM - 1) // TMM
    tile_end = jnp.cumsum(ntile_c)
    class_off = (tile_end - ntile_c) * TMM
    tile_off = jnp.cumsum(cnt, axis=0) - cnt
    base = class_off[None, :] + tile_off
    onehot = cls[:, :, None] == jnp.arange(N_CLASSES, dtype=jnp.int32)[None, None, :]
    pos = jnp.sum(jnp.where(onehot, base[:, None, :], 0), axis=-1) + rank
    nused = tile_end[-1]
    ti = jnp.minimum(jnp.arange(ntile_max, dtype=jnp.int32), nused - 1)
    tcls = jnp.sum((ti[:, None] >= tile_end[None, :]).astype(jnp.int32), axis=1)
    onec = tcls[:, None] == jnp.arange(N_CLASSES, dtype=jnp.int32)[None, :]
    class_end = jnp.sum(jnp.where(onec, (class_off + tot)[None, :], 0), axis=1)
    nvalid = jnp.clip(class_end - ti * TMM, 0, TMM)
    grp, pair = tcls // 6, tcls % 6
    lo = jnp.asarray(_PAIR_LO, jnp.int32)
    hi = jnp.asarray(_PAIR_HI, jnp.int32)
    onep = pair[:, None] == jnp.arange(6, dtype=jnp.int32)[None, :]
    ea = 4 * grp + jnp.sum(jnp.where(onep, lo[None, :], 0), axis=1)
    eb = 4 * grp + jnp.sum(jnp.where(onep, hi[None, :], 0), axis=1)
    return pos.reshape(nt * TM), ea, eb, nused.reshape(1), nvalid


def _dft_mats(n, scale):
    r = 64 if (n % 64 == 0 and n > 64) else 1
    k = lax.broadcasted_iota(jnp.int32, (1, n), 1)

    def cs(rows, step):
        j = lax.broadcasted_iota(jnp.int32, (rows, 1), 0) * step
        ang = ((j * k) % n).astype(F32) * (2.0 * math.pi / n)
        return jnp.cos(ang), jnp.sin(ang)

    ca, sa = cs(n // r, r)
    if r == 1:
        c, s = ca, sa
    else:
        cb, sb = cs(r, 1)
        c = (ca[:, None, :] * cb[None, :, :] - sa[:, None, :] * sb[None, :, :]).reshape(n, n)
        s = (sa[:, None, :] * cb[None, :, :] + ca[:, None, :] * sb[None, :, :]).reshape(n, n)
    return (c * scale).astype(BF16), (s * scale).astype(BF16)


def _rope_angles(S, dim):
    rows = S // GRID_W
    row_id = jnp.repeat(jnp.arange(rows, dtype=F32), GRID_W)
    col_id = jnp.tile(jnp.arange(GRID_W, dtype=F32), rows)
    n_freq = dim // 4
    inv = ROPE_BASE ** (-jnp.arange(n_freq, dtype=F32) / n_freq)
    return jnp.concatenate([row_id[:, None] * inv, col_id[:, None] * inv], axis=-1)


def _tables(lay):
    S, C = lay.S, lay.C
    tabs = {}
    cc, sc = _dft_mats(FOURIER_GROUP_DIM, FOURIER_GROUP_DIM ** -0.5)
    tabs["cs"] = jnp.concatenate([cc, sc], axis=0)
    tabs["cn"], tabs["sn"] = _dft_mats(S, S ** -0.5)
    tabs["cctx"], tabs["sctx"] = _dft_mats(C, C ** -0.5)
    for name, dim in (("mla", MLA_ROPE_DIM), ("gqa", GQA_HEAD_DIM)):
        ang = _rope_angles(S, dim)
        cos = jnp.concatenate([jnp.ones((TM, dim // 2), F32), jnp.cos(ang)], axis=0)
        sin = jnp.concatenate([jnp.zeros((TM, dim // 2), F32), jnp.sin(ang)], axis=0)
        tabs[name + "_ct"] = cos.T
        tabs[name + "_st"] = sin.T
        if name == "mla":
            pad = jnp.zeros((TM + S, 128 - dim), F32)
            tabs["mla_t1"] = jnp.concatenate([cos, cos, pad], axis=1)
            tabs["mla_t2"] = jnp.concatenate([sin, -sin, pad], axis=1)
    tabs["ut"] = (lax.broadcasted_iota(jnp.int32, (TM, TM), 0)
                  < lax.broadcasted_iota(jnp.int32, (TM, TM), 1)).astype(BF16)
    return tabs


def _col(g, rows):
    return jnp.broadcast_to(g.astype(F32)[:, None], (rows, TM))


def _even_weights(w_in, q_norm_g, w_qb, kv_norm_g, w_kvb, w_out):
    d = w_in.shape[0]
    o1 = FOURIER_WIDTH
    o2 = o1 + MLA_Q_RANK
    o3 = o2 + MLA_KV_RANK
    bf = lambda a: a.astype(BF16)
    w = {}
    w["wf"] = bf(w_in[:, :o1].T)
    w["wqa"] = bf(w_in[:, o1:o2].T)
    w["wkvat"] = bf(w_in[:, o2:o3].T)
    w["wkva"] = bf(w_in[:, o2:o3])
    w["wpr"] = bf(jnp.concatenate([w_in[:, o3:], jnp.zeros((d, 128 - MLA_ROPE_DIM), F32)], axis=1))
    w["gq"] = _col(q_norm_g, MLA_Q_RANK)
    w["wqb"] = bf(w_qb.T)
    w["gkvc"] = _col(kv_norm_g, MLA_KV_RANK)
    w["gkvr"] = kv_norm_g.astype(F32)[None, :]
    kvb = w_kvb.reshape(MLA_KV_RANK, MLA_HEADS, MLA_NOPE_DIM + MLA_V_DIM)
    w["wv"] = bf(kvb[:, :, MLA_NOPE_DIM:].reshape(MLA_KV_RANK, MLA_HEADS * MLA_V_DIM).T)
    w1 = jnp.concatenate([kvb[:, :, :MLA_NOPE_DIM],
                          jnp.zeros((MLA_KV_RANK, MLA_HEADS, DK - MLA_NOPE_DIM), F32)], axis=2)
    w1 = w1.reshape(MLA_KV_RANK, MLA_HEADS * DK)
    eye = jnp.eye(MLA_ROPE_DIM, dtype=F32)
    place = jnp.zeros((128, DK), F32).at[:MLA_ROPE_DIM, MLA_NOPE_DIM:MLA_QK_DIM].set(eye)
    swap = jnp.roll(eye, MLA_ROPE_DIM // 2, axis=1)
    place_s = jnp.zeros((128, DK), F32).at[:MLA_ROPE_DIM, MLA_NOPE_DIM:MLA_QK_DIM].set(swap)
    w["wk"] = bf(jnp.concatenate([w1, jnp.tile(place, (1, MLA_HEADS)), jnp.tile(place_s, (1, MLA_HEADS))], axis=0))
    w["wout_a"] = bf(w_out[:o1])
    w["wout_b"] = bf(w_out[o1:])
    return w


def _odd_weights(w_qkv, q_norm_g, k_norm_g, w_out):
    bf = lambda a: a.astype(BF16)
    qw = GQA_HEADS * GQA_HEAD_DIM
    kw = GQA_KV_HEADS * GQA_HEAD_DIM
    w = {}
    w["wq"] = bf(w_qkv[:, :qw].T)
    w["wk"] = bf(w_qkv[:, qw:qw + kw].T)
    w["wv"] = bf(w_qkv[:, qw + kw:].T)
    w["gq"] = _col(q_norm_g, GQA_HEAD_DIM)
    w["gk"] = _col(k_norm_g, GQA_HEAD_DIM)
    w["wout_a"] = bf(w_out[:FOURIER_WIDTH])
    w["wout_b"] = bf(w_out[FOURIER_WIDTH:])
    return w


def kernel(x, c, ctx, c_ctx, ada_w, ada_b, norm_mix_g, norm_ffn_g, ev_w_in, ev_q_norm_g, ev_w_qb, ev_kv_norm_g,
           ev_w_kvb, ev_w_out, od_w_qkv, od_q_norm_g, od_k_norm_g, od_w_out, router_w, router_b, exp_w_gate,
           exp_w_up, exp_w_down, final_norm_g):
    B, S, d = x.shape
    C = ctx.shape[1]
    depth = ada_w.shape[0]
    lay = _Layout(B, S, C)
    assert B + 1 <= 16
    tabs = _tables(lay)

    cond = jnp.concatenate([c, c_ctx[None, :], jnp.zeros((16 - B - 1, d), F32)], axis=0)
    mods = _modulation(cond, ada_w, ada_b).reshape(depth, 16, 6, d)

    z = jnp.concatenate([x.reshape(B * S, d), ctx.reshape(B * C, d)], axis=0)

    rwt = router_w.T.astype(F32)
    rwh = rwt.astype(BF16)
    rwl = (rwt - rwh.astype(F32)).astype(BF16)
    rbias = jnp.broadcast_to(router_b.astype(F32)[:, None], (N_EXPERTS, TM))
    rw_nat = jnp.concatenate([router_w, jnp.zeros((d, 128 - N_EXPERTS), F32)], axis=1).astype(BF16)
    wgu_all = jnp.concatenate([exp_w_gate, exp_w_up], axis=-1).astype(BF16)
    wd_all = exp_w_down.astype(BF16)
    ntile_max = lay.NT * TM // TMM + N_CLASSES

    zin = (z,)
    for i in range(depth):
        mod = mods[i]
        g_mix = norm_mix_g[i][None, :]
        g_ffn = norm_ffn_g[i][None, :]
        j = i // 2
        if i % 2 == 0:
            w = _even_weights(ev_w_in[j], ev_q_norm_g[j], ev_w_qb[j], ev_kv_norm_g[j], ev_w_kvb[j], ev_w_out[j])
            abt, qt, k, vt, *znew = _pre_even(lay, zin, mod, g_mix, w, tabs)
            ot = _attention(lay, qt, k, vt, n_kv=MLA_HEADS, grp=1, dq=DK, kb=4)
            src_a, src_a_ctx = _seq_dft(lay, abt, tabs["cn"], tabs["sn"], tabs["cctx"], tabs["sctx"])
            src_b, blk_a, blk_b = ot, 0, 0
        else:
            w = _odd_weights(od_w_qkv[j], od_q_norm_g[j], od_k_norm_g[j], od_w_out[j])
            qt, k, vt, *znew = _pre_odd(lay, zin, mod, g_mix, w, tabs)
            ot = _attention(lay, qt, k, vt, n_kv=GQA_KV_HEADS, grp=GQA_GROUP, dq=GQA_HEAD_DIM, kb=2)
            src_a, src_a_ctx, src_b, blk_a, blk_b = ot, None, ot, 0, 1
        z = znew[0] if znew else zin[0]
        z1, h2, info, cnt = _post(lay, z, mod, src_a, src_a_ctx, src_b, blk_a, blk_b, w["wout_a"], w["wout_b"],
                                  g_ffn, rwh, rwl, rbias, tabs["ut"])
        pos, ea, eb, nused, nvalid = _plan(info, cnt, lay.NT, ntile_max)
        tok3 = _invperm(pos, ntile_max * TMM).reshape(ntile_max, 1, TMM)
        ys = _experts(lay, tok3, h2, ea, eb, nused, nvalid, rw_nat, wgu_all[i], wd_all[i])
        zin = (z1, ys, mod)
    out = _final(lay, zin[0], zin[1], zin[2], final_norm_g[None, :])
    return out.reshape(B, S, d)
```

```python
import functools
import math

import jax
import jax.numpy as jnp
from jax import lax
from jax.experimental import pallas as pl
from jax.experimental.pallas import tpu as pltpu

F32 = jnp.float32
BF16 = jnp.bfloat16

TM = 512
TQ = 256
TKC = 512
TMM = 256
TN_DFT = 512
EPS = 1e-6
LOG2E = 1.4426950408889634
ROPE_BASE = 10000.0
GRID_W = 64

FOURIER_GROUPS = 4
FOURIER_GROUP_DIM = 128
FOURIER_WIDTH = 512
MLA_HEADS = 8
MLA_NOPE_DIM = 64
MLA_ROPE_DIM = 32
MLA_V_DIM = 64
MLA_QK_DIM = 96
MLA_Q_RANK = 256
MLA_KV_RANK = 128
MLA_SCALE = MLA_QK_DIM ** -0.5
GQA_HEADS = 16
GQA_KV_HEADS = 4
GQA_HEAD_DIM = 64
GQA_GROUP = 4
GQA_SCALE = GQA_HEAD_DIM ** -0.5
N_EXPERTS = 16
N_GROUPS = 4
N_CLASSES = 24
D_EXPERT = 512
DK = 128
DV = 64
VMEM_LIMIT = 56 * 1024 * 1024

_PAIR_LO = (0, 0, 0, 1, 1, 2)
_PAIR_HI = (1, 2, 3, 2, 3, 3)


def _nt(a, b):
    return lax.dot_general(a, b, (((1,), (1,)), ((), ())), preferred_element_type=F32)


def _tn(a, b):
    return lax.dot_general(a, b, (((0,), (0,)), ((), ())), preferred_element_type=F32)


def _dot(a, b):
    return jnp.dot(a, b, preferred_element_type=F32)


class _Layout:
    def __init__(self, B, S, C):
        assert C == TQ and TM % C == 0 and (B * C) % TM == 0 and S % TM == 0 and S % TN_DFT == 0 and S % TKC == 0
        self.B, self.S, self.C = B, S, C
        self.SB = S // TM
        self.NL = B * self.SB
        self.NT = self.NL + B * C // TM
        self.TOKP = self.NT * TM
        self.CB = B * S // C

    def mrow(self, u):
        return jnp.where(u < self.NL, u // self.SB, self.B)

    def ropeblk(self, u):
        return jnp.where(u < self.NL, 1 + u % self.SB, 0)


def _cparams():
    return pltpu.CompilerParams(vmem_limit_bytes=VMEM_LIMIT)


def _full(a):
    return pl.BlockSpec(a.shape, lambda *_: (0,) * a.ndim)


def _mod_kernel(s_ref, w_ref, b_ref, o_ref):
    s = s_ref[...]
    a = s * jax.nn.sigmoid(s)
    ah = a.astype(BF16)
    al = (a - ah.astype(F32)).astype(BF16)
    w = w_ref[0]
    wh = w.astype(BF16)
    wl = (w - wh.astype(F32)).astype(BF16)
    o_ref[0] = _dot(ah, wh) + _dot(ah, wl) + _dot(al, wh) + b_ref[0]


def _modulation(cond, ada_w, ada_b):
    depth, d, w6 = ada_w.shape
    tn = 768
    return pl.pallas_call(
        _mod_kernel,
        grid=(depth, w6 // tn),
        in_specs=[
            pl.BlockSpec((16, d), lambda i, j: (0, 0)),
            pl.BlockSpec((1, d, tn), lambda i, j: (i, 0, j)),
            pl.BlockSpec((1, 1, tn), lambda i, j: (i, 0, j)),
        ],
        out_specs=pl.BlockSpec((1, 16, tn), lambda i, j: (i, 0, j)),
        out_shape=jax.ShapeDtypeStruct((depth, 16, w6), F32),
        compiler_params=_cparams(),
        name="modulation",
    )(cond, ada_w, ada_b.reshape(depth, 1, w6))


def _norm_mod(z, g, sc, sh):
    ms = jnp.mean(z * z, axis=-1, keepdims=True)
    return (z * lax.rsqrt(ms + EPS) * g) * (1.0 + sc) + sh


def _zin_specs(lay, zin, d):
    row = pl.BlockSpec((TM, d), lambda u: (u, 0))
    if len(zin) == 1:
        return [row], [], []
    specs = [row, row, pl.BlockSpec((1, 6, d), lambda u: (lay.mrow(u), 0, 0))]
    return specs, [row], [jax.ShapeDtypeStruct((lay.TOKP, d), F32)]


def _zin_value(zin_refs, zout_refs):
    if len(zin_refs) == 1:
        return zin_refs[0][...]
    z_ref, y_ref, modp_ref = zin_refs
    z = z_ref[...] + modp_ref[0, 5:6, :] * y_ref[...]
    zout_refs[0][...] = z
    return z


def _pre_even_kernel(*refs, n_zin):
    (mod_ref, g_ref, wf_ref, wqa_ref, wkvat_ref, wkva_ref, wpr_ref, gq_ref, wqb_ref, gkvc_ref, gkvr_ref, wv_ref,
     wk_ref, cs_ref, ct_ref, st_ref, t1_ref, t2_ref, ab_ref, q_ref, k_ref, v_ref) = refs[n_zin:n_zin + 22]
    z = _zin_value(refs[:n_zin], refs[n_zin + 22:])
    h = _norm_mod(z, g_ref[...], mod_ref[0, 1:2, :], mod_ref[0, 0:1, :])
    hb = h.astype(BF16)

    ft = _nt(wf_ref[...], hb).astype(BF16)
    gd = FOURIER_GROUP_DIM
    for g in range(FOURIER_GROUPS):
        ab = _dot(cs_ref[...], ft[g * gd:(g + 1) * gd, :])
        ab_ref[g * gd:(g + 1) * gd, :] = ab[:gd].astype(BF16)
        ab_ref[FOURIER_WIDTH + g * gd:FOURIER_WIDTH + (g + 1) * gd, :] = ab[gd:].astype(BF16)

    pq = _nt(wqa_ref[...], hb)
    rs = lax.rsqrt(jnp.mean(pq * pq, axis=0, keepdims=True) + EPS)
    qn = (pq * rs * gq_ref[...]).astype(BF16)
    qt = _dot(wqb_ref[...], qn) * (MLA_SCALE * LOG2E)
    cos = ct_ref[...]
    sin = st_ref[...]
    hr = MLA_ROPE_DIM // 2
    for hd in range(MLA_HEADS):
        o = hd * MLA_QK_DIM
        r = hd * DK
        x1 = qt[o + MLA_NOPE_DIM:o + MLA_NOPE_DIM + hr, :]
        x2 = qt[o + MLA_NOPE_DIM + hr:o + MLA_QK_DIM, :]
        q_ref[r:r + MLA_NOPE_DIM, :] = qt[o:o + MLA_NOPE_DIM, :].astype(BF16)
        q_ref[r + MLA_NOPE_DIM:r + MLA_NOPE_DIM + hr, :] = (x1 * cos - x2 * sin).astype(BF16)
        q_ref[r + MLA_NOPE_DIM + hr:r + MLA_QK_DIM, :] = (x1 * sin + x2 * cos).astype(BF16)
        q_ref[r + MLA_QK_DIM:r + DK, :] = jnp.zeros((DK - MLA_QK_DIM, TM), BF16)

    pkvt = _nt(wkvat_ref[...], hb)
    rst = lax.rsqrt(jnp.mean(pkvt * pkvt, axis=0, keepdims=True) + EPS)
    kvnt = (pkvt * rst * gkvc_ref[...]).astype(BF16)
    v_ref[...] = _dot(wv_ref[...], kvnt).astype(BF16)

    pkv = _dot(hb, wkva_ref[...])
    rsn = lax.rsqrt(jnp.mean(pkv * pkv, axis=-1, keepdims=True) + EPS)
    kvn = (pkv * rsn * gkvr_ref[...]).astype(BF16)
    pr = _dot(hb, wpr_ref[...])
    lhs = jnp.concatenate([kvn, (pr * t1_ref[...]).astype(BF16), (pr * t2_ref[...]).astype(BF16)], axis=1)
    kf = _dot(lhs, wk_ref[...])
    for hd in range(MLA_HEADS):
        k_ref[hd] = kf[:, hd * DK:(hd + 1) * DK].astype(BF16)


def _pre_even(lay, zin, mod, g_mix, w, tabs):
    d = mod.shape[-1]
    tokp = lay.TOKP
    col = lambda rows: pl.BlockSpec((rows, TM), lambda u: (0, u))
    hr = MLA_ROPE_DIM // 2
    zin_specs, zout_specs, zout_shape = _zin_specs(lay, zin, d)
    in_specs = zin_specs + [
        pl.BlockSpec((1, 6, d), lambda u: (lay.mrow(u), 0, 0)),
        _full(g_mix),
        _full(w["wf"]), _full(w["wqa"]), _full(w["wkvat"]), _full(w["wkva"]), _full(w["wpr"]),
        _full(w["gq"]), _full(w["wqb"]), _full(w["gkvc"]), _full(w["gkvr"]), _full(w["wv"]), _full(w["wk"]),
        _full(tabs["cs"]),
        pl.BlockSpec((hr, TM), lambda u: (0, lay.ropeblk(u))),
        pl.BlockSpec((hr, TM), lambda u: (0, lay.ropeblk(u))),
        pl.BlockSpec((TM, 128), lambda u: (lay.ropeblk(u), 0)),
        pl.BlockSpec((TM, 128), lambda u: (lay.ropeblk(u), 0)),
    ]
    out_specs = [
        col(2 * FOURIER_WIDTH),
        col(MLA_HEADS * DK),
        pl.BlockSpec((MLA_HEADS, TM, DK), lambda u: (0, u, 0)),
        col(MLA_HEADS * DV),
    ]
    out_shape = [
        jax.ShapeDtypeStruct((2 * FOURIER_WIDTH, tokp), BF16),
        jax.ShapeDtypeStruct((MLA_HEADS * DK, tokp), BF16),
        jax.ShapeDtypeStruct((MLA_HEADS, tokp, DK), BF16),
        jax.ShapeDtypeStruct((MLA_HEADS * DV, tokp), BF16),
    ]
    return pl.pallas_call(
        functools.partial(_pre_even_kernel, n_zin=len(zin)),
        grid=(lay.NT,), in_specs=in_specs, out_specs=out_specs + zout_specs, out_shape=out_shape + zout_shape,
        compiler_params=_cparams(), name="pre_even",
    )(*zin, mod, g_mix, w["wf"], w["wqa"], w["wkvat"], w["wkva"], w["wpr"], w["gq"], w["wqb"], w["gkvc"],
      w["gkvr"], w["wv"], w["wk"], tabs["cs"], tabs["mla_ct"], tabs["mla_st"], tabs["mla_t1"], tabs["mla_t2"])


def _pre_odd_kernel(*refs, n_zin):
    (mod_ref, g_ref, wq_ref, wk_ref, wv_ref, gq_ref, gk_ref, ct_ref, st_ref,
     q_ref, k_ref, v_ref) = refs[n_zin:n_zin + 12]
    z = _zin_value(refs[:n_zin], refs[n_zin + 12:])
    h = _norm_mod(z, g_ref[...], mod_ref[0, 1:2, :], mod_ref[0, 0:1, :])
    hb = h.astype(BF16)
    cos = ct_ref[...]
    sin = st_ref[...]
    hd2 = GQA_HEAD_DIM // 2

    def norm_rope(xt, gain, scale):
        rs = lax.rsqrt(jnp.mean(xt * xt, axis=0, keepdims=True) + EPS)
        xn = xt * rs * gain
        x1 = xn[:hd2]
        x2 = xn[hd2:]
        return (x1 * cos - x2 * sin) * scale, (x1 * sin + x2 * cos) * scale

    qt = _nt(wq_ref[...], hb)
    for hd in range(GQA_HEADS):
        o = hd * GQA_HEAD_DIM
        r1, r2 = norm_rope(qt[o:o + GQA_HEAD_DIM], gq_ref[...], GQA_SCALE * LOG2E)
        q_ref[o:o + hd2, :] = r1.astype(BF16)
        q_ref[o + hd2:o + GQA_HEAD_DIM, :] = r2.astype(BF16)

    kt = _nt(wk_ref[...], hb)
    zero = jnp.zeros((DK - GQA_HEAD_DIM, TM), F32)
    for hd in range(GQA_KV_HEADS):
        o = hd * GQA_HEAD_DIM
        r1, r2 = norm_rope(kt[o:o + GQA_HEAD_DIM], gk_ref[...], 1.0)
        khead = jnp.concatenate([r1, r2, zero], axis=0)
        k_ref[hd] = khead.T.astype(BF16)

    v_ref[...] = _nt(wv_ref[...], hb).astype(BF16)


def _pre_odd(lay, zin, mod, g_mix, w, tabs):
    d = mod.shape[-1]
    tokp = lay.TOKP
    col = lambda rows: pl.BlockSpec((rows, TM), lambda u: (0, u))
    hd2 = GQA_HEAD_DIM // 2
    zin_specs, zout_specs, zout_shape = _zin_specs(lay, zin, d)
    in_specs = zin_specs + [
        pl.BlockSpec((1, 6, d), lambda u: (lay.mrow(u), 0, 0)),
        _full(g_mix), _full(w["wq"]), _full(w["wk"]), _full(w["wv"]), _full(w["gq"]), _full(w["gk"]),
        pl.BlockSpec((hd2, TM), lambda u: (0, lay.ropeblk(u))),
        pl.BlockSpec((hd2, TM), lambda u: (0, lay.ropeblk(u))),
    ]
    out_specs = [
        col(GQA_HEADS * GQA_HEAD_DIM),
        pl.BlockSpec((GQA_KV_HEADS, TM, DK), lambda u: (0, u, 0)),
        col(GQA_KV_HEADS * DV),
    ]
    out_shape = [
        jax.ShapeDtypeStruct((GQA_HEADS * GQA_HEAD_DIM, tokp), BF16),
        jax.ShapeDtypeStruct((GQA_KV_HEADS, tokp, DK), BF16),
        jax.ShapeDtypeStruct((GQA_KV_HEADS * DV, tokp), BF16),
    ]
    return pl.pallas_call(
        functools.partial(_pre_odd_kernel, n_zin=len(zin)),
        grid=(lay.NT,), in_specs=in_specs, out_specs=out_specs + zout_specs, out_shape=out_shape + zout_shape,
        compiler_params=_cparams(), name="pre_odd",
    )(*zin, mod, g_mix, w["wq"], w["wk"], w["wv"], w["gq"], w["gk"], tabs["gqa_ct"], tabs["gqa_st"])


def _attn_kernel(q_ref, kc_ref, kl_ref, vc_ref, vl_ref, o_ref, s0_ref, s1_ref, *, nq, kb, grp, dq, S, C):
    qi = pl.program_id(2)
    hb = kb * grp
    sbufs = (s0_ref, s1_ref)
    nchunk = S // TKC

    def scores(hh, buf, with_latent):
        kv = hh // grp
        q = q_ref[hh * dq:(hh + 1) * dq, :]
        sc = _dot(kc_ref[kv, :, :dq], q)
        buf[0:C, :] = sc
        m = jnp.max(sc, axis=0, keepdims=True)
        if with_latent:
            for c in range(nchunk):
                s = _dot(kl_ref[kv, c * TKC:(c + 1) * TKC, :dq], q)
                buf[C + c * TKC:C + (c + 1) * TKC, :] = s
                m = jnp.maximum(m, jnp.max(s, axis=0, keepdims=True))
        return m

    def values(hh, buf, m, with_latent):
        kv = hh // grp
        vrows = slice(kv * DV, (kv + 1) * DV)
        p = jnp.exp2(buf[0:C, :] - m)
        l = jnp.sum(p, axis=0, keepdims=True)
        o = _dot(vc_ref[vrows, :], p.astype(BF16))
        if with_latent:
            for c in range(nchunk):
                p = jnp.exp2(buf[C + c * TKC:C + (c + 1) * TKC, :] - m)
                l = l + jnp.sum(p, axis=0, keepdims=True)
                o = o + _dot(vl_ref[vrows, c * TKC:(c + 1) * TKC], p.astype(BF16))
        o_ref[hh * DV:(hh + 1) * DV, :] = (o * (1.0 / l)).astype(o_ref.dtype)

    def run(with_latent):
        m = scores(0, sbufs[0], with_latent)
        for hh in range(hb):
            m_next = scores(hh + 1, sbufs[(hh + 1) % 2], with_latent) if hh + 1 < hb else None
            values(hh, sbufs[hh % 2], m, with_latent)
            m = m_next

    @pl.when(qi < nq)
    def _():
        run(True)

    @pl.when(qi == nq)
    def _():
        run(False)


def _attention(lay, qt, k, vt, *, n_kv, grp, dq, kb):
    B, S, C = lay.B, lay.S, lay.C
    hb = kb * grp
    nq = S // TQ

    def qcol(b, qi):
        return jnp.where(qi == nq, lay.CB + b, b * nq + qi)

    in_specs = [
        pl.BlockSpec((hb * dq, TQ), lambda b, h, qi: (h, qcol(b, qi))),
        pl.BlockSpec((kb, C, DK), lambda b, h, qi: (h, lay.CB + b, 0)),
        pl.BlockSpec((kb, S, DK), lambda b, h, qi: (h, b, 0)),
        pl.BlockSpec((kb * DV, C), lambda b, h, qi: (h, lay.CB + b)),
        pl.BlockSpec((kb * DV, S), lambda b, h, qi: (h, b)),
    ]
    return pl.pallas_call(
        functools.partial(_attn_kernel, nq=nq, kb=kb, grp=grp, dq=dq, S=S, C=C),
        grid=(B, n_kv // kb, nq + 1),
        in_specs=in_specs,
        out_specs=pl.BlockSpec((hb * DV, TQ), lambda b, h, qi: (h, qcol(b, qi))),
        out_shape=jax.ShapeDtypeStruct((n_kv * grp * DV, lay.TOKP), BF16),
        scratch_shapes=[pltpu.VMEM((C + S, TQ), F32), pltpu.VMEM((C + S, TQ), F32)],
        compiler_params=_cparams(),
        name="attention",
    )(qt, k, k, vt, vt)


def _dft_kernel(a_ref, b_ref, c_ref, s_ref, o_ref):
    o_ref[...] = (_dot(a_ref[...], c_ref[...]) - _dot(b_ref[...], s_ref[...])).astype(o_ref.dtype)


def _seq_dft(lay, abt, cn, sn, cc, sc):
    B, S, C = lay.B, lay.S, lay.C
    fw = FOURIER_WIDTH
    nj = S // TN_DFT
    ylat = pl.pallas_call(
        _dft_kernel,
        grid=(nj, B),
        in_specs=[
            pl.BlockSpec((fw, S), lambda j, b: (0, b)),
            pl.BlockSpec((fw, S), lambda j, b: (1, b)),
            pl.BlockSpec((S, TN_DFT), lambda j, b: (0, j)),
            pl.BlockSpec((S, TN_DFT), lambda j, b: (0, j)),
        ],
        out_specs=pl.BlockSpec((fw, TN_DFT), lambda j, b: (0, b * nj + j)),
        out_shape=jax.ShapeDtypeStruct((fw, B * S), BF16),
        compiler_params=_cparams(),
        name="seq_dft",
    )(abt, abt, cn, sn)
    yctx = pl.pallas_call(
        _dft_kernel,
        grid=(B,),
        in_specs=[
            pl.BlockSpec((fw, C), lambda b: (0, lay.CB + b)),
            pl.BlockSpec((fw, C), lambda b: (1, lay.CB + b)),
            pl.BlockSpec((C, C), lambda b: (0, 0)),
            pl.BlockSpec((C, C), lambda b: (0, 0)),
        ],
        out_specs=pl.BlockSpec((fw, C), lambda b: (0, b)),
        out_shape=jax.ShapeDtypeStruct((fw, B * C), BF16),
        compiler_params=_cparams(),
        name="ctx_dft",
    )(abt, abt, cc, sc)
    return ylat, yctx


def _route_rows(lg, rbias):
    s = jax.nn.sigmoid(lg)
    sel = s + rbias
    rows = [sel[e:e + 1, :] for e in range(N_EXPERTS)]
    best = None
    bg = None
    for g in range(N_GROUPS):
        v = rows[4 * g:4 * g + 4]
        sc = None
        for i in range(4):
            for j in range(i + 1, 4):
                ps = v[i] + v[j]
                sc = ps if sc is None else jnp.maximum(sc, ps)
        if best is None:
            best, bg = sc, jnp.zeros_like(sc)
        else:
            upd = sc > best
            bg = jnp.where(upd, float(g), bg)
            best = jnp.where(upd, sc, best)
    v = []
    for i in range(4):
        acc = rows[i]
        for g in range(1, N_GROUPS):
            acc = jnp.where(bg == float(g), rows[4 * g + i], acc)
        v.append(acc)
    i0 = jnp.zeros_like(bg)
    b0 = v[0]
    for i in range(1, 4):
        upd = v[i] > b0
        i0 = jnp.where(upd, float(i), i0)
        b0 = jnp.where(upd, v[i], b0)
    w = [jnp.where(i0 == float(i), -jnp.inf, v[i]) for i in range(4)]
    i1 = jnp.zeros_like(bg)
    b1 = w[0]
    for i in range(1, 4):
        upd = w[i] > b1
        i1 = jnp.where(upd, float(i), i1)
        b1 = jnp.where(upd, w[i], b1)
    lo = jnp.minimum(i0, i1)
    hi = jnp.maximum(i0, i1)
    pbase = jnp.where(lo == 0.0, 0.0, jnp.where(lo == 1.0, 3.0, 5.0))
    return bg * 6.0 + pbase + hi - lo - 1.0


def _post_kernel(z_ref, mod_ref, a_ref, actx_ref, b_ref, wa_ref, wb_ref, g_ref, rwh_ref, rwl_ref, rb_ref, ut_ref,
                 zo_ref, h2_ref, info_ref, cnt_ref, *, nl_split):
    a = a_ref[...]
    if nl_split is not None:
        a = jnp.where(pl.program_id(0) < nl_split, a, actx_ref[...])
    attn = _tn(a, wa_ref[...]) + _tn(b_ref[...], wb_ref[...])
    x1 = z_ref[...] + mod_ref[0, 2:3, :] * attn
    zo_ref[...] = x1
    h = _norm_mod(x1, g_ref[...], mod_ref[0, 4:5, :], mod_ref[0, 3:4, :])
    h2_ref[...] = h
    hh = h.astype(BF16)
    hl = (h - hh.astype(F32)).astype(BF16)
    lg = _nt(rwh_ref[...], hh) + _nt(rwh_ref[...], hl) + _nt(rwl_ref[...], hh)
    cls = _route_rows(lg, rb_ref[...])
    ohf = (lax.broadcasted_iota(jnp.int32, (32, TM), 0).astype(F32) == cls).astype(F32)
    rank_all = _dot(ohf.astype(BF16), ut_ref[...])
    info_ref[0:1, :] = cls
    info_ref[1:2, :] = jnp.sum(ohf * rank_all, axis=0, keepdims=True)
    info_ref[2:8, :] = jnp.zeros((6, TM), F32)
    cnt_ref[0] = jnp.broadcast_to(jnp.sum(ohf, axis=1, keepdims=True), (32, 128))


def _post(lay, z, mod, src_a, src_a_ctx, src_b, blk_a, blk_b, wa, wb, g_ffn, rwh, rwl, rbias, ut):
    d = z.shape[1]
    nt = lay.NT
    fw = FOURIER_WIDTH
    if src_a_ctx is None:
        nl_split = None
        src_a_ctx = jnp.zeros((fw, TM), BF16)
        spec_a = pl.BlockSpec((fw, TM), lambda u: (blk_a, u))
        spec_actx = pl.BlockSpec((fw, TM), lambda u: (0, 0))
    else:
        nl_split = lay.NL
        spec_a = pl.BlockSpec((fw, TM), lambda u: (blk_a, jnp.minimum(u, lay.NL - 1)))
        spec_actx = pl.BlockSpec((fw, TM), lambda u: (0, jnp.maximum(u - lay.NL, 0)))
    in_specs = [
        pl.BlockSpec((TM, d), lambda u: (u, 0)),
        pl.BlockSpec((1, 6, d), lambda u: (lay.mrow(u), 0, 0)),
        spec_a,
        spec_actx,
        pl.BlockSpec((fw, TM), lambda u: (blk_b, u)),
        _full(wa), _full(wb), _full(g_ffn), _full(rwh), _full(rwl), _full(rbias), _full(ut),
    ]
    out_specs = [
        pl.BlockSpec((TM, d), lambda u: (u, 0)),
        pl.BlockSpec((TM, d), lambda u: (u, 0)),
        pl.BlockSpec((8, TM), lambda u: (0, u)),
        pl.BlockSpec((1, 32, 128), lambda u: (u, 0, 0)),
    ]
    out_shape = [
        jax.ShapeDtypeStruct((lay.TOKP, d), F32),
        jax.ShapeDtypeStruct((lay.TOKP, d), F32),
        jax.ShapeDtypeStruct((8, lay.TOKP), F32),
        jax.ShapeDtypeStruct((nt, 32, 128), F32),
    ]
    return pl.pallas_call(
        functools.partial(_post_kernel, nl_split=nl_split),
        grid=(nt,), in_specs=in_specs, out_specs=out_specs, out_shape=out_shape,
        compiler_params=_cparams(), name="post",
    )(z, mod, src_a, src_a_ctx, src_b, wa, wb, g_ffn, rwh, rwl, rbias, ut)


def _invperm_kernel(pos_ref, tok_ref):
    def clear(p, c):
        tok_ref[p] = 0
        return c
    lax.fori_loop(0, tok_ref.shape[0], clear, 0, unroll=16)

    def put(t, c):
        tok_ref[pos_ref[t]] = t
        return c
    lax.fori_loop(0, pos_ref.shape[0], put, 0, unroll=16)


def _invperm(pos, pmax):
    return pl.pallas_call(
        _invperm_kernel,
        in_specs=[pl.BlockSpec(memory_space=pltpu.SMEM)],
        out_specs=pl.BlockSpec(memory_space=pltpu.SMEM),
        out_shape=jax.ShapeDtypeStruct((pmax,), jnp.int32),
        name="moe_invperm",
    )(pos)


def _expert_kernel(ea_ref, eb_ref, nu_ref, nv_ref, tok_ref, tokn_ref, h2_ref, rw_ref, wga_ref, wgb_ref, wda_ref,
                   wdb_ref, ys_ref, xbuf, ybuf, gsem, ssem):
    i = pl.program_id(0)
    nu = nu_ref[0]
    ntok = ys_ref.shape[0] - TMM

    def gather_start(t_ref, slot):
        for r in range(TMM):
            pltpu.make_async_copy(h2_ref.at[pl.ds(t_ref[0, 0, r], 1)], xbuf.at[slot, pl.ds(r, 1)],
                                  gsem.at[slot]).start(priority=r % 2)

    def gather_wait(slot):
        pltpu.make_async_copy(h2_ref.at[pl.ds(0, TMM)], xbuf.at[slot], gsem.at[slot]).wait()

    def scatter_start(n):
        for r in range(TMM):
            dst = jnp.where(r < n, tok_ref[0, 0, r], ntok + r)
            pltpu.make_async_copy(ybuf.at[pl.ds(r, 1)], ys_ref.at[pl.ds(dst, 1)], ssem).start(priority=r % 2)

    def scatter_wait():
        pltpu.make_async_copy(ybuf, ys_ref.at[pl.ds(0, TMM)], ssem).wait()

    @pl.when(i < nu)
    def _():
        slot = i % 2

        @pl.when(i == 0)
        def _():
            gather_start(tok_ref, 0)
            ybuf[...] = jnp.zeros(ybuf.shape, F32)
            spare = pltpu.make_async_copy(ybuf, ys_ref.at[pl.ds(ntok, TMM)], ssem)
            spare.start()
            spare.wait()

        gather_wait(slot)
        gather_start(tokn_ref, 1 - slot)

        xb = xbuf[slot].astype(BF16)
        s = jax.nn.sigmoid(_dot(xb, rw_ref[...]))
        lane = lax.broadcasted_iota(jnp.int32, s.shape, 1)
        sa = jnp.sum(jnp.where(lane == ea_ref[i], s, 0.0), axis=1, keepdims=True)
        sb = jnp.sum(jnp.where(lane == eb_ref[i], s, 0.0), axis=1, keepdims=True)
        inv = 1.0 / (sa + sb)

        def hidden(w_ref, gate):
            gu = _dot(xb, w_ref[0])
            g = gu[:, :D_EXPERT]
            return (g * jax.nn.sigmoid(g) * gu[:, D_EXPERT:] * gate).astype(BF16)

        y = _dot(hidden(wga_ref, sa * inv), wda_ref[0]) + _dot(hidden(wgb_ref, sb * inv), wdb_ref[0])

        @pl.when(i > 0)
        def _():
            scatter_wait()

        ybuf[...] = y
        scatter_start(nv_ref[i])

        @pl.when(i == nu - 1)
        def _():
            scatter_wait()
            gather_wait(1 - slot)


def _experts(lay, tok3, h2, ea, eb, nused, nvalid, rw, wgu, wd):
    ntile = tok3.shape[0]
    d = wd.shape[2]
    grid_spec = pltpu.PrefetchScalarGridSpec(
        num_scalar_prefetch=4,
        grid=(ntile,),
        in_specs=[
            pl.BlockSpec((1, 1, TMM), lambda i, *_: (i, 0, 0), memory_space=pltpu.SMEM),
            pl.BlockSpec((1, 1, TMM), lambda i, *_: (jnp.minimum(i + 1, ntile - 1), 0, 0), memory_space=pltpu.SMEM),
            pl.BlockSpec(memory_space=pl.ANY),
            pl.BlockSpec(rw.shape, lambda i, *_: (0, 0)),
            pl.BlockSpec((1, d, 2 * D_EXPERT), lambda i, ea, eb, nu, nv: (ea[i], 0, 0)),
            pl.BlockSpec((1, d, 2 * D_EXPERT), lambda i, ea, eb, nu, nv: (eb[i], 0, 0)),
            pl.BlockSpec((1, D_EXPERT, d), lambda i, ea, eb, nu, nv: (ea[i], 0, 0)),
            pl.BlockSpec((1, D_EXPERT, d), lambda i, ea, eb, nu, nv: (eb[i], 0, 0)),
        ],
        out_specs=pl.BlockSpec(memory_space=pl.ANY),
        scratch_shapes=[
            pltpu.VMEM((2, TMM, d), F32),
            pltpu.VMEM((TMM, d), F32),
            pltpu.SemaphoreType.DMA((2,)),
            pltpu.SemaphoreType.DMA(()),
        ],
    )
    return pl.pallas_call(
        _expert_kernel, grid_spec=grid_spec,
        out_shape=jax.ShapeDtypeStruct((lay.TOKP + TMM, d), F32),
        compiler_params=_cparams(), name="moe_experts",
    )(ea, eb, nused, nvalid, tok3, tok3, h2, rw, wgu, wgu, wd, wd)


def _final_kernel(z_ref, y_ref, mod_ref, gf_ref, o_ref):
    x2 = z_ref[...] + mod_ref[0, 5:6, :] * y_ref[...]
    ms = jnp.mean(x2 * x2, axis=-1, keepdims=True)
    o_ref[...] = x2 * lax.rsqrt(ms + EPS) * gf_ref[...]


def _final(lay, z, ys, mod, g_final):
    d = z.shape[1]
    return pl.pallas_call(
        _final_kernel,
        grid=(lay.NL,),
        in_specs=[
            pl.BlockSpec((TM, d), lambda u: (u, 0)),
            pl.BlockSpec((TM, d), lambda u: (u, 0)),
            pl.BlockSpec((1, 6, d), lambda u: (lay.mrow(u), 0, 0)),
            _full(g_final),
        ],
        out_specs=pl.BlockSpec((TM, d), lambda u: (u, 0)),
        out_shape=jax.ShapeDtypeStruct((lay.NL * TM, d), F32),
        compiler_params=_cparams(), name="final_norm",
    )(z, ys, mod, g_final)


def _plan(info, cnt, nt, ntile_max):
    cls = info[0].astype(jnp.int32).reshape(nt, TM)
    rank = info[1].astype(jnp.int32).reshape(nt, TM)
    cnt = cnt[:, :N_CLASSES, 0].astype(jnp.int32)
    tot = jnp.sum(cnt, axis=0)
    ntile_c = (tot + TMM - 1) // TMM
    tile_end = jnp.cumsum(ntile_c)
    class_off = (tile_end - ntile_c) * TMM
    tile_off = jnp.cumsum(cnt, axis=0) - cnt
    base = class_off[None, :] + tile_off
    onehot = cls[:, :, None] == jnp.arange(N_CLASSES, dtype=jnp.int32)[None, None, :]
    pos = jnp.sum(jnp.where(onehot, base[:, None, :], 0), axis=-1) + rank
    nused = tile_end[-1]
    ti = jnp.minimum(jnp.arange(ntile_max, dtype=jnp.int32), nused - 1)
    tcls = jnp.sum((ti[:, None] >= tile_end[None, :]).astype(jnp.int32), axis=1)
    onec = tcls[:, None] == jnp.arange(N_CLASSES, dtype=jnp.int32)[None, :]
    class_end = jnp.sum(jnp.where(onec, (class_off + tot)[None, :], 0), axis=1)
    nvalid = jnp.clip(class_end - ti * TMM, 0, TMM)
    grp, pair = tcls // 6, tcls % 6
    lo = jnp.asarray(_PAIR_LO, jnp.int32)
    hi = jnp.asarray(_PAIR_HI, jnp.int32)
    onep = pair[:, None] == jnp.arange(6, dtype=jnp.int32)[None, :]
    ea = 4 * grp + jnp.sum(jnp.where(onep, lo[None, :], 0), axis=1)
    eb = 4 * grp + jnp.sum(jnp.where(onep, hi[None, :], 0), axis=1)
    return pos.reshape(nt * TM), ea, eb, nused.reshape(1), nvalid


def _dft_mats(n, scale):
    r = 64 if (n % 64 == 0 and n > 64) else 1
    k = lax.broadcasted_iota(jnp.int32, (1, n), 1)

    def cs(rows, step):
        j = lax.broadcasted_iota(jnp.int32, (rows, 1), 0) * step
        ang = ((j * k) % n).astype(F32) * (2.0 * math.pi / n)
        return jnp.cos(ang), jnp.sin(ang)

    ca, sa = cs(n // r, r)
    if r == 1:
        c, s = ca, sa
    else:
        cb, sb = cs(r, 1)
        c = (ca[:, None, :] * cb[None, :, :] - sa[:, None, :] * sb[None, :, :]).reshape(n, n)
        s = (sa[:, None, :] * cb[None, :, :] + ca[:, None, :] * sb[None, :, :]).reshape(n, n)
    return (c * scale).astype(BF16), (s * scale).astype(BF16)


def _rope_angles(S, dim):
    rows = S // GRID_W
    row_id = jnp.repeat(jnp.arange(rows, dtype=F32), GRID_W)
    col_id = jnp.tile(jnp.arange(GRID_W, dtype=F32), rows)
    n_freq = dim // 4
    inv = ROPE_BASE ** (-jnp.arange(n_freq, dtype=F32) / n_freq)
    return jnp.concatenate([row_id[:, None] * inv, col_id[:, None] * inv], axis=-1)


def _tables(lay):
    S, C = lay.S, lay.C
    tabs = {}
    cc, sc = _dft_mats(FOURIER_GROUP_DIM, FOURIER_GROUP_DIM ** -0.5)
    tabs["cs"] = jnp.concatenate([cc, sc], axis=0)
    tabs["cn"], tabs["sn"] = _dft_mats(S, S ** -0.5)
    tabs["cctx"], tabs["sctx"] = _dft_mats(C, C ** -0.5)
    for name, dim in (("mla", MLA_ROPE_DIM), ("gqa", GQA_HEAD_DIM)):
        ang = _rope_angles(S, dim)
        cos = jnp.concatenate([jnp.ones((TM, dim // 2), F32), jnp.cos(ang)], axis=0)
        sin = jnp.concatenate([jnp.zeros((TM, dim // 2), F32), jnp.sin(ang)], axis=0)
        tabs[name + "_ct"] = cos.T
        tabs[name + "_st"] = sin.T
        if name == "mla":
            pad = jnp.zeros((TM + S, 128 - dim), F32)
            tabs["mla_t1"] = jnp.concatenate([cos, cos, pad], axis=1)
            tabs["mla_t2"] = jnp.concatenate([sin, -sin, pad], axis=1)
    tabs["ut"] = (lax.broadcasted_iota(jnp.int32, (TM, TM), 0)
                  < lax.broadcasted_iota(jnp.int32, (TM, TM), 1)).astype(BF16)
    return tabs


def _col(g, rows):
    return jnp.broadcast_to(g.astype(F32)[:, None], (rows, TM))


def _even_weights(w_in, q_norm_g, w_qb, kv_norm_g, w_kvb, w_out):
    d = w_in.shape[0]
    o1 = FOURIER_WIDTH
    o2 = o1 + MLA_Q_RANK
    o3 = o2 + MLA_KV_RANK
    bf = lambda a: a.astype(BF16)
    w = {}
    w["wf"] = bf(w_in[:, :o1].T)
    w["wqa"] = bf(w_in[:, o1:o2].T)
    w["wkvat"] = bf(w_in[:, o2:o3].T)
    w["wkva"] = bf(w_in[:, o2:o3])
    w["wpr"] = bf(jnp.concatenate([w_in[:, o3:], jnp.zeros((d, 128 - MLA_ROPE_DIM), F32)], axis=1))
    w["gq"] = _col(q_norm_g, MLA_Q_RANK)
    w["wqb"] = bf(w_qb.T)
    w["gkvc"] = _col(kv_norm_g, MLA_KV_RANK)
    w["gkvr"] = kv_norm_g.astype(F32)[None, :]
    kvb = w_kvb.reshape(MLA_KV_RANK, MLA_HEADS, MLA_NOPE_DIM + MLA_V_DIM)
    w["wv"] = bf(kvb[:, :, MLA_NOPE_DIM:].reshape(MLA_KV_RANK, MLA_HEADS * MLA_V_DIM).T)
    w1 = jnp.concatenate([kvb[:, :, :MLA_NOPE_DIM],
                          jnp.zeros((MLA_KV_RANK, MLA_HEADS, DK - MLA_NOPE_DIM), F32)], axis=2)
    w1 = w1.reshape(MLA_KV_RANK, MLA_HEADS * DK)
    eye = jnp.eye(MLA_ROPE_DIM, dtype=F32)
    place = jnp.zeros((128, DK), F32).at[:MLA_ROPE_DIM, MLA_NOPE_DIM:MLA_QK_DIM].set(eye)
    swap = jnp.roll(eye, MLA_ROPE_DIM // 2, axis=1)
    place_s = jnp.zeros((128, DK), F32).at[:MLA_ROPE_DIM, MLA_NOPE_DIM:MLA_QK_DIM].set(swap)
    w["wk"] = bf(jnp.concatenate([w1, jnp.tile(place, (1, MLA_HEADS)), jnp.tile(place_s, (1, MLA_HEADS))], axis=0))
    w["wout_a"] = bf(w_out[:o1])
    w["wout_b"] = bf(w_out[o1:])
    return w


def _odd_weights(w_qkv, q_norm_g, k_norm_g, w_out):
    bf = lambda a: a.astype(BF16)
    qw = GQA_HEADS * GQA_HEAD_DIM
    kw = GQA_KV_HEADS * GQA_HEAD_DIM
    w = {}
    w["wq"] = bf(w_qkv[:, :qw].T)
    w["wk"] = bf(w_qkv[:, qw:qw + kw].T)
    w["wv"] = bf(w_qkv[:, qw + kw:].T)
    w["gq"] = _col(q_norm_g, GQA_HEAD_DIM)
    w["gk"] = _col(k_norm_g, GQA_HEAD_DIM)
    w["wout_a"] = bf(w_out[:FOURIER_WIDTH])
    w["wout_b"] = bf(w_out[FOURIER_WIDTH:])
    return w


def kernel(x, c, ctx, c_ctx, ada_w, ada_b, norm_mix_g, norm_ffn_g, ev_w_in, ev_q_norm_g, ev_w_qb, ev_kv_norm_g,
           ev_w_kvb, ev_w_out, od_w_qkv, od_q_norm_g, od_k_norm_g, od_w_out, router_w, router_b, exp_w_gate,
           exp_w_up, exp_w_down, final_norm_g):
    B, S, d = x.shape
    C = ctx.shape[1]
    depth = ada_w.shape[0]
    lay = _Layout(B, S, C)
    assert B + 1 <= 16
    tabs = _tables(lay)

    cond = jnp.concatenate([c, c_ctx[None, :], jnp.zeros((16 - B - 1, d), F32)], axis=0)
    mods = _modulation(cond, ada_w, ada_b).reshape(depth, 16, 6, d)

    z = jnp.concatenate([x.reshape(B * S, d), ctx.reshape(B * C, d)], axis=0)

    rwt = router_w.T.astype(F32)
    rwh = rwt.astype(BF16)
    rwl = (rwt - rwh.astype(F32)).astype(BF16)
    rbias = jnp.broadcast_to(router_b.astype(F32)[:, None], (N_EXPERTS, TM))
    rw_nat = jnp.concatenate([router_w, jnp.zeros((d, 128 - N_EXPERTS), F32)], axis=1).astype(BF16)
    wgu_all = jnp.concatenate([exp_w_gate, exp_w_up], axis=-1).astype(BF16)
    wd_all = exp_w_down.astype(BF16)
    ntile_max = lay.NT * TM // TMM + N_CLASSES

    zin = (z,)
    for i in range(depth):
        mod = mods[i]
        g_mix = norm_mix_g[i][None, :]
        g_ffn = norm_ffn_g[i][None, :]
        j = i // 2
        if i % 2 == 0:
            w = _even_weights(ev_w_in[j], ev_q_norm_g[j], ev_w_qb[j], ev_kv_norm_g[j], ev_w_kvb[j], ev_w_out[j])
            abt, qt, k, vt, *znew = _pre_even(lay, zin, mod, g_mix, w, tabs)
            ot = _attention(lay, qt, k, vt, n_kv=MLA_HEADS, grp=1, dq=DK, kb=4)
            src_a, src_a_ctx = _seq_dft(lay, abt, tabs["cn"], tabs["sn"], tabs["cctx"], tabs["sctx"])
            src_b, blk_a, blk_b = ot, 0, 0
        else:
            w = _odd_weights(od_w_qkv[j], od_q_norm_g[j], od_k_norm_g[j], od_w_out[j])
            qt, k, vt, *znew = _pre_odd(lay, zin, mod, g_mix, w, tabs)
            ot = _attention(lay, qt, k, vt, n_kv=GQA_KV_HEADS, grp=GQA_GROUP, dq=GQA_HEAD_DIM, kb=2)
            src_a, src_a_ctx, src_b, blk_a, blk_b = ot, None, ot, 0, 1
        z = znew[0] if znew else zin[0]
        z1, h2, info, cnt = _post(lay, z, mod, src_a, src_a_ctx, src_b, blk_a, blk_b, w["wout_a"], w["wout_b"],
                                  g_ffn, rwh, rwl, rbias, tabs["ut"])
        pos, ea, eb, nused, nvalid = _plan(info, cnt, lay.NT, ntile_max)
        tok3 = _invperm(pos, ntile_max * TMM).reshape(ntile_max, 1, TMM)
        ys = _experts(lay, tok3, h2, ea, eb, nused, nvalid, rw_nat, wgu_all[i], wd_all[i])
        zin = (z1, ys, mod)
    out = _final(lay, zin[0], zin[1], zin[2], final_norm_g[None, :])
    return out.reshape(B, S, d)
```

```python
import functools
import math

import jax
import jax.numpy as jnp
from jax import lax
from jax.experimental import pallas as pl
from jax.experimental.pallas import tpu as pltpu

F32 = jnp.float32
BF16 = jnp.bfloat16

TM = 512
TQ = 256
TKC = 512
TMM = 256
TN_DFT = 512
EPS = 1e-6
LOG2E = 1.4426950408889634
ROPE_BASE = 10000.0
GRID_W = 64

FOURIER_GROUPS = 4
FOURIER_GROUP_DIM = 128
FOURIER_WIDTH = 512
MLA_HEADS = 8
MLA_NOPE_DIM = 64
MLA_ROPE_DIM = 32
MLA_V_DIM = 64
MLA_QK_DIM = 96
MLA_Q_RANK = 256
MLA_KV_RANK = 128
MLA_SCALE = MLA_QK_DIM ** -0.5
GQA_HEADS = 16
GQA_KV_HEADS = 4
GQA_HEAD_DIM = 64
GQA_GROUP = 4
GQA_SCALE = GQA_HEAD_DIM ** -0.5
N_EXPERTS = 16
N_GROUPS = 4
N_CLASSES = 24
D_EXPERT = 512
DK = 128
DV = 64
VMEM_LIMIT = 56 * 1024 * 1024

_PAIR_LO = (0, 0, 0, 1, 1, 2)
_PAIR_HI = (1, 2, 3, 2, 3, 3)


def _nt(a, b):
    return lax.dot_general(a, b, (((1,), (1,)), ((), ())), preferred_element_type=F32)


def _tn(a, b):
    return lax.dot_general(a, b, (((0,), (0,)), ((), ())), preferred_element_type=F32)


def _dot(a, b):
    return jnp.dot(a, b, preferred_element_type=F32)


class _Layout:
    def __init__(self, B, S, C):
        assert C == TQ and TM % C == 0 and (B * C) % TM == 0 and S % TM == 0 and S % TN_DFT == 0 and S % TKC == 0
        self.B, self.S, self.C = B, S, C
        self.SB = S // TM
        self.NL = B * self.SB
        self.NT = self.NL + B * C // TM
        self.TOKP = self.NT * TM
        self.CB = B * S // C

    def mrow(self, u):
        return jnp.where(u < self.NL, u // self.SB, self.B)

    def ropeblk(self, u):
        return jnp.where(u < self.NL, 1 + u % self.SB, 0)


def _cparams():
    return pltpu.CompilerParams(vmem_limit_bytes=VMEM_LIMIT)


def _full(a):
    return pl.BlockSpec(a.shape, lambda *_: (0,) * a.ndim)


def _mod_kernel(s_ref, w_ref, b_ref, o_ref):
    s = s_ref[...]
    a = s * jax.nn.sigmoid(s)
    ah = a.astype(BF16)
    al = (a - ah.astype(F32)).astype(BF16)
    w = w_ref[0]
    wh = w.astype(BF16)
    wl = (w - wh.astype(F32)).astype(BF16)
    o_ref[0] = _dot(ah, wh) + _dot(ah, wl) + _dot(al, wh) + b_ref[0]


def _modulation(cond, ada_w, ada_b):
    depth, d, w6 = ada_w.shape
    tn = 768
    return pl.pallas_call(
        _mod_kernel,
        grid=(depth, w6 // tn),
        in_specs=[
            pl.BlockSpec((16, d), lambda i, j: (0, 0)),
            pl.BlockSpec((1, d, tn), lambda i, j: (i, 0, j)),
            pl.BlockSpec((1, 1, tn), lambda i, j: (i, 0, j)),
        ],
        out_specs=pl.BlockSpec((1, 16, tn), lambda i, j: (i, 0, j)),
        out_shape=jax.ShapeDtypeStruct((depth, 16, w6), F32),
        compiler_params=_cparams(),
        name="modulation",
    )(cond, ada_w, ada_b.reshape(depth, 1, w6))


def _norm_mod(z, g, sc, sh):
    ms = jnp.mean(z * z, axis=-1, keepdims=True)
    return (z * lax.rsqrt(ms + EPS) * g) * (1.0 + sc) + sh


def _moe_rows(ys_ref, pos_ref, posn_ref, ybuf, sem, nt):
    u = pl.program_id(0)
    slot = u % 2

    def start(p_ref, dst_slot):
        def body(j, c):
            for par in range(2):
                r = 2 * j + par
                pltpu.make_async_copy(ys_ref.at[pl.ds(p_ref[0, 0, r], 1)], ybuf.at[dst_slot, pl.ds(r, 1)],
                                      sem.at[dst_slot]).start(priority=par)
            return c
        lax.fori_loop(0, TM // 2, body, 0, unroll=4)

    @pl.when(u == 0)
    def _():
        start(pos_ref, 0)

    pltpu.make_async_copy(ys_ref.at[pl.ds(0, TM)], ybuf.at[slot], sem.at[slot]).wait()

    @pl.when(u + 1 < nt)
    def _():
        start(posn_ref, 1 - slot)

    return ybuf[slot]


def _zin_specs(lay, zin, d, nt):
    row = pl.BlockSpec((TM, d), lambda u: (u, 0))
    if len(zin) == 1:
        return [row], list(zin), [], [], []
    z1, ys, pos3, modp = zin
    specs = [
        row,
        pl.BlockSpec(memory_space=pl.ANY),
        pl.BlockSpec((1, 1, TM), lambda u: (u, 0, 0), memory_space=pltpu.SMEM),
        pl.BlockSpec((1, 1, TM), lambda u: (jnp.minimum(u + 1, nt - 1), 0, 0), memory_space=pltpu.SMEM),
        pl.BlockSpec((1, 6, d), lambda u: (lay.mrow(u), 0, 0)),
    ]
    scratch = [pltpu.VMEM((2, TM, d), F32), pltpu.SemaphoreType.DMA((2,))]
    return specs, [z1, ys, pos3, pos3, modp], [row], [jax.ShapeDtypeStruct((lay.TOKP, d), F32)], scratch


def _zin_value(zin_refs, tail_refs, nt):
    if len(zin_refs) == 1:
        return zin_refs[0][...]
    z_ref, ys_ref, pos_ref, posn_ref, modp_ref = zin_refs
    zo_ref, ybuf, sem = tail_refs
    z = z_ref[...] + modp_ref[0, 5:6, :] * _moe_rows(ys_ref, pos_ref, posn_ref, ybuf, sem, nt)
    zo_ref[...] = z
    return z


def _pre_even_kernel(*refs, n_zin, nt):
    (mod_ref, g_ref, wf_ref, wqa_ref, wkvat_ref, wkva_ref, wpr_ref, gq_ref, wqb_ref, gkvc_ref, gkvr_ref, wv_ref,
     wk_ref, cs_ref, ct_ref, st_ref, t1_ref, t2_ref, ab_ref, q_ref, k_ref, v_ref) = refs[n_zin:n_zin + 22]
    z = _zin_value(refs[:n_zin], refs[n_zin + 22:], nt)
    h = _norm_mod(z, g_ref[...], mod_ref[0, 1:2, :], mod_ref[0, 0:1, :])
    hb = h.astype(BF16)

    ft = _nt(wf_ref[...], hb).astype(BF16)
    gd = FOURIER_GROUP_DIM
    for g in range(FOURIER_GROUPS):
        ab = _dot(cs_ref[...], ft[g * gd:(g + 1) * gd, :])
        ab_ref[g * gd:(g + 1) * gd, :] = ab[:gd].astype(BF16)
        ab_ref[FOURIER_WIDTH + g * gd:FOURIER_WIDTH + (g + 1) * gd, :] = ab[gd:].astype(BF16)

    pq = _nt(wqa_ref[...], hb)
    rs = lax.rsqrt(jnp.mean(pq * pq, axis=0, keepdims=True) + EPS)
    qn = (pq * rs * gq_ref[...]).astype(BF16)
    qt = _dot(wqb_ref[...], qn) * (MLA_SCALE * LOG2E)
    cos = ct_ref[...]
    sin = st_ref[...]
    hr = MLA_ROPE_DIM // 2
    for hd in range(MLA_HEADS):
        o = hd * MLA_QK_DIM
        r = hd * DK
        x1 = qt[o + MLA_NOPE_DIM:o + MLA_NOPE_DIM + hr, :]
        x2 = qt[o + MLA_NOPE_DIM + hr:o + MLA_QK_DIM, :]
        q_ref[r:r + MLA_NOPE_DIM, :] = qt[o:o + MLA_NOPE_DIM, :].astype(BF16)
        q_ref[r + MLA_NOPE_DIM:r + MLA_NOPE_DIM + hr, :] = (x1 * cos - x2 * sin).astype(BF16)
        q_ref[r + MLA_NOPE_DIM + hr:r + MLA_QK_DIM, :] = (x1 * sin + x2 * cos).astype(BF16)
        q_ref[r + MLA_QK_DIM:r + DK, :] = jnp.zeros((DK - MLA_QK_DIM, TM), BF16)

    pkvt = _nt(wkvat_ref[...], hb)
    rst = lax.rsqrt(jnp.mean(pkvt * pkvt, axis=0, keepdims=True) + EPS)
    kvnt = (pkvt * rst * gkvc_ref[...]).astype(BF16)
    v_ref[...] = _dot(wv_ref[...], kvnt).astype(BF16)

    pkv = _dot(hb, wkva_ref[...])
    rsn = lax.rsqrt(jnp.mean(pkv * pkv, axis=-1, keepdims=True) + EPS)
    kvn = (pkv * rsn * gkvr_ref[...]).astype(BF16)
    pr = _dot(hb, wpr_ref[...])
    lhs = jnp.concatenate([kvn, (pr * t1_ref[...]).astype(BF16), (pr * t2_ref[...]).astype(BF16)], axis=1)
    kf = _dot(lhs, wk_ref[...])
    for hd in range(MLA_HEADS):
        k_ref[hd] = kf[:, hd * DK:(hd + 1) * DK].astype(BF16)


def _pre_even(lay, zin, mod, g_mix, w, tabs):
    d = mod.shape[-1]
    tokp = lay.TOKP
    col = lambda rows: pl.BlockSpec((rows, TM), lambda u: (0, u))
    hr = MLA_ROPE_DIM // 2
    zin_specs, zin_ops, zout_specs, zout_shape, scratch = _zin_specs(lay, zin, d, lay.NT)
    in_specs = zin_specs + [
        pl.BlockSpec((1, 6, d), lambda u: (lay.mrow(u), 0, 0)),
        _full(g_mix),
        _full(w["wf"]), _full(w["wqa"]), _full(w["wkvat"]), _full(w["wkva"]), _full(w["wpr"]),
        _full(w["gq"]), _full(w["wqb"]), _full(w["gkvc"]), _full(w["gkvr"]), _full(w["wv"]), _full(w["wk"]),
        _full(tabs["cs"]),
        pl.BlockSpec((hr, TM), lambda u: (0, lay.ropeblk(u))),
        pl.BlockSpec((hr, TM), lambda u: (0, lay.ropeblk(u))),
        pl.BlockSpec((TM, 128), lambda u: (lay.ropeblk(u), 0)),
        pl.BlockSpec((TM, 128), lambda u: (lay.ropeblk(u), 0)),
    ]
    out_specs = [
        col(2 * FOURIER_WIDTH),
        col(MLA_HEADS * DK),
        pl.BlockSpec((MLA_HEADS, TM, DK), lambda u: (0, u, 0)),
        col(MLA_HEADS * DV),
    ]
    out_shape = [
        jax.ShapeDtypeStruct((2 * FOURIER_WIDTH, tokp), BF16),
        jax.ShapeDtypeStruct((MLA_HEADS * DK, tokp), BF16),
        jax.ShapeDtypeStruct((MLA_HEADS, tokp, DK), BF16),
        jax.ShapeDtypeStruct((MLA_HEADS * DV, tokp), BF16),
    ]
    return pl.pallas_call(
        functools.partial(_pre_even_kernel, n_zin=len(zin_ops), nt=lay.NT),
        grid=(lay.NT,), in_specs=in_specs, out_specs=out_specs + zout_specs, out_shape=out_shape + zout_shape,
        scratch_shapes=scratch, compiler_params=_cparams(), name="pre_even",
    )(*zin_ops, mod, g_mix, w["wf"], w["wqa"], w["wkvat"], w["wkva"], w["wpr"], w["gq"], w["wqb"], w["gkvc"],
      w["gkvr"], w["wv"], w["wk"], tabs["cs"], tabs["mla_ct"], tabs["mla_st"], tabs["mla_t1"], tabs["mla_t2"])


def _pre_odd_kernel(*refs, n_zin, nt):
    (mod_ref, g_ref, wq_ref, wk_ref, wv_ref, gq_ref, gk_ref, ct_ref, st_ref,
     q_ref, k_ref, v_ref) = refs[n_zin:n_zin + 12]
    z = _zin_value(refs[:n_zin], refs[n_zin + 12:], nt)
    h = _norm_mod(z, g_ref[...], mod_ref[0, 1:2, :], mod_ref[0, 0:1, :])
    hb = h.astype(BF16)
    cos = ct_ref[...]
    sin = st_ref[...]
    hd2 = GQA_HEAD_DIM // 2

    def norm_rope(xt, gain, scale):
        rs = lax.rsqrt(jnp.mean(xt * xt, axis=0, keepdims=True) + EPS)
        xn = xt * rs * gain
        x1 = xn[:hd2]
        x2 = xn[hd2:]
        return (x1 * cos - x2 * sin) * scale, (x1 * sin + x2 * cos) * scale

    qt = _nt(wq_ref[...], hb)
    for hd in range(GQA_HEADS):
        o = hd * GQA_HEAD_DIM
        r1, r2 = norm_rope(qt[o:o + GQA_HEAD_DIM], gq_ref[...], GQA_SCALE * LOG2E)
        q_ref[o:o + hd2, :] = r1.astype(BF16)
        q_ref[o + hd2:o + GQA_HEAD_DIM, :] = r2.astype(BF16)

    kt = _nt(wk_ref[...], hb)
    zero = jnp.zeros((DK - GQA_HEAD_DIM, TM), F32)
    for hd in range(GQA_KV_HEADS):
        o = hd * GQA_HEAD_DIM
        r1, r2 = norm_rope(kt[o:o + GQA_HEAD_DIM], gk_ref[...], 1.0)
        khead = jnp.concatenate([r1, r2, zero], axis=0)
        k_ref[hd] = khead.T.astype(BF16)

    v_ref[...] = _nt(wv_ref[...], hb).astype(BF16)


def _pre_odd(lay, zin, mod, g_mix, w, tabs):
    d = mod.shape[-1]
    tokp = lay.TOKP
    col = lambda rows: pl.BlockSpec((rows, TM), lambda u: (0, u))
    hd2 = GQA_HEAD_DIM // 2
    zin_specs, zin_ops, zout_specs, zout_shape, scratch = _zin_specs(lay, zin, d, lay.NT)
    in_specs = zin_specs + [
        pl.BlockSpec((1, 6, d), lambda u: (lay.mrow(u), 0, 0)),
        _full(g_mix), _full(w["wq"]), _full(w["wk"]), _full(w["wv"]), _full(w["gq"]), _full(w["gk"]),
        pl.BlockSpec((hd2, TM), lambda u: (0, lay.ropeblk(u))),
        pl.BlockSpec((hd2, TM), lambda u: (0, lay.ropeblk(u))),
    ]
    out_specs = [
        col(GQA_HEADS * GQA_HEAD_DIM),
        pl.BlockSpec((GQA_KV_HEADS, TM, DK), lambda u: (0, u, 0)),
        col(GQA_KV_HEADS * DV),
    ]
    out_shape = [
        jax.ShapeDtypeStruct((GQA_HEADS * GQA_HEAD_DIM, tokp), BF16),
        jax.ShapeDtypeStruct((GQA_KV_HEADS, tokp, DK), BF16),
        jax.ShapeDtypeStruct((GQA_KV_HEADS * DV, tokp), BF16),
    ]
    return pl.pallas_call(
        functools.partial(_pre_odd_kernel, n_zin=len(zin_ops), nt=lay.NT),
        grid=(lay.NT,), in_specs=in_specs, out_specs=out_specs + zout_specs, out_shape=out_shape + zout_shape,
        scratch_shapes=scratch, compiler_params=_cparams(), name="pre_odd",
    )(*zin_ops, mod, g_mix, w["wq"], w["wk"], w["wv"], w["gq"], w["gk"], tabs["gqa_ct"], tabs["gqa_st"])


def _attn_kernel(q_ref, kc_ref, kl_ref, vc_ref, vl_ref, o_ref, s0_ref, s1_ref, *, nq, kb, grp, dq, S, C):
    qi = pl.program_id(2)
    hb = kb * grp
    sbufs = (s0_ref, s1_ref)
    nchunk = S // TKC

    def scores(hh, buf, with_latent):
        kv = hh // grp
        q = q_ref[hh * dq:(hh + 1) * dq, :]
        sc = _dot(kc_ref[kv, :, :dq], q)
        buf[0:C, :] = sc
        m = jnp.max(sc, axis=0, keepdims=True)
        if with_latent:
            for c in range(nchunk):
                s = _dot(kl_ref[kv, c * TKC:(c + 1) * TKC, :dq], q)
                buf[C + c * TKC:C + (c + 1) * TKC, :] = s
                m = jnp.maximum(m, jnp.max(s, axis=0, keepdims=True))
        return m

    def values(hh, buf, m, with_latent):
        kv = hh // grp
        vrows = slice(kv * DV, (kv + 1) * DV)
        p = jnp.exp2(buf[0:C, :] - m)
        l = jnp.sum(p, axis=0, keepdims=True)
        o = _dot(vc_ref[vrows, :], p.astype(BF16))
        if with_latent:
            for c in range(nchunk):
                p = jnp.exp2(buf[C + c * TKC:C + (c + 1) * TKC, :] - m)
                l = l + jnp.sum(p, axis=0, keepdims=True)
                o = o + _dot(vl_ref[vrows, c * TKC:(c + 1) * TKC], p.astype(BF16))
        o_ref[hh * DV:(hh + 1) * DV, :] = (o * (1.0 / l)).astype(o_ref.dtype)

    def run(with_latent):
        m = scores(0, sbufs[0], with_latent)
        for hh in range(hb):
            m_next = scores(hh + 1, sbufs[(hh + 1) % 2], with_latent) if hh + 1 < hb else None
            values(hh, sbufs[hh % 2], m, with_latent)
            m = m_next

    @pl.when(qi < nq)
    def _():
        run(True)

    @pl.when(qi == nq)
    def _():
        run(False)


def _attention(lay, qt, k, vt, *, n_kv, grp, dq, kb):
    B, S, C = lay.B, lay.S, lay.C
    hb = kb * grp
    nq = S // TQ

    def qcol(b, qi):
        return jnp.where(qi == nq, lay.CB + b, b * nq + qi)

    in_specs = [
        pl.BlockSpec((hb * dq, TQ), lambda b, h, qi: (h, qcol(b, qi))),
        pl.BlockSpec((kb, C, DK), lambda b, h, qi: (h, lay.CB + b, 0)),
        pl.BlockSpec((kb, S, DK), lambda b, h, qi: (h, b, 0)),
        pl.BlockSpec((kb * DV, C), lambda b, h, qi: (h, lay.CB + b)),
        pl.BlockSpec((kb * DV, S), lambda b, h, qi: (h, b)),
    ]
    return pl.pallas_call(
        functools.partial(_attn_kernel, nq=nq, kb=kb, grp=grp, dq=dq, S=S, C=C),
        grid=(B, n_kv // kb, nq + 1),
        in_specs=in_specs,
        out_specs=pl.BlockSpec((hb * DV, TQ), lambda b, h, qi: (h, qcol(b, qi))),
        out_shape=jax.ShapeDtypeStruct((n_kv * grp * DV, lay.TOKP), BF16),
        scratch_shapes=[pltpu.VMEM((C + S, TQ), F32), pltpu.VMEM((C + S, TQ), F32)],
        compiler_params=_cparams(),
        name="attention",
    )(qt, k, k, vt, vt)


def _dft_kernel(a_ref, b_ref, c_ref, s_ref, o_ref):
    o_ref[...] = (_dot(a_ref[...], c_ref[...]) - _dot(b_ref[...], s_ref[...])).astype(o_ref.dtype)


def _seq_dft(lay, abt, cn, sn, cc, sc):
    B, S, C = lay.B, lay.S, lay.C
    fw = FOURIER_WIDTH
    nj = S // TN_DFT
    ylat = pl.pallas_call(
        _dft_kernel,
        grid=(nj, B),
        in_specs=[
            pl.BlockSpec((fw, S), lambda j, b: (0, b)),
            pl.BlockSpec((fw, S), lambda j, b: (1, b)),
            pl.BlockSpec((S, TN_DFT), lambda j, b: (0, j)),
            pl.BlockSpec((S, TN_DFT), lambda j, b: (0, j)),
        ],
        out_specs=pl.BlockSpec((fw, TN_DFT), lambda j, b: (0, b * nj + j)),
        out_shape=jax.ShapeDtypeStruct((fw, B * S), BF16),
        compiler_params=_cparams(),
        name="seq_dft",
    )(abt, abt, cn, sn)
    yctx = pl.pallas_call(
        _dft_kernel,
        grid=(B,),
        in_specs=[
            pl.BlockSpec((fw, C), lambda b: (0, lay.CB + b)),
            pl.BlockSpec((fw, C), lambda b: (1, lay.CB + b)),
            pl.BlockSpec((C, C), lambda b: (0, 0)),
            pl.BlockSpec((C, C), lambda b: (0, 0)),
        ],
        out_specs=pl.BlockSpec((fw, C), lambda b: (0, b)),
        out_shape=jax.ShapeDtypeStruct((fw, B * C), BF16),
        compiler_params=_cparams(),
        name="ctx_dft",
    )(abt, abt, cc, sc)
    return ylat, yctx


def _route_rows(lg, rbias):
    s = jax.nn.sigmoid(lg)
    sel = s + rbias
    rows = [sel[e:e + 1, :] for e in range(N_EXPERTS)]
    best = None
    bg = None
    for g in range(N_GROUPS):
        v = rows[4 * g:4 * g + 4]
        sc = None
        for i in range(4):
            for j in range(i + 1, 4):
                ps = v[i] + v[j]
                sc = ps if sc is None else jnp.maximum(sc, ps)
        if best is None:
            best, bg = sc, jnp.zeros_like(sc)
        else:
            upd = sc > best
            bg = jnp.where(upd, float(g), bg)
            best = jnp.where(upd, sc, best)
    v = []
    for i in range(4):
        acc = rows[i]
        for g in range(1, N_GROUPS):
            acc = jnp.where(bg == float(g), rows[4 * g + i], acc)
        v.append(acc)
    i0 = jnp.zeros_like(bg)
    b0 = v[0]
    for i in range(1, 4):
        upd = v[i] > b0
        i0 = jnp.where(upd, float(i), i0)
        b0 = jnp.where(upd, v[i], b0)
    w = [jnp.where(i0 == float(i), -jnp.inf, v[i]) for i in range(4)]
    i1 = jnp.zeros_like(bg)
    b1 = w[0]
    for i in range(1, 4):
        upd = w[i] > b1
        i1 = jnp.where(upd, float(i), i1)
        b1 = jnp.where(upd, w[i], b1)
    lo = jnp.minimum(i0, i1)
    hi = jnp.maximum(i0, i1)
    pbase = jnp.where(lo == 0.0, 0.0, jnp.where(lo == 1.0, 3.0, 5.0))
    return bg * 6.0 + pbase + hi - lo - 1.0


def _post_kernel(z_ref, mod_ref, a_ref, actx_ref, b_ref, wa_ref, wb_ref, g_ref, rwh_ref, rwl_ref, rb_ref, ut_ref,
                 zo_ref, h2_ref, info_ref, cnt_ref, *, nl_split):
    a = a_ref[...]
    if nl_split is not None:
        a = jnp.where(pl.program_id(0) < nl_split, a, actx_ref[...])
    attn = _tn(a, wa_ref[...]) + _tn(b_ref[...], wb_ref[...])
    x1 = z_ref[...] + mod_ref[0, 2:3, :] * attn
    zo_ref[...] = x1
    h = _norm_mod(x1, g_ref[...], mod_ref[0, 4:5, :], mod_ref[0, 3:4, :])
    h2_ref[...] = h
    hh = h.astype(BF16)
    hl = (h - hh.astype(F32)).astype(BF16)
    lg = _nt(rwh_ref[...], hh) + _nt(rwh_ref[...], hl) + _nt(rwl_ref[...], hh)
    cls = _route_rows(lg, rb_ref[...])
    ohf = (lax.broadcasted_iota(jnp.int32, (32, TM), 0).astype(F32) == cls).astype(F32)
    rank_all = _dot(ohf.astype(BF16), ut_ref[...])
    info_ref[0:1, :] = cls
    info_ref[1:2, :] = jnp.sum(ohf * rank_all, axis=0, keepdims=True)
    info_ref[2:8, :] = jnp.zeros((6, TM), F32)
    cnt_ref[0] = jnp.broadcast_to(jnp.sum(ohf, axis=1, keepdims=True), (32, 128))


def _post(lay, z, mod, src_a, src_a_ctx, src_b, blk_a, blk_b, wa, wb, g_ffn, rwh, rwl, rbias, ut):
    d = z.shape[1]
    nt = lay.NT
    fw = FOURIER_WIDTH
    if src_a_ctx is None:
        nl_split = None
        src_a_ctx = jnp.zeros((fw, TM), BF16)
        spec_a = pl.BlockSpec((fw, TM), lambda u: (blk_a, u))
        spec_actx = pl.BlockSpec((fw, TM), lambda u: (0, 0))
    else:
        nl_split = lay.NL
        spec_a = pl.BlockSpec((fw, TM), lambda u: (blk_a, jnp.minimum(u, lay.NL - 1)))
        spec_actx = pl.BlockSpec((fw, TM), lambda u: (0, jnp.maximum(u - lay.NL, 0)))
    in_specs = [
        pl.BlockSpec((TM, d), lambda u: (u, 0)),
        pl.BlockSpec((1, 6, d), lambda u: (lay.mrow(u), 0, 0)),
        spec_a,
        spec_actx,
        pl.BlockSpec((fw, TM), lambda u: (blk_b, u)),
        _full(wa), _full(wb), _full(g_ffn), _full(rwh), _full(rwl), _full(rbias), _full(ut),
    ]
    out_specs = [
        pl.BlockSpec((TM, d), lambda u: (u, 0)),
        pl.BlockSpec((TM, d), lambda u: (u, 0)),
        pl.BlockSpec((8, TM), lambda u: (0, u)),
        pl.BlockSpec((1, 32, 128), lambda u: (u, 0, 0)),
    ]
    out_shape = [
        jax.ShapeDtypeStruct((lay.TOKP, d), F32),
        jax.ShapeDtypeStruct((lay.TOKP, d), F32),
        jax.ShapeDtypeStruct((8, lay.TOKP), F32),
        jax.ShapeDtypeStruct((nt, 32, 128), F32),
    ]
    return pl.pallas_call(
        functools.partial(_post_kernel, nl_split=nl_split),
        grid=(nt,), in_specs=in_specs, out_specs=out_specs, out_shape=out_shape,
        compiler_params=_cparams(), name="post",
    )(z, mod, src_a, src_a_ctx, src_b, wa, wb, g_ffn, rwh, rwl, rbias, ut)


def _invperm_kernel(pos_ref, tok_ref):
    def clear(p, c):
        tok_ref[p] = 0
        return c
    lax.fori_loop(0, tok_ref.shape[0], clear, 0, unroll=16)

    def put(t, c):
        tok_ref[pos_ref[t]] = t
        return c
    lax.fori_loop(0, pos_ref.shape[0], put, 0, unroll=16)


def _invperm(pos, pmax):
    return pl.pallas_call(
        _invperm_kernel,
        in_specs=[pl.BlockSpec(memory_space=pltpu.SMEM)],
        out_specs=pl.BlockSpec(memory_space=pltpu.SMEM),
        out_shape=jax.ShapeDtypeStruct((pmax,), jnp.int32),
        name="moe_invperm",
    )(pos)


def _expert_kernel(ea_ref, eb_ref, nu_ref, nv_ref, tok_ref, tokn_ref, h2_ref, rw_ref, wga_ref, wua_ref, wda_ref,
                   wgb_ref, wub_ref, wdb_ref, y_ref, xbuf, gsem):
    del nv_ref
    i = pl.program_id(0)
    nu = nu_ref[0]

    def gather_start(t_ref, slot):
        for r in range(TMM):
            pltpu.make_async_copy(h2_ref.at[pl.ds(t_ref[0, 0, r], 1)], xbuf.at[slot, pl.ds(r, 1)],
                                  gsem.at[slot]).start(priority=r % 2)

    def gather_wait(slot):
        pltpu.make_async_copy(h2_ref.at[pl.ds(0, TMM)], xbuf.at[slot], gsem.at[slot]).wait()

    @pl.when(i < nu)
    def _():
        slot = i % 2

        @pl.when(i == 0)
        def _():
            gather_start(tok_ref, 0)

        gather_wait(slot)
        gather_start(tokn_ref, 1 - slot)

        xb = xbuf[slot].astype(BF16)
        s = jax.nn.sigmoid(_dot(xb, rw_ref[...]))
        lane = lax.broadcasted_iota(jnp.int32, s.shape, 1)
        sa = jnp.sum(jnp.where(lane == ea_ref[i], s, 0.0), axis=1, keepdims=True)
        sb = jnp.sum(jnp.where(lane == eb_ref[i], s, 0.0), axis=1, keepdims=True)
        inv = 1.0 / (sa + sb)

        def hidden(wg_ref, wu_ref, gate):
            g = _dot(xb, wg_ref[0, 0])
            u = _dot(xb, wu_ref[0, 0])
            return (g * jax.nn.sigmoid(g) * u * gate).astype(BF16)

        y_ref[...] = (_dot(hidden(wga_ref, wua_ref, sa * inv), wda_ref[0, 0])
                      + _dot(hidden(wgb_ref, wub_ref, sb * inv), wdb_ref[0, 0]))

        @pl.when(i == nu - 1)
        def _():
            gather_wait(1 - slot)

    @pl.when(i >= nu)
    def _():
        y_ref[...] = jnp.zeros(y_ref.shape, F32)


def _experts(layer, tok3, h2, ea, eb, nused, nvalid, rw, wg, wu, wd):
    ntile = tok3.shape[0]
    d = wd.shape[3]
    w_in = lambda sel: pl.BlockSpec((1, 1, d, D_EXPERT), lambda i, ea, eb, nu, nv: (layer, sel(ea, eb)[i], 0, 0))
    w_out = lambda sel: pl.BlockSpec((1, 1, D_EXPERT, d), lambda i, ea, eb, nu, nv: (layer, sel(ea, eb)[i], 0, 0))
    first = lambda a, b: a
    second = lambda a, b: b
    grid_spec = pltpu.PrefetchScalarGridSpec(
        num_scalar_prefetch=4,
        grid=(ntile,),
        in_specs=[
            pl.BlockSpec((1, 1, TMM), lambda i, *_: (i, 0, 0), memory_space=pltpu.SMEM),
            pl.BlockSpec((1, 1, TMM), lambda i, *_: (jnp.minimum(i + 1, ntile - 1), 0, 0), memory_space=pltpu.SMEM),
            pl.BlockSpec(memory_space=pl.ANY),
            pl.BlockSpec(rw.shape, lambda i, *_: (0, 0)),
            w_in(first), w_in(first), w_out(first),
            w_in(second), w_in(second), w_out(second),
        ],
        out_specs=pl.BlockSpec((TMM, d), lambda i, *_: (i, 0)),
        scratch_shapes=[pltpu.VMEM((2, TMM, d), F32), pltpu.SemaphoreType.DMA((2,))],
    )
    return pl.pallas_call(
        _expert_kernel, grid_spec=grid_spec,
        out_shape=jax.ShapeDtypeStruct((ntile * TMM, d), F32),
        compiler_params=_cparams(), name="moe_experts",
    )(ea, eb, nused, nvalid, tok3, tok3, h2, rw, wg, wu, wd, wg, wu, wd)


def _final_kernel(z_ref, ys_ref, pos_ref, posn_ref, mod_ref, gf_ref, o_ref, ybuf, sem, *, nt):
    x2 = z_ref[...] + mod_ref[0, 5:6, :] * _moe_rows(ys_ref, pos_ref, posn_ref, ybuf, sem, nt)
    ms = jnp.mean(x2 * x2, axis=-1, keepdims=True)
    o_ref[...] = x2 * lax.rsqrt(ms + EPS) * gf_ref[...]


def _final(lay, zin, g_final):
    z, ys, pos3, mod = zin
    d = z.shape[1]
    nt = lay.NL
    return pl.pallas_call(
        functools.partial(_final_kernel, nt=nt),
        grid=(nt,),
        in_specs=[
            pl.BlockSpec((TM, d), lambda u: (u, 0)),
            pl.BlockSpec(memory_space=pl.ANY),
            pl.BlockSpec((1, 1, TM), lambda u: (u, 0, 0), memory_space=pltpu.SMEM),
            pl.BlockSpec((1, 1, TM), lambda u: (jnp.minimum(u + 1, nt - 1), 0, 0), memory_space=pltpu.SMEM),
            pl.BlockSpec((1, 6, d), lambda u: (lay.mrow(u), 0, 0)),
            _full(g_final),
        ],
        out_specs=pl.BlockSpec((TM, d), lambda u: (u, 0)),
        out_shape=jax.ShapeDtypeStruct((nt * TM, d), F32),
        scratch_shapes=[pltpu.VMEM((2, TM, d), F32), pltpu.SemaphoreType.DMA((2,))],
        compiler_params=_cparams(), name="final_norm",
    )(z, ys, pos3, pos3, mod, g_final)


def _plan(info, cnt, nt, ntile_max):
    cls = info[0].astype(jnp.int32).reshape(nt, TM)
    rank = info[1].astype(jnp.int32).reshape(nt, TM)
    cnt = cnt[:, :N_CLASSES, 0].astype(jnp.int32)
    tot = jnp.sum(cnt, axis=0)
    ntile_c = (tot + TMM - 1) // TMM
    tile_end = jnp.cumsum(ntile_c)
    class_off = (tile_end - ntile_c) * TMM
    tile_off = jnp.cumsum(cnt, axis=0) - cnt
    base = class_off[None, :] + tile_off
    onehot = cls[:, :, None] == jnp.arange(N_CLASSES, dtype=jnp.int32)[None, None, :]
    pos = jnp.sum(jnp.where(onehot, base[:, None, :], 0), axis=-1) + rank
    nused = tile_end[-1]
    ti = jnp.minimum(jnp.arange(ntile_max, dtype=jnp.int32), nused - 1)
    tcls = jnp.sum((ti[:, None] >= tile_end[None, :]).astype(jnp.int32), axis=1)
    onec = tcls[:, None] == jnp.arange(N_CLASSES, dtype=jnp.int32)[None, :]
    class_end = jnp.sum(jnp.where(onec, (class_off + tot)[None, :], 0), axis=1)
    nvalid = jnp.clip(class_end - ti * TMM, 0, TMM)
    grp, pair = tcls // 6, tcls % 6
    lo = jnp.asarray(_PAIR_LO, jnp.int32)
    hi = jnp.asarray(_PAIR_HI, jnp.int32)
    onep = pair[:, None] == jnp.arange(6, dtype=jnp.int32)[None, :]
    ea = 4 * grp + jnp.sum(jnp.where(onep, lo[None, :], 0), axis=1)
    eb = 4 * grp + jnp.sum(jnp.where(onep, hi[None, :], 0), axis=1)
    return pos.reshape(nt * TM), ea, eb, nused.reshape(1), nvalid


def _dft_mats(n, scale):
    r = 64 if (n % 64 == 0 and n > 64) else 1
    k = lax.broadcasted_iota(jnp.int32, (1, n), 1)

    def cs(rows, step):
        j = lax.broadcasted_iota(jnp.int32, (rows, 1), 0) * step
        ang = ((j * k) % n).astype(F32) * (2.0 * math.pi / n)
        return jnp.cos(ang), jnp.sin(ang)

    ca, sa = cs(n // r, r)
    if r == 1:
        c, s = ca, sa
    else:
        cb, sb = cs(r, 1)
        c = (ca[:, None, :] * cb[None, :, :] - sa[:, None, :] * sb[None, :, :]).reshape(n, n)
        s = (sa[:, None, :] * cb[None, :, :] + ca[:, None, :] * sb[None, :, :]).reshape(n, n)
    return (c * scale).astype(BF16), (s * scale).astype(BF16)


def _rope_angles(S, dim):
    rows = S // GRID_W
    row_id = jnp.repeat(jnp.arange(rows, dtype=F32), GRID_W)
    col_id = jnp.tile(jnp.arange(GRID_W, dtype=F32), rows)
    n_freq = dim // 4
    inv = ROPE_BASE ** (-jnp.arange(n_freq, dtype=F32) / n_freq)
    return jnp.concatenate([row_id[:, None] * inv, col_id[:, None] * inv], axis=-1)


def _tables(lay):
    S, C = lay.S, lay.C
    tabs = {}
    cc, sc = _dft_mats(FOURIER_GROUP_DIM, FOURIER_GROUP_DIM ** -0.5)
    tabs["cs"] = jnp.concatenate([cc, sc], axis=0)
    tabs["cn"], tabs["sn"] = _dft_mats(S, S ** -0.5)
    tabs["cctx"], tabs["sctx"] = _dft_mats(C, C ** -0.5)
    for name, dim in (("mla", MLA_ROPE_DIM), ("gqa", GQA_HEAD_DIM)):
        ang = _rope_angles(S, dim)
        cos = jnp.concatenate([jnp.ones((TM, dim // 2), F32), jnp.cos(ang)], axis=0)
        sin = jnp.concatenate([jnp.zeros((TM, dim // 2), F32), jnp.sin(ang)], axis=0)
        tabs[name + "_ct"] = cos.T
        tabs[name + "_st"] = sin.T
        if name == "mla":
            pad = jnp.zeros((TM + S, 128 - dim), F32)
            tabs["mla_t1"] = jnp.concatenate([cos, cos, pad], axis=1)
            tabs["mla_t2"] = jnp.concatenate([sin, -sin, pad], axis=1)
    tabs["ut"] = (lax.broadcasted_iota(jnp.int32, (TM, TM), 0)
                  < lax.broadcasted_iota(jnp.int32, (TM, TM), 1)).astype(BF16)
    return tabs


def _col(g, rows):
    return jnp.broadcast_to(g.astype(F32)[:, None], (rows, TM))


def _even_weights(w_in, q_norm_g, w_qb, kv_norm_g, w_kvb, w_out):
    d = w_in.shape[0]
    o1 = FOURIER_WIDTH
    o2 = o1 + MLA_Q_RANK
    o3 = o2 + MLA_KV_RANK
    bf = lambda a: a.astype(BF16)
    w = {}
    w["wf"] = bf(w_in[:, :o1].T)
    w["wqa"] = bf(w_in[:, o1:o2].T)
    w["wkvat"] = bf(w_in[:, o2:o3].T)
    w["wkva"] = bf(w_in[:, o2:o3])
    w["wpr"] = bf(jnp.concatenate([w_in[:, o3:], jnp.zeros((d, 128 - MLA_ROPE_DIM), F32)], axis=1))
    w["gq"] = _col(q_norm_g, MLA_Q_RANK)
    w["wqb"] = bf(w_qb.T)
    w["gkvc"] = _col(kv_norm_g, MLA_KV_RANK)
    w["gkvr"] = kv_norm_g.astype(F32)[None, :]
    kvb = w_kvb.reshape(MLA_KV_RANK, MLA_HEADS, MLA_NOPE_DIM + MLA_V_DIM)
    w["wv"] = bf(kvb[:, :, MLA_NOPE_DIM:].reshape(MLA_KV_RANK, MLA_HEADS * MLA_V_DIM).T)
    w1 = jnp.concatenate([kvb[:, :, :MLA_NOPE_DIM],
                          jnp.zeros((MLA_KV_RANK, MLA_HEADS, DK - MLA_NOPE_DIM), F32)], axis=2)
    w1 = w1.reshape(MLA_KV_RANK, MLA_HEADS * DK)
    eye = jnp.eye(MLA_ROPE_DIM, dtype=F32)
    place = jnp.zeros((128, DK), F32).at[:MLA_ROPE_DIM, MLA_NOPE_DIM:MLA_QK_DIM].set(eye)
    swap = jnp.roll(eye, MLA_ROPE_DIM // 2, axis=1)
    place_s = jnp.zeros((128, DK), F32).at[:MLA_ROPE_DIM, MLA_NOPE_DIM:MLA_QK_DIM].set(swap)
    w["wk"] = bf(jnp.concatenate([w1, jnp.tile(place, (1, MLA_HEADS)), jnp.tile(place_s, (1, MLA_HEADS))], axis=0))
    w["wout_a"] = bf(w_out[:o1])
    w["wout_b"] = bf(w_out[o1:])
    return w


def _odd_weights(w_qkv, q_norm_g, k_norm_g, w_out):
    bf = lambda a: a.astype(BF16)
    qw = GQA_HEADS * GQA_HEAD_DIM
    kw = GQA_KV_HEADS * GQA_HEAD_DIM
    w = {}
    w["wq"] = bf(w_qkv[:, :qw].T)
    w["wk"] = bf(w_qkv[:, qw:qw + kw].T)
    w["wv"] = bf(w_qkv[:, qw + kw:].T)
    w["gq"] = _col(q_norm_g, GQA_HEAD_DIM)
    w["gk"] = _col(k_norm_g, GQA_HEAD_DIM)
    w["wout_a"] = bf(w_out[:FOURIER_WIDTH])
    w["wout_b"] = bf(w_out[FOURIER_WIDTH:])
    return w


def kernel(x, c, ctx, c_ctx, ada_w, ada_b, norm_mix_g, norm_ffn_g, ev_w_in, ev_q_norm_g, ev_w_qb, ev_kv_norm_g,
           ev_w_kvb, ev_w_out, od_w_qkv, od_q_norm_g, od_k_norm_g, od_w_out, router_w, router_b, exp_w_gate,
           exp_w_up, exp_w_down, final_norm_g):
    B, S, d = x.shape
    C = ctx.shape[1]
    depth = ada_w.shape[0]
    lay = _Layout(B, S, C)
    assert B + 1 <= 16
    tabs = _tables(lay)

    cond = jnp.concatenate([c, c_ctx[None, :], jnp.zeros((16 - B - 1, d), F32)], axis=0)
    mods = _modulation(cond, ada_w, ada_b).reshape(depth, 16, 6, d)

    z = jnp.concatenate([x.reshape(B * S, d), ctx.reshape(B * C, d)], axis=0)

    rwt = router_w.T.astype(F32)
    rwh = rwt.astype(BF16)
    rwl = (rwt - rwh.astype(F32)).astype(BF16)
    rbias = jnp.broadcast_to(router_b.astype(F32)[:, None], (N_EXPERTS, TM))
    rw_nat = jnp.concatenate([router_w, jnp.zeros((d, 128 - N_EXPERTS), F32)], axis=1).astype(BF16)
    wg_all = exp_w_gate.astype(BF16)
    wu_all = exp_w_up.astype(BF16)
    wd_all = exp_w_down.astype(BF16)
    ntile_max = lay.NT * TM // TMM + N_CLASSES

    zin = (z,)
    for i in range(depth):
        mod = mods[i]
        g_mix = norm_mix_g[i][None, :]
        g_ffn = norm_ffn_g[i][None, :]
        j = i // 2
        if i % 2 == 0:
            w = _even_weights(ev_w_in[j], ev_q_norm_g[j], ev_w_qb[j], ev_kv_norm_g[j], ev_w_kvb[j], ev_w_out[j])
            abt, qt, k, vt, *znew = _pre_even(lay, zin, mod, g_mix, w, tabs)
            ot = _attention(lay, qt, k, vt, n_kv=MLA_HEADS, grp=1, dq=DK, kb=8)
            src_a, src_a_ctx = _seq_dft(lay, abt, tabs["cn"], tabs["sn"], tabs["cctx"], tabs["sctx"])
            src_b, blk_a, blk_b = ot, 0, 0
        else:
            w = _odd_weights(od_w_qkv[j], od_q_norm_g[j], od_k_norm_g[j], od_w_out[j])
            qt, k, vt, *znew = _pre_odd(lay, zin, mod, g_mix, w, tabs)
            ot = _attention(lay, qt, k, vt, n_kv=GQA_KV_HEADS, grp=GQA_GROUP, dq=GQA_HEAD_DIM, kb=4)
            src_a, src_a_ctx, src_b, blk_a, blk_b = ot, None, ot, 0, 1
        z = znew[0] if znew else zin[0]
        z1, h2, info, cnt = _post(lay, z, mod, src_a, src_a_ctx, src_b, blk_a, blk_b, w["wout_a"], w["wout_b"],
                                  g_ffn, rwh, rwl, rbias, tabs["ut"])
        pos, ea, eb, nused, nvalid = _plan(info, cnt, lay.NT, ntile_max)
        tok3 = _invperm(pos, ntile_max * TMM).reshape(ntile_max, 1, TMM)
        ys = _experts(i, tok3, h2, ea, eb, nused, nvalid, rw_nat, wg_all, wu_all, wd_all)
        zin = (z1, ys, pos.reshape(lay.NT, 1, TM), mod)
    out = _final(lay, zin, final_norm_g[None, :])
    return out.reshape(B, S, d)
```

```python
import functools
import math

import jax
import jax.numpy as jnp
from jax import lax
from jax.experimental import pallas as pl
from jax.experimental.pallas import tpu as pltpu

F32 = jnp.float32
BF16 = jnp.bfloat16

TM = 512
TQ = 256
TKC = 512
TMM = 256
TN_DFT = 512
EPS = 1e-6
LOG2E = 1.4426950408889634
ROPE_BASE = 10000.0
GRID_W = 64

FOURIER_GROUPS = 4
FOURIER_GROUP_DIM = 128
FOURIER_WIDTH = 512
MLA_HEADS = 8
MLA_NOPE_DIM = 64
MLA_ROPE_DIM = 32
MLA_V_DIM = 64
MLA_QK_DIM = 96
MLA_Q_RANK = 256
MLA_KV_RANK = 128
MLA_SCALE = MLA_QK_DIM ** -0.5
GQA_HEADS = 16
GQA_KV_HEADS = 4
GQA_HEAD_DIM = 64
GQA_GROUP = 4
GQA_SCALE = GQA_HEAD_DIM ** -0.5
N_EXPERTS = 16
N_GROUPS = 4
N_CLASSES = 24
D_EXPERT = 512
DK = 128
DV = 64
VMEM_LIMIT = 56 * 1024 * 1024

_PAIR_LO = (0, 0, 0, 1, 1, 2)
_PAIR_HI = (1, 2, 3, 2, 3, 3)


def _nt(a, b):
    return lax.dot_general(a, b, (((1,), (1,)), ((), ())), preferred_element_type=F32)


def _tn(a, b):
    return lax.dot_general(a, b, (((0,), (0,)), ((), ())), preferred_element_type=F32)


def _dot(a, b):
    return jnp.dot(a, b, preferred_element_type=F32)


class _Layout:
    def __init__(self, B, S, C):
        assert C == TQ and TM % C == 0 and (B * C) % TM == 0 and S % TM == 0 and S % TN_DFT == 0 and S % TKC == 0
        self.B, self.S, self.C = B, S, C
        self.SB = S // TM
        self.NL = B * self.SB
        self.NT = self.NL + B * C // TM
        self.TOKP = self.NT * TM
        self.CB = B * S // C

    def mrow(self, u):
        return jnp.where(u < self.NL, u // self.SB, self.B)

    def ropeblk(self, u):
        return jnp.where(u < self.NL, 1 + u % self.SB, 0)


def _cparams():
    return pltpu.CompilerParams(vmem_limit_bytes=VMEM_LIMIT)


def _full(a):
    return pl.BlockSpec(a.shape, lambda *_: (0,) * a.ndim)


def _mod_kernel(s_ref, w_ref, b_ref, o_ref):
    s = s_ref[...]
    a = s * jax.nn.sigmoid(s)
    ah = a.astype(BF16)
    al = (a - ah.astype(F32)).astype(BF16)
    w = w_ref[0]
    wh = w.astype(BF16)
    wl = (w - wh.astype(F32)).astype(BF16)
    o_ref[0] = _dot(ah, wh) + _dot(ah, wl) + _dot(al, wh) + b_ref[0]


def _modulation(cond, ada_w, ada_b):
    depth, d, w6 = ada_w.shape
    tn = 768
    return pl.pallas_call(
        _mod_kernel,
        grid=(depth, w6 // tn),
        in_specs=[
            pl.BlockSpec((16, d), lambda i, j: (0, 0)),
            pl.BlockSpec((1, d, tn), lambda i, j: (i, 0, j)),
            pl.BlockSpec((1, 1, tn), lambda i, j: (i, 0, j)),
        ],
        out_specs=pl.BlockSpec((1, 16, tn), lambda i, j: (i, 0, j)),
        out_shape=jax.ShapeDtypeStruct((depth, 16, w6), F32),
        compiler_params=_cparams(),
        name="modulation",
    )(cond, ada_w, ada_b.reshape(depth, 1, w6))


def _norm_mod(z, g, sc, sh):
    ms = jnp.mean(z * z, axis=-1, keepdims=True)
    return (z * lax.rsqrt(ms + EPS) * g) * (1.0 + sc) + sh


def _zin_specs(lay, zin, d):
    row = pl.BlockSpec((TM, d), lambda u: (u, 0))
    if len(zin) == 1:
        return [row], [], []
    specs = [row, row, pl.BlockSpec((1, 6, d), lambda u: (lay.mrow(u), 0, 0))]
    return specs, [row], [jax.ShapeDtypeStruct((lay.TOKP, d), F32)]


def _zin_value(zin_refs, zout_refs):
    if len(zin_refs) == 1:
        return zin_refs[0][...]
    z_ref, y_ref, modp_ref = zin_refs
    z = z_ref[...] + modp_ref[0, 5:6, :] * y_ref[...]
    zout_refs[0][...] = z
    return z


def _pre_even_kernel(*refs, n_zin):
    (mod_ref, g_ref, wf_ref, wqa_ref, wkvat_ref, wkva_ref, wpr_ref, gq_ref, wqb_ref, gkvc_ref, gkvr_ref, wv_ref,
     wk_ref, cs_ref, ct_ref, st_ref, t1_ref, t2_ref, ab_ref, q_ref, k_ref, v_ref) = refs[n_zin:n_zin + 22]
    z = _zin_value(refs[:n_zin], refs[n_zin + 22:])
    h = _norm_mod(z, g_ref[...], mod_ref[0, 1:2, :], mod_ref[0, 0:1, :])
    hb = h.astype(BF16)

    ft = _nt(wf_ref[...], hb).astype(BF16)
    gd = FOURIER_GROUP_DIM
    for g in range(FOURIER_GROUPS):
        ab = _dot(cs_ref[...], ft[g * gd:(g + 1) * gd, :])
        ab_ref[g * gd:(g + 1) * gd, :] = ab[:gd].astype(BF16)
        ab_ref[FOURIER_WIDTH + g * gd:FOURIER_WIDTH + (g + 1) * gd, :] = ab[gd:].astype(BF16)

    pq = _nt(wqa_ref[...], hb)
    rs = lax.rsqrt(jnp.mean(pq * pq, axis=0, keepdims=True) + EPS)
    qn = (pq * rs * gq_ref[...]).astype(BF16)
    qt = _dot(wqb_ref[...], qn) * (MLA_SCALE * LOG2E)
    cos = ct_ref[...]
    sin = st_ref[...]
    hr = MLA_ROPE_DIM // 2
    for hd in range(MLA_HEADS):
        o = hd * MLA_QK_DIM
        r = hd * DK
        x1 = qt[o + MLA_NOPE_DIM:o + MLA_NOPE_DIM + hr, :]
        x2 = qt[o + MLA_NOPE_DIM + hr:o + MLA_QK_DIM, :]
        q_ref[r:r + MLA_NOPE_DIM, :] = qt[o:o + MLA_NOPE_DIM, :].astype(BF16)
        q_ref[r + MLA_NOPE_DIM:r + MLA_NOPE_DIM + hr, :] = (x1 * cos - x2 * sin).astype(BF16)
        q_ref[r + MLA_NOPE_DIM + hr:r + MLA_QK_DIM, :] = (x1 * sin + x2 * cos).astype(BF16)
        q_ref[r + MLA_QK_DIM:r + DK, :] = jnp.zeros((DK - MLA_QK_DIM, TM), BF16)

    pkvt = _nt(wkvat_ref[...], hb)
    rst = lax.rsqrt(jnp.mean(pkvt * pkvt, axis=0, keepdims=True) + EPS)
    kvnt = (pkvt * rst * gkvc_ref[...]).astype(BF16)
    v_ref[...] = _dot(wv_ref[...], kvnt).astype(BF16)

    pkv = _dot(hb, wkva_ref[...])
    rsn = lax.rsqrt(jnp.mean(pkv * pkv, axis=-1, keepdims=True) + EPS)
    kvn = (pkv * rsn * gkvr_ref[...]).astype(BF16)
    pr = _dot(hb, wpr_ref[...])
    lhs = jnp.concatenate([kvn, (pr * t1_ref[...]).astype(BF16), (pr * t2_ref[...]).astype(BF16)], axis=1)
    kf = _dot(lhs, wk_ref[...])
    for hd in range(MLA_HEADS):
        k_ref[hd] = kf[:, hd * DK:(hd + 1) * DK].astype(BF16)


def _pre_even(lay, zin, mod, g_mix, w, tabs):
    d = mod.shape[-1]
    tokp = lay.TOKP
    col = lambda rows: pl.BlockSpec((rows, TM), lambda u: (0, u))
    hr = MLA_ROPE_DIM // 2
    zin_specs, zout_specs, zout_shape = _zin_specs(lay, zin, d)
    in_specs = zin_specs + [
        pl.BlockSpec((1, 6, d), lambda u: (lay.mrow(u), 0, 0)),
        _full(g_mix),
        _full(w["wf"]), _full(w["wqa"]), _full(w["wkvat"]), _full(w["wkva"]), _full(w["wpr"]),
        _full(w["gq"]), _full(w["wqb"]), _full(w["gkvc"]), _full(w["gkvr"]), _full(w["wv"]), _full(w["wk"]),
        _full(tabs["cs"]),
        pl.BlockSpec((hr, TM), lambda u: (0, lay.ropeblk(u))),
        pl.BlockSpec((hr, TM), lambda u: (0, lay.ropeblk(u))),
        pl.BlockSpec((TM, 128), lambda u: (lay.ropeblk(u), 0)),
        pl.BlockSpec((TM, 128), lambda u: (lay.ropeblk(u), 0)),
    ]
    out_specs = [
        col(2 * FOURIER_WIDTH),
        col(MLA_HEADS * DK),
        pl.BlockSpec((MLA_HEADS, TM, DK), lambda u: (0, u, 0)),
        col(MLA_HEADS * DV),
    ]
    out_shape = [
        jax.ShapeDtypeStruct((2 * FOURIER_WIDTH, tokp), BF16),
        jax.ShapeDtypeStruct((MLA_HEADS * DK, tokp), BF16),
        jax.ShapeDtypeStruct((MLA_HEADS, tokp, DK), BF16),
        jax.ShapeDtypeStruct((MLA_HEADS * DV, tokp), BF16),
    ]
    return pl.pallas_call(
        functools.partial(_pre_even_kernel, n_zin=len(zin)),
        grid=(lay.NT,), in_specs=in_specs, out_specs=out_specs + zout_specs, out_shape=out_shape + zout_shape,
        compiler_params=_cparams(), name="pre_even",
    )(*zin, mod, g_mix, w["wf"], w["wqa"], w["wkvat"], w["wkva"], w["wpr"], w["gq"], w["wqb"], w["gkvc"],
      w["gkvr"], w["wv"], w["wk"], tabs["cs"], tabs["mla_ct"], tabs["mla_st"], tabs["mla_t1"], tabs["mla_t2"])


def _pre_odd_kernel(*refs, n_zin):
    (mod_ref, g_ref, wq_ref, wk_ref, wv_ref, gq_ref, gk_ref, ct_ref, st_ref,
     q_ref, k_ref, v_ref) = refs[n_zin:n_zin + 12]
    z = _zin_value(refs[:n_zin], refs[n_zin + 12:])
    h = _norm_mod(z, g_ref[...], mod_ref[0, 1:2, :], mod_ref[0, 0:1, :])
    hb = h.astype(BF16)
    cos = ct_ref[...]
    sin = st_ref[...]
    hd2 = GQA_HEAD_DIM // 2

    def norm_rope(xt, gain, scale):
        rs = lax.rsqrt(jnp.mean(xt * xt, axis=0, keepdims=True) + EPS)
        xn = xt * rs * gain
        x1 = xn[:hd2]
        x2 = xn[hd2:]
        return (x1 * cos - x2 * sin) * scale, (x1 * sin + x2 * cos) * scale

    qt = _nt(wq_ref[...], hb)
    for hd in range(GQA_HEADS):
        o = hd * GQA_HEAD_DIM
        r1, r2 = norm_rope(qt[o:o + GQA_HEAD_DIM], gq_ref[...], GQA_SCALE * LOG2E)
        q_ref[o:o + hd2, :] = r1.astype(BF16)
        q_ref[o + hd2:o + GQA_HEAD_DIM, :] = r2.astype(BF16)

    kt = _nt(wk_ref[...], hb)
    zero = jnp.zeros((DK - GQA_HEAD_DIM, TM), F32)
    for hd in range(GQA_KV_HEADS):
        o = hd * GQA_HEAD_DIM
        r1, r2 = norm_rope(kt[o:o + GQA_HEAD_DIM], gk_ref[...], 1.0)
        khead = jnp.concatenate([r1, r2, zero], axis=0)
        k_ref[hd] = khead.T.astype(BF16)

    v_ref[...] = _nt(wv_ref[...], hb).astype(BF16)


def _pre_odd(lay, zin, mod, g_mix, w, tabs):
    d = mod.shape[-1]
    tokp = lay.TOKP
    col = lambda rows: pl.BlockSpec((rows, TM), lambda u: (0, u))
    hd2 = GQA_HEAD_DIM // 2
    zin_specs, zout_specs, zout_shape = _zin_specs(lay, zin, d)
    in_specs = zin_specs + [
        pl.BlockSpec((1, 6, d), lambda u: (lay.mrow(u), 0, 0)),
        _full(g_mix), _full(w["wq"]), _full(w["wk"]), _full(w["wv"]), _full(w["gq"]), _full(w["gk"]),
        pl.BlockSpec((hd2, TM), lambda u: (0, lay.ropeblk(u))),
        pl.BlockSpec((hd2, TM), lambda u: (0, lay.ropeblk(u))),
    ]
    out_specs = [
        col(GQA_HEADS * GQA_HEAD_DIM),
        pl.BlockSpec((GQA_KV_HEADS, TM, DK), lambda u: (0, u, 0)),
        col(GQA_KV_HEADS * DV),
    ]
    out_shape = [
        jax.ShapeDtypeStruct((GQA_HEADS * GQA_HEAD_DIM, tokp), BF16),
        jax.ShapeDtypeStruct((GQA_KV_HEADS, tokp, DK), BF16),
        jax.ShapeDtypeStruct((GQA_KV_HEADS * DV, tokp), BF16),
    ]
    return pl.pallas_call(
        functools.partial(_pre_odd_kernel, n_zin=len(zin)),
        grid=(lay.NT,), in_specs=in_specs, out_specs=out_specs + zout_specs, out_shape=out_shape + zout_shape,
        compiler_params=_cparams(), name="pre_odd",
    )(*zin, mod, g_mix, w["wq"], w["wk"], w["wv"], w["gq"], w["gk"], tabs["gqa_ct"], tabs["gqa_st"])


def _attn_kernel(q_ref, kc_ref, kl_ref, vc_ref, vl_ref, o_ref, s0_ref, s1_ref, *, nq, kb, grp, dq, S, C):
    qi = pl.program_id(2)
    hb = kb * grp
    sbufs = (s0_ref, s1_ref)
    nchunk = S // TKC

    def scores(hh, buf, with_latent):
        kv = hh // grp
        q = q_ref[hh * dq:(hh + 1) * dq, :]
        sc = _dot(kc_ref[kv, :, :dq], q)
        buf[0:C, :] = sc
        m = jnp.max(sc, axis=0, keepdims=True)
        if with_latent:
            for c in range(nchunk):
                s = _dot(kl_ref[kv, c * TKC:(c + 1) * TKC, :dq], q)
                buf[C + c * TKC:C + (c + 1) * TKC, :] = s
                m = jnp.maximum(m, jnp.max(s, axis=0, keepdims=True))
        return m

    def values(hh, buf, m, with_latent):
        kv = hh // grp
        vrows = slice(kv * DV, (kv + 1) * DV)
        p = jnp.exp2(buf[0:C, :] - m)
        l = jnp.sum(p, axis=0, keepdims=True)
        o = _dot(vc_ref[vrows, :], p.astype(BF16))
        if with_latent:
            for c in range(nchunk):
                p = jnp.exp2(buf[C + c * TKC:C + (c + 1) * TKC, :] - m)
                l = l + jnp.sum(p, axis=0, keepdims=True)
                o = o + _dot(vl_ref[vrows, c * TKC:(c + 1) * TKC], p.astype(BF16))
        o_ref[hh * DV:(hh + 1) * DV, :] = (o * (1.0 / l)).astype(o_ref.dtype)

    def run(with_latent):
        m = scores(0, sbufs[0], with_latent)
        for hh in range(hb):
            m_next = scores(hh + 1, sbufs[(hh + 1) % 2], with_latent) if hh + 1 < hb else None
            values(hh, sbufs[hh % 2], m, with_latent)
            m = m_next

    @pl.when(qi < nq)
    def _():
        run(True)

    @pl.when(qi == nq)
    def _():
        run(False)


def _attention(lay, qt, k, vt, *, n_kv, grp, dq, kb):
    B, S, C = lay.B, lay.S, lay.C
    hb = kb * grp
    nq = S // TQ

    def qcol(b, qi):
        return jnp.where(qi == nq, lay.CB + b, b * nq + qi)

    in_specs = [
        pl.BlockSpec((hb * dq, TQ), lambda b, h, qi: (h, qcol(b, qi))),
        pl.BlockSpec((kb, C, DK), lambda b, h, qi: (h, lay.CB + b, 0)),
        pl.BlockSpec((kb, S, DK), lambda b, h, qi: (h, b, 0)),
        pl.BlockSpec((kb * DV, C), lambda b, h, qi: (h, lay.CB + b)),
        pl.BlockSpec((kb * DV, S), lambda b, h, qi: (h, b)),
    ]
    return pl.pallas_call(
        functools.partial(_attn_kernel, nq=nq, kb=kb, grp=grp, dq=dq, S=S, C=C),
        grid=(B, n_kv // kb, nq + 1),
        in_specs=in_specs,
        out_specs=pl.BlockSpec((hb * DV, TQ), lambda b, h, qi: (h, qcol(b, qi))),
        out_shape=jax.ShapeDtypeStruct((n_kv * grp * DV, lay.TOKP), BF16),
        scratch_shapes=[pltpu.VMEM((C + S, TQ), F32), pltpu.VMEM((C + S, TQ), F32)],
        compiler_params=_cparams(),
        name="attention",
    )(qt, k, k, vt, vt)


def _dft_kernel(a_ref, b_ref, c_ref, s_ref, o_ref):
    o_ref[...] = (_dot(a_ref[...], c_ref[...]) - _dot(b_ref[...], s_ref[...])).astype(o_ref.dtype)


def _seq_dft(lay, abt, cn, sn, cc, sc):
    B, S, C = lay.B, lay.S, lay.C
    fw = FOURIER_WIDTH
    nj = S // TN_DFT
    ylat = pl.pallas_call(
        _dft_kernel,
        grid=(nj, B),
        in_specs=[
            pl.BlockSpec((fw, S), lambda j, b: (0, b)),
            pl.BlockSpec((fw, S), lambda j, b: (1, b)),
            pl.BlockSpec((S, TN_DFT), lambda j, b: (0, j)),
            pl.BlockSpec((S, TN_DFT), lambda j, b: (0, j)),
        ],
        out_specs=pl.BlockSpec((fw, TN_DFT), lambda j, b: (0, b * nj + j)),
        out_shape=jax.ShapeDtypeStruct((fw, B * S), BF16),
        compiler_params=_cparams(),
        name="seq_dft",
    )(abt, abt, cn, sn)
    yctx = pl.pallas_call(
        _dft_kernel,
        grid=(B,),
        in_specs=[
            pl.BlockSpec((fw, C), lambda b: (0, lay.CB + b)),
            pl.BlockSpec((fw, C), lambda b: (1, lay.CB + b)),
            pl.BlockSpec((C, C), lambda b: (0, 0)),
            pl.BlockSpec((C, C), lambda b: (0, 0)),
        ],
        out_specs=pl.BlockSpec((fw, C), lambda b: (0, b)),
        out_shape=jax.ShapeDtypeStruct((fw, B * C), BF16),
        compiler_params=_cparams(),
        name="ctx_dft",
    )(abt, abt, cc, sc)
    return ylat, yctx


def _route_rows(lg, rbias):
    s = jax.nn.sigmoid(lg)
    sel = s + rbias
    rows = [sel[e:e + 1, :] for e in range(N_EXPERTS)]
    best = None
    bg = None
    for g in range(N_GROUPS):
        v = rows[4 * g:4 * g + 4]
        sc = None
        for i in range(4):
            for j in range(i + 1, 4):
                ps = v[i] + v[j]
                sc = ps if sc is None else jnp.maximum(sc, ps)
        if best is None:
            best, bg = sc, jnp.zeros_like(sc)
        else:
            upd = sc > best
            bg = jnp.where(upd, float(g), bg)
            best = jnp.where(upd, sc, best)
    v = []
    for i in range(4):
        acc = rows[i]
        for g in range(1, N_GROUPS):
            acc = jnp.where(bg == float(g), rows[4 * g + i], acc)
        v.append(acc)
    i0 = jnp.zeros_like(bg)
    b0 = v[0]
    for i in range(1, 4):
        upd = v[i] > b0
        i0 = jnp.where(upd, float(i), i0)
        b0 = jnp.where(upd, v[i], b0)
    w = [jnp.where(i0 == float(i), -jnp.inf, v[i]) for i in range(4)]
    i1 = jnp.zeros_like(bg)
    b1 = w[0]
    for i in range(1, 4):
        upd = w[i] > b1
        i1 = jnp.where(upd, float(i), i1)
        b1 = jnp.where(upd, w[i], b1)
    lo = jnp.minimum(i0, i1)
    hi = jnp.maximum(i0, i1)
    pbase = jnp.where(lo == 0.0, 0.0, jnp.where(lo == 1.0, 3.0, 5.0))
    return bg * 6.0 + pbase + hi - lo - 1.0


def _post_kernel(z_ref, mod_ref, a_ref, actx_ref, b_ref, wa_ref, wb_ref, g_ref, rwh_ref, rwl_ref, rb_ref, ut_ref,
                 zo_ref, h2_ref, info_ref, cnt_ref, *, nl_split):
    a = a_ref[...]
    if nl_split is not None:
        a = jnp.where(pl.program_id(0) < nl_split, a, actx_ref[...])
    attn = _tn(a, wa_ref[...]) + _tn(b_ref[...], wb_ref[...])
    x1 = z_ref[...] + mod_ref[0, 2:3, :] * attn
    zo_ref[...] = x1
    h = _norm_mod(x1, g_ref[...], mod_ref[0, 4:5, :], mod_ref[0, 3:4, :])
    h2_ref[...] = h
    hh = h.astype(BF16)
    hl = (h - hh.astype(F32)).astype(BF16)
    lg = _nt(rwh_ref[...], hh) + _nt(rwh_ref[...], hl) + _nt(rwl_ref[...], hh)
    cls = _route_rows(lg, rb_ref[...])
    ohf = (lax.broadcasted_iota(jnp.int32, (32, TM), 0).astype(F32) == cls).astype(F32)
    rank_all = _dot(ohf.astype(BF16), ut_ref[...])
    info_ref[0:1, :] = cls
    info_ref[1:2, :] = jnp.sum(ohf * rank_all, axis=0, keepdims=True)
    info_ref[2:8, :] = jnp.zeros((6, TM), F32)
    cnt_ref[0] = jnp.broadcast_to(jnp.sum(ohf, axis=1, keepdims=True), (32, 128))


def _post(lay, z, mod, src_a, src_a_ctx, src_b, blk_a, blk_b, wa, wb, g_ffn, rwh, rwl, rbias, ut):
    d = z.shape[1]
    nt = lay.NT
    fw = FOURIER_WIDTH
    if src_a_ctx is None:
        nl_split = None
        src_a_ctx = jnp.zeros((fw, TM), BF16)
        spec_a = pl.BlockSpec((fw, TM), lambda u: (blk_a, u))
        spec_actx = pl.BlockSpec((fw, TM), lambda u: (0, 0))
    else:
        nl_split = lay.NL
        spec_a = pl.BlockSpec((fw, TM), lambda u: (blk_a, jnp.minimum(u, lay.NL - 1)))
        spec_actx = pl.BlockSpec((fw, TM), lambda u: (0, jnp.maximum(u - lay.NL, 0)))
    in_specs = [
        pl.BlockSpec((TM, d), lambda u: (u, 0)),
        pl.BlockSpec((1, 6, d), lambda u: (lay.mrow(u), 0, 0)),
        spec_a,
        spec_actx,
        pl.BlockSpec((fw, TM), lambda u: (blk_b, u)),
        _full(wa), _full(wb), _full(g_ffn), _full(rwh), _full(rwl), _full(rbias), _full(ut),
    ]
    out_specs = [
        pl.BlockSpec((TM, d), lambda u: (u, 0)),
        pl.BlockSpec((TM, d), lambda u: (u, 0)),
        pl.BlockSpec((8, TM), lambda u: (0, u)),
        pl.BlockSpec((1, 32, 128), lambda u: (u, 0, 0)),
    ]
    out_shape = [
        jax.ShapeDtypeStruct((lay.TOKP, d), F32),
        jax.ShapeDtypeStruct((lay.TOKP, d), F32),
        jax.ShapeDtypeStruct((8, lay.TOKP), F32),
        jax.ShapeDtypeStruct((nt, 32, 128), F32),
    ]
    return pl.pallas_call(
        functools.partial(_post_kernel, nl_split=nl_split),
        grid=(nt,), in_specs=in_specs, out_specs=out_specs, out_shape=out_shape,
        compiler_params=_cparams(), name="post",
    )(z, mod, src_a, src_a_ctx, src_b, wa, wb, g_ffn, rwh, rwl, rbias, ut)


def _invperm_kernel(pos_ref, tok_ref):
    def clear(p, c):
        tok_ref[p] = 0
        return c
    lax.fori_loop(0, tok_ref.shape[0], clear, 0, unroll=16)

    def put(t, c):
        tok_ref[pos_ref[t]] = t
        return c
    lax.fori_loop(0, pos_ref.shape[0], put, 0, unroll=16)


def _invperm(pos, pmax):
    return pl.pallas_call(
        _invperm_kernel,
        in_specs=[pl.BlockSpec(memory_space=pltpu.SMEM)],
        out_specs=pl.BlockSpec(memory_space=pltpu.SMEM),
        out_shape=jax.ShapeDtypeStruct((pmax,), jnp.int32),
        name="moe_invperm",
    )(pos)


def _dispatch_kernel(zf_ref, pos_ref, h_ref, xs_ref, zbuf, zsem, sem):
    u = pl.program_id(0)
    ntile = zf_ref.shape[0]

    def fill(i):
        return pltpu.make_async_copy(zbuf, xs_ref.at[pl.ds(pl.multiple_of(i * TMM, TMM), TMM)], zsem)

    @pl.when(u == 0)
    def _():
        zbuf[...] = jnp.zeros(zbuf.shape, F32)

        def start(i, c):
            @pl.when(zf_ref[i] != 0)
            def _():
                fill(i).start()
            return c

        def wait(i, c):
            @pl.when(zf_ref[i] != 0)
            def _():
                fill(i).wait()
            return c

        lax.fori_loop(0, ntile, start, 0)
        lax.fori_loop(0, ntile, wait, 0)

    for r in range(TM):
        pltpu.make_async_copy(h_ref.at[pl.ds(r, 1)], xs_ref.at[pl.ds(pos_ref[0, 0, r], 1)], sem).start(priority=r % 2)
    pltpu.make_async_copy(h_ref, xs_ref.at[pl.ds(0, TM)], sem).wait()


def _dispatch(lay, zfill, pos3, h2, pmax):
    d = h2.shape[1]
    grid_spec = pltpu.PrefetchScalarGridSpec(
        num_scalar_prefetch=1,
        grid=(lay.NT,),
        in_specs=[
            pl.BlockSpec((1, 1, TM), lambda u, *_: (u, 0, 0), memory_space=pltpu.SMEM),
            pl.BlockSpec((TM, d), lambda u, *_: (u, 0)),
        ],
        out_specs=pl.BlockSpec(memory_space=pl.ANY),
        scratch_shapes=[pltpu.VMEM((TMM, d), F32), pltpu.SemaphoreType.DMA(()), pltpu.SemaphoreType.DMA(())],
    )
    return pl.pallas_call(
        _dispatch_kernel, grid_spec=grid_spec,
        out_shape=jax.ShapeDtypeStruct((pmax, d), F32),
        compiler_params=_cparams(), name="moe_dispatch",
    )(zfill, pos3, h2)


def _expert_kernel(ea_ref, eb_ref, nu_ref, nv_ref, tok_ref, x_ref, rw_ref, wga_ref, wua_ref, wda_ref,
                   wgb_ref, wub_ref, wdb_ref, ys_ref, ybuf, ssem):
    i = pl.program_id(0)
    nu = nu_ref[0]
    ntok = ys_ref.shape[0] - TMM

    def scatter_start(n):
        for r in range(TMM):
            dst = jnp.where(r < n, tok_ref[0, 0, r], ntok + r)
            pltpu.make_async_copy(ybuf.at[pl.ds(r, 1)], ys_ref.at[pl.ds(dst, 1)], ssem).start(priority=r % 2)

    def scatter_wait():
        pltpu.make_async_copy(ybuf, ys_ref.at[pl.ds(0, TMM)], ssem).wait()

    @pl.when(i < nu)
    def _():
        @pl.when(i == 0)
        def _():
            ybuf[...] = jnp.zeros(ybuf.shape, F32)
            spare = pltpu.make_async_copy(ybuf, ys_ref.at[pl.ds(ntok, TMM)], ssem)
            spare.start()
            spare.wait()

        xb = x_ref[...].astype(BF16)
        s = jax.nn.sigmoid(_dot(xb, rw_ref[...]))
        lane = lax.broadcasted_iota(jnp.int32, s.shape, 1)
        sa = jnp.sum(jnp.where(lane == ea_ref[i], s, 0.0), axis=1, keepdims=True)
        sb = jnp.sum(jnp.where(lane == eb_ref[i], s, 0.0), axis=1, keepdims=True)
        inv = 1.0 / (sa + sb)

        def hidden(wg_ref, wu_ref, gate):
            g = _dot(xb, wg_ref[0, 0])
            u = _dot(xb, wu_ref[0, 0])
            return (g * jax.nn.sigmoid(g) * u * gate).astype(BF16)

        y = (_dot(hidden(wga_ref, wua_ref, sa * inv), wda_ref[0, 0])
             + _dot(hidden(wgb_ref, wub_ref, sb * inv), wdb_ref[0, 0]))

        @pl.when(i > 0)
        def _():
            scatter_wait()

        ybuf[...] = y
        scatter_start(nv_ref[i])

        @pl.when(i == nu - 1)
        def _():
            scatter_wait()


def _experts(lay, layer, tok3, xs, ea, eb, nused, nvalid, rw, wg, wu, wd):
    ntile = tok3.shape[0]
    d = wd.shape[3]
    w_in = lambda sel: pl.BlockSpec((1, 1, d, D_EXPERT), lambda i, ea, eb, nu, nv: (layer, sel(ea, eb)[i], 0, 0))
    w_out = lambda sel: pl.BlockSpec((1, 1, D_EXPERT, d), lambda i, ea, eb, nu, nv: (layer, sel(ea, eb)[i], 0, 0))
    first = lambda a, b: a
    second = lambda a, b: b
    grid_spec = pltpu.PrefetchScalarGridSpec(
        num_scalar_prefetch=4,
        grid=(ntile,),
        in_specs=[
            pl.BlockSpec((1, 1, TMM), lambda i, *_: (i, 0, 0), memory_space=pltpu.SMEM),
            pl.BlockSpec((TMM, d), lambda i, ea, eb, nu, nv: (jnp.minimum(i, nu[0] - 1), 0)),
            pl.BlockSpec(rw.shape, lambda i, *_: (0, 0)),
            w_in(first), w_in(first), w_out(first),
            w_in(second), w_in(second), w_out(second),
        ],
        out_specs=pl.BlockSpec(memory_space=pl.ANY),
        scratch_shapes=[pltpu.VMEM((TMM, d), F32), pltpu.SemaphoreType.DMA(())],
    )
    return pl.pallas_call(
        _expert_kernel, grid_spec=grid_spec,
        out_shape=jax.ShapeDtypeStruct((lay.TOKP + TMM, d), F32),
        compiler_params=_cparams(), name="moe_experts",
    )(ea, eb, nused, nvalid, tok3, xs, rw, wg, wu, wd, wg, wu, wd)


def _final_kernel(z_ref, y_ref, mod_ref, gf_ref, o_ref):
    x2 = z_ref[...] + mod_ref[0, 5:6, :] * y_ref[...]
    ms = jnp.mean(x2 * x2, axis=-1, keepdims=True)
    o_ref[...] = x2 * lax.rsqrt(ms + EPS) * gf_ref[...]


def _final(lay, z, ys, mod, g_final):
    d = z.shape[1]
    return pl.pallas_call(
        _final_kernel,
        grid=(lay.NL,),
        in_specs=[
            pl.BlockSpec((TM, d), lambda u: (u, 0)),
            pl.BlockSpec((TM, d), lambda u: (u, 0)),
            pl.BlockSpec((1, 6, d), lambda u: (lay.mrow(u), 0, 0)),
            _full(g_final),
        ],
        out_specs=pl.BlockSpec((TM, d), lambda u: (u, 0)),
        out_shape=jax.ShapeDtypeStruct((lay.NL * TM, d), F32),
        compiler_params=_cparams(), name="final_norm",
    )(z, ys, mod, g_final)


def _plan(info, cnt, nt, ntile_max):
    cls = info[0].astype(jnp.int32).reshape(nt, TM)
    rank = info[1].astype(jnp.int32).reshape(nt, TM)
    cnt = cnt[:, :N_CLASSES, 0].astype(jnp.int32)
    tot = jnp.sum(cnt, axis=0)
    ntile_c = (tot + TMM - 1) // TMM
    tile_end = jnp.cumsum(ntile_c)
    class_off = (tile_end - ntile_c) * TMM
    tile_off = jnp.cumsum(cnt, axis=0) - cnt
    base = class_off[None, :] + tile_off
    onehot = cls[:, :, None] == jnp.arange(N_CLASSES, dtype=jnp.int32)[None, None, :]
    pos = jnp.sum(jnp.where(onehot, base[:, None, :], 0), axis=-1) + rank
    nused = tile_end[-1]
    ti = jnp.minimum(jnp.arange(ntile_max, dtype=jnp.int32), nused - 1)
    tcls = jnp.sum((ti[:, None] >= tile_end[None, :]).astype(jnp.int32), axis=1)
    onec = tcls[:, None] == jnp.arange(N_CLASSES, dtype=jnp.int32)[None, :]
    class_end = jnp.sum(jnp.where(onec, (class_off + tot)[None, :], 0), axis=1)
    nvalid = jnp.clip(class_end - ti * TMM, 0, TMM)
    tiles = jnp.arange(ntile_max, dtype=jnp.int32)
    zfill = ((tiles >= nused) | (nvalid < TMM)).astype(jnp.int32)
    grp, pair = tcls // 6, tcls % 6
    lo = jnp.asarray(_PAIR_LO, jnp.int32)
    hi = jnp.asarray(_PAIR_HI, jnp.int32)
    onep = pair[:, None] == jnp.arange(6, dtype=jnp.int32)[None, :]
    ea = 4 * grp + jnp.sum(jnp.where(onep, lo[None, :], 0), axis=1)
    eb = 4 * grp + jnp.sum(jnp.where(onep, hi[None, :], 0), axis=1)
    return pos.reshape(nt * TM), ea, eb, nused.reshape(1), nvalid, zfill


def _dft_mats(n, scale):
    r = 64 if (n % 64 == 0 and n > 64) else 1
    k = lax.broadcasted_iota(jnp.int32, (1, n), 1)

    def cs(rows, step):
        j = lax.broadcasted_iota(jnp.int32, (rows, 1), 0) * step
        ang = ((j * k) % n).astype(F32) * (2.0 * math.pi / n)
        return jnp.cos(ang), jnp.sin(ang)

    ca, sa = cs(n // r, r)
    if r == 1:
        c, s = ca, sa
    else:
        cb, sb = cs(r, 1)
        c = (ca[:, None, :] * cb[None, :, :] - sa[:, None, :] * sb[None, :, :]).reshape(n, n)
        s = (sa[:, None, :] * cb[None, :, :] + ca[:, None, :] * sb[None, :, :]).reshape(n, n)
    return (c * scale).astype(BF16), (s * scale).astype(BF16)


def _rope_angles(S, dim):
    rows = S // GRID_W
    row_id = jnp.repeat(jnp.arange(rows, dtype=F32), GRID_W)
    col_id = jnp.tile(jnp.arange(GRID_W, dtype=F32), rows)
    n_freq = dim // 4
    inv = ROPE_BASE ** (-jnp.arange(n_freq, dtype=F32) / n_freq)
    return jnp.concatenate([row_id[:, None] * inv, col_id[:, None] * inv], axis=-1)


def _tables(lay):
    S, C = lay.S, lay.C
    tabs = {}
    cc, sc = _dft_mats(FOURIER_GROUP_DIM, FOURIER_GROUP_DIM ** -0.5)
    tabs["cs"] = jnp.concatenate([cc, sc], axis=0)
    tabs["cn"], tabs["sn"] = _dft_mats(S, S ** -0.5)
    tabs["cctx"], tabs["sctx"] = _dft_mats(C, C ** -0.5)
    for name, dim in (("mla", MLA_ROPE_DIM), ("gqa", GQA_HEAD_DIM)):
        ang = _rope_angles(S, dim)
        cos = jnp.concatenate([jnp.ones((TM, dim // 2), F32), jnp.cos(ang)], axis=0)
        sin = jnp.concatenate([jnp.zeros((TM, dim // 2), F32), jnp.sin(ang)], axis=0)
        tabs[name + "_ct"] = cos.T
        tabs[name + "_st"] = sin.T
        if name == "mla":
            pad = jnp.zeros((TM + S, 128 - dim), F32)
            tabs["mla_t1"] = jnp.concatenate([cos, cos, pad], axis=1)
            tabs["mla_t2"] = jnp.concatenate([sin, -sin, pad], axis=1)
    tabs["ut"] = (lax.broadcasted_iota(jnp.int32, (TM, TM), 0)
                  < lax.broadcasted_iota(jnp.int32, (TM, TM), 1)).astype(BF16)
    return tabs


def _col(g, rows):
    return jnp.broadcast_to(g.astype(F32)[:, None], (rows, TM))


def _even_weights(w_in, q_norm_g, w_qb, kv_norm_g, w_kvb, w_out):
    d = w_in.shape[0]
    o1 = FOURIER_WIDTH
    o2 = o1 + MLA_Q_RANK
    o3 = o2 + MLA_KV_RANK
    bf = lambda a: a.astype(BF16)
    w = {}
    w["wf"] = bf(w_in[:, :o1].T)
    w["wqa"] = bf(w_in[:, o1:o2].T)
    w["wkvat"] = bf(w_in[:, o2:o3].T)
    w["wkva"] = bf(w_in[:, o2:o3])
    w["wpr"] = bf(jnp.concatenate([w_in[:, o3:], jnp.zeros((d, 128 - MLA_ROPE_DIM), F32)], axis=1))
    w["gq"] = _col(q_norm_g, MLA_Q_RANK)
    w["wqb"] = bf(w_qb.T)
    w["gkvc"] = _col(kv_norm_g, MLA_KV_RANK)
    w["gkvr"] = kv_norm_g.astype(F32)[None, :]
    kvb = w_kvb.reshape(MLA_KV_RANK, MLA_HEADS, MLA_NOPE_DIM + MLA_V_DIM)
    w["wv"] = bf(kvb[:, :, MLA_NOPE_DIM:].reshape(MLA_KV_RANK, MLA_HEADS * MLA_V_DIM).T)
    w1 = jnp.concatenate([kvb[:, :, :MLA_NOPE_DIM],
                          jnp.zeros((MLA_KV_RANK, MLA_HEADS, DK - MLA_NOPE_DIM), F32)], axis=2)
    w1 = w1.reshape(MLA_KV_RANK, MLA_HEADS * DK)
    eye = jnp.eye(MLA_ROPE_DIM, dtype=F32)
    place = jnp.zeros((128, DK), F32).at[:MLA_ROPE_DIM, MLA_NOPE_DIM:MLA_QK_DIM].set(eye)
    swap = jnp.roll(eye, MLA_ROPE_DIM // 2, axis=1)
    place_s = jnp.zeros((128, DK), F32).at[:MLA_ROPE_DIM, MLA_NOPE_DIM:MLA_QK_DIM].set(swap)
    w["wk"] = bf(jnp.concatenate([w1, jnp.tile(place, (1, MLA_HEADS)), jnp.tile(place_s, (1, MLA_HEADS))], axis=0))
    w["wout_a"] = bf(w_out[:o1])
    w["wout_b"] = bf(w_out[o1:])
    return w


def _odd_weights(w_qkv, q_norm_g, k_norm_g, w_out):
    bf = lambda a: a.astype(BF16)
    qw = GQA_HEADS * GQA_HEAD_DIM
    kw = GQA_KV_HEADS * GQA_HEAD_DIM
    w = {}
    w["wq"] = bf(w_qkv[:, :qw].T)
    w["wk"] = bf(w_qkv[:, qw:qw + kw].T)
    w["wv"] = bf(w_qkv[:, qw + kw:].T)
    w["gq"] = _col(q_norm_g, GQA_HEAD_DIM)
    w["gk"] = _col(k_norm_g, GQA_HEAD_DIM)
    w["wout_a"] = bf(w_out[:FOURIER_WIDTH])
    w["wout_b"] = bf(w_out[FOURIER_WIDTH:])
    return w


def kernel(x, c, ctx, c_ctx, ada_w, ada_b, norm_mix_g, norm_ffn_g, ev_w_in, ev_q_norm_g, ev_w_qb, ev_kv_norm_g,
           ev_w_kvb, ev_w_out, od_w_qkv, od_q_norm_g, od_k_norm_g, od_w_out, router_w, router_b, exp_w_gate,
           exp_w_up, exp_w_down, final_norm_g):
    B, S, d = x.shape
    C = ctx.shape[1]
    depth = ada_w.shape[0]
    lay = _Layout(B, S, C)
    assert B + 1 <= 16
    tabs = _tables(lay)

    cond = jnp.concatenate([c, c_ctx[None, :], jnp.zeros((16 - B - 1, d), F32)], axis=0)
    mods = _modulation(cond, ada_w, ada_b).reshape(depth, 16, 6, d)

    z = jnp.concatenate([x.reshape(B * S, d), ctx.reshape(B * C, d)], axis=0)

    rwt = router_w.T.astype(F32)
    rwh = rwt.astype(BF16)
    rwl = (rwt - rwh.astype(F32)).astype(BF16)
    rbias = jnp.broadcast_to(router_b.astype(F32)[:, None], (N_EXPERTS, TM))
    rw_nat = jnp.concatenate([router_w, jnp.zeros((d, 128 - N_EXPERTS), F32)], axis=1).astype(BF16)
    wg_all = exp_w_gate.astype(BF16)
    wu_all = exp_w_up.astype(BF16)
    wd_all = exp_w_down.astype(BF16)
    ntile_max = lay.NT * TM // TMM + N_CLASSES

    zin = (z,)
    for i in range(depth):
        mod = mods[i]
        g_mix = norm_mix_g[i][None, :]
        g_ffn = norm_ffn_g[i][None, :]
        j = i // 2
        if i % 2 == 0:
            w = _even_weights(ev_w_in[j], ev_q_norm_g[j], ev_w_qb[j], ev_kv_norm_g[j], ev_w_kvb[j], ev_w_out[j])
            abt, qt, k, vt, *znew = _pre_even(lay, zin, mod, g_mix, w, tabs)
            ot = _attention(lay, qt, k, vt, n_kv=MLA_HEADS, grp=1, dq=DK, kb=8)
            src_a, src_a_ctx = _seq_dft(lay, abt, tabs["cn"], tabs["sn"], tabs["cctx"], tabs["sctx"])
            src_b, blk_a, blk_b = ot, 0, 0
        else:
            w = _odd_weights(od_w_qkv[j], od_q_norm_g[j], od_k_norm_g[j], od_w_out[j])
            qt, k, vt, *znew = _pre_odd(lay, zin, mod, g_mix, w, tabs)
            ot = _attention(lay, qt, k, vt, n_kv=GQA_KV_HEADS, grp=GQA_GROUP, dq=GQA_HEAD_DIM, kb=4)
            src_a, src_a_ctx, src_b, blk_a, blk_b = ot, None, ot, 0, 1
        z = znew[0] if znew else zin[0]
        z1, h2, info, cnt = _post(lay, z, mod, src_a, src_a_ctx, src_b, blk_a, blk_b, w["wout_a"], w["wout_b"],
                                  g_ffn, rwh, rwl, rbias, tabs["ut"])
        pos, ea, eb, nused, nvalid, zfill = _plan(info, cnt, lay.NT, ntile_max)
        tok3 = _invperm(pos, ntile_max * TMM).reshape(ntile_max, 1, TMM)
        xs = _dispatch(lay, zfill, pos.reshape(lay.NT, 1, TM), h2, ntile_max * TMM)
        ys = _experts(lay, i, tok3, xs, ea, eb, nused, nvalid, rw_nat, wg_all, wu_all, wd_all)
        zin = (z1, ys, mod)
    out = _final(lay, zin[0], zin[1], zin[2], final_norm_g[None, :])
    return out.reshape(B, S, d)
```

```python
import functools
import math

import jax
import jax.numpy as jnp
from jax import lax
from jax.experimental import pallas as pl
from jax.experimental.pallas import tpu as pltpu

F32 = jnp.float32
BF16 = jnp.bfloat16

TM = 1024
TQ = 256
TKC = 512
TMM = 256
TN_DFT = 512
EPS = 1e-6
LOG2E = 1.4426950408889634
ROPE_BASE = 10000.0
GRID_W = 64

FOURIER_GROUPS = 4
FOURIER_GROUP_DIM = 128
FOURIER_WIDTH = 512
MLA_HEADS = 8
MLA_NOPE_DIM = 64
MLA_ROPE_DIM = 32
MLA_V_DIM = 64
MLA_QK_DIM = 96
MLA_Q_RANK = 256
MLA_KV_RANK = 128
MLA_SCALE = MLA_QK_DIM ** -0.5
GQA_HEADS = 16
GQA_KV_HEADS = 4
GQA_HEAD_DIM = 64
GQA_GROUP = 4
GQA_SCALE = GQA_HEAD_DIM ** -0.5
N_EXPERTS = 16
N_GROUPS = 4
N_CLASSES = 24
D_EXPERT = 512
DK = 128
DV = 64
VMEM_LIMIT = 56 * 1024 * 1024

_PAIR_LO = (0, 0, 0, 1, 1, 2)
_PAIR_HI = (1, 2, 3, 2, 3, 3)


def _nt(a, b):
    return lax.dot_general(a, b, (((1,), (1,)), ((), ())), preferred_element_type=F32)


def _tn(a, b):
    return lax.dot_general(a, b, (((0,), (0,)), ((), ())), preferred_element_type=F32)


def _dot(a, b):
    return jnp.dot(a, b, preferred_element_type=F32)


class _Layout:
    def __init__(self, B, S, C):
        assert C == TQ and TM % C == 0 and (B * C) % TM == 0 and S % TM == 0 and S % TN_DFT == 0 and S % TKC == 0
        self.B, self.S, self.C = B, S, C
        self.SB = S // TM
        self.NL = B * self.SB
        self.NT = self.NL + B * C // TM
        self.TOKP = self.NT * TM
        self.CB = B * S // C

    def mrow(self, u):
        return jnp.where(u < self.NL, u // self.SB, self.B)

    def ropeblk(self, u):
        return jnp.where(u < self.NL, 1 + u % self.SB, 0)


def _cparams():
    return pltpu.CompilerParams(vmem_limit_bytes=VMEM_LIMIT)


def _full(a):
    return pl.BlockSpec(a.shape, lambda *_: (0,) * a.ndim)


def _mod_kernel(s_ref, w_ref, b_ref, o_ref):
    s = s_ref[...]
    a = s * jax.nn.sigmoid(s)
    ah = a.astype(BF16)
    al = (a - ah.astype(F32)).astype(BF16)
    w = w_ref[0]
    wh = w.astype(BF16)
    wl = (w - wh.astype(F32)).astype(BF16)
    o_ref[0] = _dot(ah, wh) + _dot(ah, wl) + _dot(al, wh) + b_ref[0]


def _modulation(cond, ada_w, ada_b):
    depth, d, w6 = ada_w.shape
    tn = 768
    return pl.pallas_call(
        _mod_kernel,
        grid=(depth, w6 // tn),
        in_specs=[
            pl.BlockSpec((16, d), lambda i, j: (0, 0)),
            pl.BlockSpec((1, d, tn), lambda i, j: (i, 0, j)),
            pl.BlockSpec((1, 1, tn), lambda i, j: (i, 0, j)),
        ],
        out_specs=pl.BlockSpec((1, 16, tn), lambda i, j: (i, 0, j)),
        out_shape=jax.ShapeDtypeStruct((depth, 16, w6), F32),
        compiler_params=_cparams(),
        name="modulation",
    )(cond, ada_w, ada_b.reshape(depth, 1, w6))


def _norm_mod(z, g, sc, sh):
    ms = jnp.mean(z * z, axis=-1, keepdims=True)
    return (z * lax.rsqrt(ms + EPS) * g) * (1.0 + sc) + sh


def _zin_specs(lay, zin, d):
    row = pl.BlockSpec((TM, d), lambda u: (u, 0))
    if len(zin) == 1:
        return [row], [], []
    specs = [row, row, pl.BlockSpec((1, 6, d), lambda u: (lay.mrow(u), 0, 0))]
    return specs, [row], [jax.ShapeDtypeStruct((lay.TOKP, d), F32)]


def _zin_value(zin_refs, zout_refs):
    if len(zin_refs) == 1:
        return zin_refs[0][...]
    z_ref, y_ref, modp_ref = zin_refs
    z = z_ref[...] + modp_ref[0, 5:6, :] * y_ref[...]
    zout_refs[0][...] = z
    return z


def _pre_even_kernel(*refs, n_zin):
    (mod_ref, g_ref, wf_ref, wqa_ref, wkvat_ref, wkva_ref, wpr_ref, gq_ref, wqb_ref, gkvc_ref, gkvr_ref, wv_ref,
     wk_ref, cs_ref, ct_ref, st_ref, t1_ref, t2_ref, ab_ref, q_ref, k_ref, v_ref) = refs[n_zin:n_zin + 22]
    z = _zin_value(refs[:n_zin], refs[n_zin + 22:])
    h = _norm_mod(z, g_ref[...], mod_ref[0, 1:2, :], mod_ref[0, 0:1, :])
    hb = h.astype(BF16)

    ft = _nt(wf_ref[...], hb).astype(BF16)
    gd = FOURIER_GROUP_DIM
    for g in range(FOURIER_GROUPS):
        ab = _dot(cs_ref[...], ft[g * gd:(g + 1) * gd, :])
        ab_ref[g * gd:(g + 1) * gd, :] = ab[:gd].astype(BF16)
        ab_ref[FOURIER_WIDTH + g * gd:FOURIER_WIDTH + (g + 1) * gd, :] = ab[gd:].astype(BF16)

    pq = _nt(wqa_ref[...], hb)
    rs = lax.rsqrt(jnp.mean(pq * pq, axis=0, keepdims=True) + EPS)
    qn = (pq * rs * gq_ref[...]).astype(BF16)
    qt = _dot(wqb_ref[...], qn) * (MLA_SCALE * LOG2E)
    cos = ct_ref[...]
    sin = st_ref[...]
    hr = MLA_ROPE_DIM // 2
    for hd in range(MLA_HEADS):
        o = hd * MLA_QK_DIM
        r = hd * DK
        x1 = qt[o + MLA_NOPE_DIM:o + MLA_NOPE_DIM + hr, :]
        x2 = qt[o + MLA_NOPE_DIM + hr:o + MLA_QK_DIM, :]
        q_ref[r:r + MLA_NOPE_DIM, :] = qt[o:o + MLA_NOPE_DIM, :].astype(BF16)
        q_ref[r + MLA_NOPE_DIM:r + MLA_NOPE_DIM + hr, :] = (x1 * cos - x2 * sin).astype(BF16)
        q_ref[r + MLA_NOPE_DIM + hr:r + MLA_QK_DIM, :] = (x1 * sin + x2 * cos).astype(BF16)
        q_ref[r + MLA_QK_DIM:r + DK, :] = jnp.zeros((DK - MLA_QK_DIM, TM), BF16)

    pkvt = _nt(wkvat_ref[...], hb)
    rst = lax.rsqrt(jnp.mean(pkvt * pkvt, axis=0, keepdims=True) + EPS)
    kvnt = (pkvt * rst * gkvc_ref[...]).astype(BF16)
    v_ref[...] = _dot(wv_ref[...], kvnt).astype(BF16)

    pkv = _dot(hb, wkva_ref[...])
    rsn = lax.rsqrt(jnp.mean(pkv * pkv, axis=-1, keepdims=True) + EPS)
    kvn = (pkv * rsn * gkvr_ref[...]).astype(BF16)
    pr = _dot(hb, wpr_ref[...])
    lhs = jnp.concatenate([kvn, (pr * t1_ref[...]).astype(BF16), (pr * t2_ref[...]).astype(BF16)], axis=1)
    kf = _dot(lhs, wk_ref[...])
    for hd in range(MLA_HEADS):
        k_ref[hd] = kf[:, hd * DK:(hd + 1) * DK].astype(BF16)


def _pre_even(lay, zin, mod, g_mix, w, tabs):
    d = mod.shape[-1]
    tokp = lay.TOKP
    col = lambda rows: pl.BlockSpec((rows, TM), lambda u: (0, u))
    hr = MLA_ROPE_DIM // 2
    zin_specs, zout_specs, zout_shape = _zin_specs(lay, zin, d)
    in_specs = zin_specs + [
        pl.BlockSpec((1, 6, d), lambda u: (lay.mrow(u), 0, 0)),
        _full(g_mix),
        _full(w["wf"]), _full(w["wqa"]), _full(w["wkvat"]), _full(w["wkva"]), _full(w["wpr"]),
        _full(w["gq"]), _full(w["wqb"]), _full(w["gkvc"]), _full(w["gkvr"]), _full(w["wv"]), _full(w["wk"]),
        _full(tabs["cs"]),
        pl.BlockSpec((hr, TM), lambda u: (0, lay.ropeblk(u))),
        pl.BlockSpec((hr, TM), lambda u: (0, lay.ropeblk(u))),
        pl.BlockSpec((TM, 128), lambda u: (lay.ropeblk(u), 0)),
        pl.BlockSpec((TM, 128), lambda u: (lay.ropeblk(u), 0)),
    ]
    out_specs = [
        col(2 * FOURIER_WIDTH),
        col(MLA_HEADS * DK),
        pl.BlockSpec((MLA_HEADS, TM, DK), lambda u: (0, u, 0)),
        col(MLA_HEADS * DV),
    ]
    out_shape = [
        jax.ShapeDtypeStruct((2 * FOURIER_WIDTH, tokp), BF16),
        jax.ShapeDtypeStruct((MLA_HEADS * DK, tokp), BF16),
        jax.ShapeDtypeStruct((MLA_HEADS, tokp, DK), BF16),
        jax.ShapeDtypeStruct((MLA_HEADS * DV, tokp), BF16),
    ]
    return pl.pallas_call(
        functools.partial(_pre_even_kernel, n_zin=len(zin)),
        grid=(lay.NT,), in_specs=in_specs, out_specs=out_specs + zout_specs, out_shape=out_shape + zout_shape,
        compiler_params=_cparams(), name="pre_even",
    )(*zin, mod, g_mix, w["wf"], w["wqa"], w["wkvat"], w["wkva"], w["wpr"], w["gq"], w["wqb"], w["gkvc"],
      w["gkvr"], w["wv"], w["wk"], tabs["cs"], tabs["mla_ct"], tabs["mla_st"], tabs["mla_t1"], tabs["mla_t2"])


def _pre_odd_kernel(*refs, n_zin):
    (mod_ref, g_ref, wq_ref, wk_ref, wv_ref, gq_ref, gk_ref, ct_ref, st_ref,
     q_ref, k_ref, v_ref) = refs[n_zin:n_zin + 12]
    z = _zin_value(refs[:n_zin], refs[n_zin + 12:])
    h = _norm_mod(z, g_ref[...], mod_ref[0, 1:2, :], mod_ref[0, 0:1, :])
    hb = h.astype(BF16)
    cos = ct_ref[...]
    sin = st_ref[...]
    hd2 = GQA_HEAD_DIM // 2

    def norm_rope(xt, gain, scale):
        rs = lax.rsqrt(jnp.mean(xt * xt, axis=0, keepdims=True) + EPS)
        xn = xt * rs * gain
        x1 = xn[:hd2]
        x2 = xn[hd2:]
        return (x1 * cos - x2 * sin) * scale, (x1 * sin + x2 * cos) * scale

    qt = _nt(wq_ref[...], hb)
    for hd in range(GQA_HEADS):
        o = hd * GQA_HEAD_DIM
        r1, r2 = norm_rope(qt[o:o + GQA_HEAD_DIM], gq_ref[...], GQA_SCALE * LOG2E)
        q_ref[o:o + hd2, :] = r1.astype(BF16)
        q_ref[o + hd2:o + GQA_HEAD_DIM, :] = r2.astype(BF16)

    kt = _nt(wk_ref[...], hb)
    zero = jnp.zeros((DK - GQA_HEAD_DIM, TM), F32)
    for hd in range(GQA_KV_HEADS):
        o = hd * GQA_HEAD_DIM
        r1, r2 = norm_rope(kt[o:o + GQA_HEAD_DIM], gk_ref[...], 1.0)
        khead = jnp.concatenate([r1, r2, zero], axis=0)
        k_ref[hd] = khead.T.astype(BF16)

    v_ref[...] = _nt(wv_ref[...], hb).astype(BF16)


def _pre_odd(lay, zin, mod, g_mix, w, tabs):
    d = mod.shape[-1]
    tokp = lay.TOKP
    col = lambda rows: pl.BlockSpec((rows, TM), lambda u: (0, u))
    hd2 = GQA_HEAD_DIM // 2
    zin_specs, zout_specs, zout_shape = _zin_specs(lay, zin, d)
    in_specs = zin_specs + [
        pl.BlockSpec((1, 6, d), lambda u: (lay.mrow(u), 0, 0)),
        _full(g_mix), _full(w["wq"]), _full(w["wk"]), _full(w["wv"]), _full(w["gq"]), _full(w["gk"]),
        pl.BlockSpec((hd2, TM), lambda u: (0, lay.ropeblk(u))),
        pl.BlockSpec((hd2, TM), lambda u: (0, lay.ropeblk(u))),
    ]
    out_specs = [
        col(GQA_HEADS * GQA_HEAD_DIM),
        pl.BlockSpec((GQA_KV_HEADS, TM, DK), lambda u: (0, u, 0)),
        col(GQA_KV_HEADS * DV),
    ]
    out_shape = [
        jax.ShapeDtypeStruct((GQA_HEADS * GQA_HEAD_DIM, tokp), BF16),
        jax.ShapeDtypeStruct((GQA_KV_HEADS, tokp, DK), BF16),
        jax.ShapeDtypeStruct((GQA_KV_HEADS * DV, tokp), BF16),
    ]
    return pl.pallas_call(
        functools.partial(_pre_odd_kernel, n_zin=len(zin)),
        grid=(lay.NT,), in_specs=in_specs, out_specs=out_specs + zout_specs, out_shape=out_shape + zout_shape,
        compiler_params=_cparams(), name="pre_odd",
    )(*zin, mod, g_mix, w["wq"], w["wk"], w["wv"], w["gq"], w["gk"], tabs["gqa_ct"], tabs["gqa_st"])


def _attn_kernel(q_ref, kc_ref, kl_ref, vc_ref, vl_ref, o_ref, s0_ref, s1_ref, *, nq, kb, grp, dq, S, C):
    qi = pl.program_id(2)
    hb = kb * grp
    sbufs = (s0_ref, s1_ref)
    nchunk = S // TKC

    def scores(hh, buf, with_latent):
        kv = hh // grp
        q = q_ref[hh * dq:(hh + 1) * dq, :]
        sc = _dot(kc_ref[kv, :, :dq], q)
        buf[0:C, :] = sc
        m = jnp.max(sc, axis=0, keepdims=True)
        if with_latent:
            for c in range(nchunk):
                s = _dot(kl_ref[kv, c * TKC:(c + 1) * TKC, :dq], q)
                buf[C + c * TKC:C + (c + 1) * TKC, :] = s
                m = jnp.maximum(m, jnp.max(s, axis=0, keepdims=True))
        return m

    def values(hh, buf, m, with_latent):
        kv = hh // grp
        vrows = slice(kv * DV, (kv + 1) * DV)
        p = jnp.exp2(buf[0:C, :] - m)
        l = jnp.sum(p, axis=0, keepdims=True)
        o = _dot(vc_ref[vrows, :], p.astype(BF16))
        if with_latent:
            for c in range(nchunk):
                p = jnp.exp2(buf[C + c * TKC:C + (c + 1) * TKC, :] - m)
                l = l + jnp.sum(p, axis=0, keepdims=True)
                o = o + _dot(vl_ref[vrows, c * TKC:(c + 1) * TKC], p.astype(BF16))
        o_ref[hh * DV:(hh + 1) * DV, :] = (o * (1.0 / l)).astype(o_ref.dtype)

    def run(with_latent):
        m = scores(0, sbufs[0], with_latent)
        for hh in range(hb):
            m_next = scores(hh + 1, sbufs[(hh + 1) % 2], with_latent) if hh + 1 < hb else None
            values(hh, sbufs[hh % 2], m, with_latent)
            m = m_next

    @pl.when(qi < nq)
    def _():
        run(True)

    @pl.when(qi == nq)
    def _():
        run(False)


def _attention(lay, qt, k, vt, *, n_kv, grp, dq, kb):
    B, S, C = lay.B, lay.S, lay.C
    hb = kb * grp
    nq = S // TQ

    def qcol(b, qi):
        return jnp.where(qi == nq, lay.CB + b, b * nq + qi)

    in_specs = [
        pl.BlockSpec((hb * dq, TQ), lambda b, h, qi: (h, qcol(b, qi))),
        pl.BlockSpec((kb, C, DK), lambda b, h, qi: (h, lay.CB + b, 0)),
        pl.BlockSpec((kb, S, DK), lambda b, h, qi: (h, b, 0)),
        pl.BlockSpec((kb * DV, C), lambda b, h, qi: (h, lay.CB + b)),
        pl.BlockSpec((kb * DV, S), lambda b, h, qi: (h, b)),
    ]
    return pl.pallas_call(
        functools.partial(_attn_kernel, nq=nq, kb=kb, grp=grp, dq=dq, S=S, C=C),
        grid=(B, n_kv // kb, nq + 1),
        in_specs=in_specs,
        out_specs=pl.BlockSpec((hb * DV, TQ), lambda b, h, qi: (h, qcol(b, qi))),
        out_shape=jax.ShapeDtypeStruct((n_kv * grp * DV, lay.TOKP), BF16),
        scratch_shapes=[pltpu.VMEM((C + S, TQ), F32), pltpu.VMEM((C + S, TQ), F32)],
        compiler_params=_cparams(),
        name="attention",
    )(qt, k, k, vt, vt)


def _dft_kernel(a_ref, b_ref, c_ref, s_ref, o_ref):
    o_ref[...] = (_dot(a_ref[...], c_ref[...]) - _dot(b_ref[...], s_ref[...])).astype(o_ref.dtype)


def _seq_dft(lay, abt, cn, sn, cc, sc):
    B, S, C = lay.B, lay.S, lay.C
    fw = FOURIER_WIDTH
    nj = S // TN_DFT
    ylat = pl.pallas_call(
        _dft_kernel,
        grid=(nj, B),
        in_specs=[
            pl.BlockSpec((fw, S), lambda j, b: (0, b)),
            pl.BlockSpec((fw, S), lambda j, b: (1, b)),
            pl.BlockSpec((S, TN_DFT), lambda j, b: (0, j)),
            pl.BlockSpec((S, TN_DFT), lambda j, b: (0, j)),
        ],
        out_specs=pl.BlockSpec((fw, TN_DFT), lambda j, b: (0, b * nj + j)),
        out_shape=jax.ShapeDtypeStruct((fw, B * S), BF16),
        compiler_params=_cparams(),
        name="seq_dft",
    )(abt, abt, cn, sn)
    yctx = pl.pallas_call(
        _dft_kernel,
        grid=(B,),
        in_specs=[
            pl.BlockSpec((fw, C), lambda b: (0, lay.CB + b)),
            pl.BlockSpec((fw, C), lambda b: (1, lay.CB + b)),
            pl.BlockSpec((C, C), lambda b: (0, 0)),
            pl.BlockSpec((C, C), lambda b: (0, 0)),
        ],
        out_specs=pl.BlockSpec((fw, C), lambda b: (0, b)),
        out_shape=jax.ShapeDtypeStruct((fw, B * C), BF16),
        compiler_params=_cparams(),
        name="ctx_dft",
    )(abt, abt, cc, sc)
    return ylat, yctx


def _route_rows(lg, rbias):
    s = jax.nn.sigmoid(lg)
    sel = s + rbias
    rows = [sel[e:e + 1, :] for e in range(N_EXPERTS)]
    best = None
    bg = None
    for g in range(N_GROUPS):
        v = rows[4 * g:4 * g + 4]
        sc = None
        for i in range(4):
            for j in range(i + 1, 4):
                ps = v[i] + v[j]
                sc = ps if sc is None else jnp.maximum(sc, ps)
        if best is None:
            best, bg = sc, jnp.zeros_like(sc)
        else:
            upd = sc > best
            bg = jnp.where(upd, float(g), bg)
            best = jnp.where(upd, sc, best)
    v = []
    for i in range(4):
        acc = rows[i]
        for g in range(1, N_GROUPS):
            acc = jnp.where(bg == float(g), rows[4 * g + i], acc)
        v.append(acc)
    i0 = jnp.zeros_like(bg)
    b0 = v[0]
    for i in range(1, 4):
        upd = v[i] > b0
        i0 = jnp.where(upd, float(i), i0)
        b0 = jnp.where(upd, v[i], b0)
    w = [jnp.where(i0 == float(i), -jnp.inf, v[i]) for i in range(4)]
    i1 = jnp.zeros_like(bg)
    b1 = w[0]
    for i in range(1, 4):
        upd = w[i] > b1
        i1 = jnp.where(upd, float(i), i1)
        b1 = jnp.where(upd, w[i], b1)
    lo = jnp.minimum(i0, i1)
    hi = jnp.maximum(i0, i1)
    pbase = jnp.where(lo == 0.0, 0.0, jnp.where(lo == 1.0, 3.0, 5.0))
    return bg * 6.0 + pbase + hi - lo - 1.0


def _post_kernel(z_ref, mod_ref, a_ref, actx_ref, b_ref, wa_ref, wb_ref, g_ref, rwh_ref, rwl_ref, rb_ref, ut_ref,
                 zo_ref, h2_ref, info_ref, cnt_ref, *, nl_split):
    a = a_ref[...]
    if nl_split is not None:
        a = jnp.where(pl.program_id(0) < nl_split, a, actx_ref[...])
    attn = _tn(a, wa_ref[...]) + _tn(b_ref[...], wb_ref[...])
    x1 = z_ref[...] + mod_ref[0, 2:3, :] * attn
    zo_ref[...] = x1
    h = _norm_mod(x1, g_ref[...], mod_ref[0, 4:5, :], mod_ref[0, 3:4, :])
    h2_ref[...] = h
    hh = h.astype(BF16)
    hl = (h - hh.astype(F32)).astype(BF16)
    lg = _nt(rwh_ref[...], hh) + _nt(rwh_ref[...], hl) + _nt(rwl_ref[...], hh)
    cls = _route_rows(lg, rb_ref[...])
    ohf = (lax.broadcasted_iota(jnp.int32, (32, TM), 0).astype(F32) == cls).astype(F32)
    rank_all = _dot(ohf.astype(BF16), ut_ref[...])
    info_ref[0:1, :] = cls
    info_ref[1:2, :] = jnp.sum(ohf * rank_all, axis=0, keepdims=True)
    info_ref[2:8, :] = jnp.zeros((6, TM), F32)
    cnt_ref[0] = jnp.broadcast_to(jnp.sum(ohf, axis=1, keepdims=True), (32, 128))


def _post(lay, z, mod, src_a, src_a_ctx, src_b, blk_a, blk_b, wa, wb, g_ffn, rwh, rwl, rbias, ut):
    d = z.shape[1]
    nt = lay.NT
    fw = FOURIER_WIDTH
    if src_a_ctx is None:
        nl_split = None
        src_a_ctx = jnp.zeros((fw, TM), BF16)
        spec_a = pl.BlockSpec((fw, TM), lambda u: (blk_a, u))
        spec_actx = pl.BlockSpec((fw, TM), lambda u: (0, 0))
    else:
        nl_split = lay.NL
        spec_a = pl.BlockSpec((fw, TM), lambda u: (blk_a, jnp.minimum(u, lay.NL - 1)))
        spec_actx = pl.BlockSpec((fw, TM), lambda u: (0, jnp.maximum(u - lay.NL, 0)))
    in_specs = [
        pl.BlockSpec((TM, d), lambda u: (u, 0)),
        pl.BlockSpec((1, 6, d), lambda u: (lay.mrow(u), 0, 0)),
        spec_a,
        spec_actx,
        pl.BlockSpec((fw, TM), lambda u: (blk_b, u)),
        _full(wa), _full(wb), _full(g_ffn), _full(rwh), _full(rwl), _full(rbias), _full(ut),
    ]
    out_specs = [
        pl.BlockSpec((TM, d), lambda u: (u, 0)),
        pl.BlockSpec((TM, d), lambda u: (u, 0)),
        pl.BlockSpec((8, TM), lambda u: (0, u)),
        pl.BlockSpec((1, 32, 128), lambda u: (u, 0, 0)),
    ]
    out_shape = [
        jax.ShapeDtypeStruct((lay.TOKP, d), F32),
        jax.ShapeDtypeStruct((lay.TOKP, d), F32),
        jax.ShapeDtypeStruct((8, lay.TOKP), F32),
        jax.ShapeDtypeStruct((nt, 32, 128), F32),
    ]
    return pl.pallas_call(
        functools.partial(_post_kernel, nl_split=nl_split),
        grid=(nt,), in_specs=in_specs, out_specs=out_specs, out_shape=out_shape,
        compiler_params=_cparams(), name="post",
    )(z, mod, src_a, src_a_ctx, src_b, wa, wb, g_ffn, rwh, rwl, rbias, ut)


def _invperm_kernel(pos_ref, lo_ref, hi_ref, tok_ref):
    def clear(p, c):
        tok_ref[p] = 0
        return c
    for k in range(lo_ref.shape[0]):
        lax.fori_loop(lo_ref[k], hi_ref[k], clear, 0)

    def put(t, c):
        tok_ref[pos_ref[t]] = t
        return c
    lax.fori_loop(0, pos_ref.shape[0], put, 0, unroll=16)


def _invperm(pos, pad_lo, pad_hi, pmax):
    smem = pl.BlockSpec(memory_space=pltpu.SMEM)
    return pl.pallas_call(
        _invperm_kernel,
        in_specs=[smem, smem, smem],
        out_specs=smem,
        out_shape=jax.ShapeDtypeStruct((pmax,), jnp.int32),
        name="moe_invperm",
    )(pos, pad_lo, pad_hi)


def _dispatch_kernel(zf_ref, pos_ref, h_ref, xs_ref, zbuf, zsem, sem):
    u = pl.program_id(0)
    ntile = zf_ref.shape[0]

    def fill(i):
        return pltpu.make_async_copy(zbuf, xs_ref.at[pl.ds(pl.multiple_of(i * TMM, TMM), TMM)], zsem)

    @pl.when(u == 0)
    def _():
        zbuf[...] = jnp.zeros(zbuf.shape, F32)

        def start(i, c):
            @pl.when(zf_ref[i] != 0)
            def _():
                fill(i).start()
            return c

        def wait(i, c):
            @pl.when(zf_ref[i] != 0)
            def _():
                fill(i).wait()
            return c

        lax.fori_loop(0, ntile, start, 0)
        lax.fori_loop(0, ntile, wait, 0)

    for r in range(TM):
        pltpu.make_async_copy(h_ref.at[pl.ds(r, 1)], xs_ref.at[pl.ds(pos_ref[0, 0, r], 1)], sem).start(priority=r % 2)
    pltpu.make_async_copy(h_ref, xs_ref.at[pl.ds(0, TM)], sem).wait()


def _dispatch(lay, zfill, pos3, h2, pmax):
    d = h2.shape[1]
    grid_spec = pltpu.PrefetchScalarGridSpec(
        num_scalar_prefetch=1,
        grid=(lay.NT,),
        in_specs=[
            pl.BlockSpec((1, 1, TM), lambda u, *_: (u, 0, 0), memory_space=pltpu.SMEM),
            pl.BlockSpec((TM, d), lambda u, *_: (u, 0)),
        ],
        out_specs=pl.BlockSpec(memory_space=pl.ANY),
        scratch_shapes=[pltpu.VMEM((TMM, d), F32), pltpu.SemaphoreType.DMA(()), pltpu.SemaphoreType.DMA(())],
    )
    return pl.pallas_call(
        _dispatch_kernel, grid_spec=grid_spec,
        out_shape=jax.ShapeDtypeStruct((pmax, d), F32),
        compiler_params=_cparams(), name="moe_dispatch",
    )(zfill, pos3, h2)


def _expert_kernel(ea_ref, eb_ref, nu_ref, nv_ref, tok_ref, x_ref, rw_ref, wga_ref, wua_ref, wda_ref,
                   wgb_ref, wub_ref, wdb_ref, ys_ref, ybuf, ssem):
    i = pl.program_id(0)
    nu = nu_ref[0]
    ntok = ys_ref.shape[0] - TMM

    def scatter_start(n):
        for r in range(TMM):
            dst = jnp.where(r < n, tok_ref[0, 0, r], ntok + r)
            pltpu.make_async_copy(ybuf.at[pl.ds(r, 1)], ys_ref.at[pl.ds(dst, 1)], ssem).start(priority=r % 2)

    def scatter_wait():
        pltpu.make_async_copy(ybuf, ys_ref.at[pl.ds(0, TMM)], ssem).wait()

    @pl.when(i < nu)
    def _():
        @pl.when(i == 0)
        def _():
            ybuf[...] = jnp.zeros(ybuf.shape, F32)
            spare = pltpu.make_async_copy(ybuf, ys_ref.at[pl.ds(ntok, TMM)], ssem)
            spare.start()
            spare.wait()

        xb = x_ref[...].astype(BF16)
        s = jax.nn.sigmoid(_dot(xb, rw_ref[...]))
        lane = lax.broadcasted_iota(jnp.int32, s.shape, 1)
        sa = jnp.sum(jnp.where(lane == ea_ref[i], s, 0.0), axis=1, keepdims=True)
        sb = jnp.sum(jnp.where(lane == eb_ref[i], s, 0.0), axis=1, keepdims=True)
        inv = 1.0 / (sa + sb)

        def hidden(wg_ref, wu_ref, gate):
            g = _dot(xb, wg_ref[0, 0])
            u = _dot(xb, wu_ref[0, 0])
            return (g * jax.nn.sigmoid(g) * u * gate).astype(BF16)

        y = (_dot(hidden(wga_ref, wua_ref, sa * inv), wda_ref[0, 0])
             + _dot(hidden(wgb_ref, wub_ref, sb * inv), wdb_ref[0, 0]))

        @pl.when(i > 0)
        def _():
            scatter_wait()

        ybuf[...] = y
        scatter_start(nv_ref[i])

        @pl.when(i == nu - 1)
        def _():
            scatter_wait()


def _experts(lay, layer, tok3, xs, ea, eb, nused, nvalid, rw, wg, wu, wd):
    ntile = tok3.shape[0]
    d = wd.shape[3]
    w_in = lambda sel: pl.BlockSpec((1, 1, d, D_EXPERT), lambda i, ea, eb, nu, nv: (layer, sel(ea, eb)[i], 0, 0))
    w_out = lambda sel: pl.BlockSpec((1, 1, D_EXPERT, d), lambda i, ea, eb, nu, nv: (layer, sel(ea, eb)[i], 0, 0))
    first = lambda a, b: a
    second = lambda a, b: b
    grid_spec = pltpu.PrefetchScalarGridSpec(
        num_scalar_prefetch=4,
        grid=(ntile,),
        in_specs=[
            pl.BlockSpec((1, 1, TMM), lambda i, *_: (i, 0, 0), memory_space=pltpu.SMEM),
            pl.BlockSpec((TMM, d), lambda i, ea, eb, nu, nv: (jnp.minimum(i, nu[0] - 1), 0)),
            pl.BlockSpec(rw.shape, lambda i, *_: (0, 0)),
            w_in(first), w_in(first), w_out(first),
            w_in(second), w_in(second), w_out(second),
        ],
        out_specs=pl.BlockSpec(memory_space=pl.ANY),
        scratch_shapes=[pltpu.VMEM((TMM, d), F32), pltpu.SemaphoreType.DMA(())],
    )
    return pl.pallas_call(
        _expert_kernel, grid_spec=grid_spec,
        out_shape=jax.ShapeDtypeStruct((lay.TOKP + TMM, d), F32),
        compiler_params=_cparams(), name="moe_experts",
    )(ea, eb, nused, nvalid, tok3, xs, rw, wg, wu, wd, wg, wu, wd)


def _final_kernel(z_ref, y_ref, mod_ref, gf_ref, o_ref):
    x2 = z_ref[...] + mod_ref[0, 5:6, :] * y_ref[...]
    ms = jnp.mean(x2 * x2, axis=-1, keepdims=True)
    o_ref[...] = x2 * lax.rsqrt(ms + EPS) * gf_ref[...]


def _final(lay, z, ys, mod, g_final):
    d = z.shape[1]
    return pl.pallas_call(
        _final_kernel,
        grid=(lay.NL,),
        in_specs=[
            pl.BlockSpec((TM, d), lambda u: (u, 0)),
            pl.BlockSpec((TM, d), lambda u: (u, 0)),
            pl.BlockSpec((1, 6, d), lambda u: (lay.mrow(u), 0, 0)),
            _full(g_final),
        ],
        out_specs=pl.BlockSpec((TM, d), lambda u: (u, 0)),
        out_shape=jax.ShapeDtypeStruct((lay.NL * TM, d), F32),
        compiler_params=_cparams(), name="final_norm",
    )(z, ys, mod, g_final)


def _plan(info, cnt, nt, ntile_max):
    cls = info[0].astype(jnp.int32).reshape(nt, TM)
    rank = info[1].astype(jnp.int32).reshape(nt, TM)
    cnt = cnt[:, :N_CLASSES, 0].astype(jnp.int32)
    tot = jnp.sum(cnt, axis=0)
    ntile_c = (tot + TMM - 1) // TMM
    tile_end = jnp.cumsum(ntile_c)
    class_off = (tile_end - ntile_c) * TMM
    tile_off = jnp.cumsum(cnt, axis=0) - cnt
    base = class_off[None, :] + tile_off
    onehot = cls[:, :, None] == jnp.arange(N_CLASSES, dtype=jnp.int32)[None, None, :]
    pos = jnp.sum(jnp.where(onehot, base[:, None, :], 0), axis=-1) + rank
    nused = tile_end[-1]
    ti = jnp.minimum(jnp.arange(ntile_max, dtype=jnp.int32), nused - 1)
    tcls = jnp.sum((ti[:, None] >= tile_end[None, :]).astype(jnp.int32), axis=1)
    onec = tcls[:, None] == jnp.arange(N_CLASSES, dtype=jnp.int32)[None, :]
    class_end = jnp.sum(jnp.where(onec, (class_off + tot)[None, :], 0), axis=1)
    nvalid = jnp.clip(class_end - ti * TMM, 0, TMM)
    tiles = jnp.arange(ntile_max, dtype=jnp.int32)
    zfill = ((tiles >= nused) | (nvalid < TMM)).astype(jnp.int32)
    grp, pair = tcls // 6, tcls % 6
    lo = jnp.asarray(_PAIR_LO, jnp.int32)
    hi = jnp.asarray(_PAIR_HI, jnp.int32)
    onep = pair[:, None] == jnp.arange(6, dtype=jnp.int32)[None, :]
    ea = 4 * grp + jnp.sum(jnp.where(onep, lo[None, :], 0), axis=1)
    eb = 4 * grp + jnp.sum(jnp.where(onep, hi[None, :], 0), axis=1)
    pad_lo = jnp.concatenate([class_off + tot, (nused * TMM).reshape(1)])
    pad_hi = jnp.concatenate([tile_end * TMM, jnp.full((1,), ntile_max * TMM, jnp.int32)])
    return pos.reshape(nt * TM), ea, eb, nused.reshape(1), nvalid, zfill, pad_lo, pad_hi


def _dft_mats(n, scale):
    r = 64 if (n % 64 == 0 and n > 64) else 1
    k = lax.broadcasted_iota(jnp.int32, (1, n), 1)

    def cs(rows, step):
        j = lax.broadcasted_iota(jnp.int32, (rows, 1), 0) * step
        ang = ((j * k) % n).astype(F32) * (2.0 * math.pi / n)
        return jnp.cos(ang), jnp.sin(ang)

    ca, sa = cs(n // r, r)
    if r == 1:
        c, s = ca, sa
    else:
        cb, sb = cs(r, 1)
        c = (ca[:, None, :] * cb[None, :, :] - sa[:, None, :] * sb[None, :, :]).reshape(n, n)
        s = (sa[:, None, :] * cb[None, :, :] + ca[:, None, :] * sb[None, :, :]).reshape(n, n)
    return (c * scale).astype(BF16), (s * scale).astype(BF16)


def _rope_angles(S, dim):
    rows = S // GRID_W
    row_id = jnp.repeat(jnp.arange(rows, dtype=F32), GRID_W)
    col_id = jnp.tile(jnp.arange(GRID_W, dtype=F32), rows)
    n_freq = dim // 4
    inv = ROPE_BASE ** (-jnp.arange(n_freq, dtype=F32) / n_freq)
    return jnp.concatenate([row_id[:, None] * inv, col_id[:, None] * inv], axis=-1)


def _tables(lay):
    S, C = lay.S, lay.C
    tabs = {}
    cc, sc = _dft_mats(FOURIER_GROUP_DIM, FOURIER_GROUP_DIM ** -0.5)
    tabs["cs"] = jnp.concatenate([cc, sc], axis=0)
    tabs["cn"], tabs["sn"] = _dft_mats(S, S ** -0.5)
    tabs["cctx"], tabs["sctx"] = _dft_mats(C, C ** -0.5)
    for name, dim in (("mla", MLA_ROPE_DIM), ("gqa", GQA_HEAD_DIM)):
        ang = _rope_angles(S, dim)
        cos = jnp.concatenate([jnp.ones((TM, dim // 2), F32), jnp.cos(ang)], axis=0)
        sin = jnp.concatenate([jnp.zeros((TM, dim // 2), F32), jnp.sin(ang)], axis=0)
        tabs[name + "_ct"] = cos.T
        tabs[name + "_st"] = sin.T
        if name == "mla":
            pad = jnp.zeros((TM + S, 128 - dim), F32)
            tabs["mla_t1"] = jnp.concatenate([cos, cos, pad], axis=1)
            tabs["mla_t2"] = jnp.concatenate([sin, -sin, pad], axis=1)
    tabs["ut"] = (lax.broadcasted_iota(jnp.int32, (TM, TM), 0)
                  < lax.broadcasted_iota(jnp.int32, (TM, TM), 1)).astype(BF16)
    return tabs


def _col(g, rows):
    return jnp.broadcast_to(g.astype(F32)[:, None], (rows, TM))


def _even_weights(w_in, q_norm_g, w_qb, kv_norm_g, w_kvb, w_out):
    d = w_in.shape[0]
    o1 = FOURIER_WIDTH
    o2 = o1 + MLA_Q_RANK
    o3 = o2 + MLA_KV_RANK
    bf = lambda a: a.astype(BF16)
    w = {}
    w["wf"] = bf(w_in[:, :o1].T)
    w["wqa"] = bf(w_in[:, o1:o2].T)
    w["wkvat"] = bf(w_in[:, o2:o3].T)
    w["wkva"] = bf(w_in[:, o2:o3])
    w["wpr"] = bf(jnp.concatenate([w_in[:, o3:], jnp.zeros((d, 128 - MLA_ROPE_DIM), F32)], axis=1))
    w["gq"] = _col(q_norm_g, MLA_Q_RANK)
    w["wqb"] = bf(w_qb.T)
    w["gkvc"] = _col(kv_norm_g, MLA_KV_RANK)
    w["gkvr"] = kv_norm_g.astype(F32)[None, :]
    kvb = w_kvb.reshape(MLA_KV_RANK, MLA_HEADS, MLA_NOPE_DIM + MLA_V_DIM)
    w["wv"] = bf(kvb[:, :, MLA_NOPE_DIM:].reshape(MLA_KV_RANK, MLA_HEADS * MLA_V_DIM).T)
    w1 = jnp.concatenate([kvb[:, :, :MLA_NOPE_DIM],
                          jnp.zeros((MLA_KV_RANK, MLA_HEADS, DK - MLA_NOPE_DIM), F32)], axis=2)
    w1 = w1.reshape(MLA_KV_RANK, MLA_HEADS * DK)
    eye = jnp.eye(MLA_ROPE_DIM, dtype=F32)
    place = jnp.zeros((128, DK), F32).at[:MLA_ROPE_DIM, MLA_NOPE_DIM:MLA_QK_DIM].set(eye)
    swap = jnp.roll(eye, MLA_ROPE_DIM // 2, axis=1)
    place_s = jnp.zeros((128, DK), F32).at[:MLA_ROPE_DIM, MLA_NOPE_DIM:MLA_QK_DIM].set(swap)
    w["wk"] = bf(jnp.concatenate([w1, jnp.tile(place, (1, MLA_HEADS)), jnp.tile(place_s, (1, MLA_HEADS))], axis=0))
    w["wout_a"] = bf(w_out[:o1])
    w["wout_b"] = bf(w_out[o1:])
    return w


def _odd_weights(w_qkv, q_norm_g, k_norm_g, w_out):
    bf = lambda a: a.astype(BF16)
    qw = GQA_HEADS * GQA_HEAD_DIM
    kw = GQA_KV_HEADS * GQA_HEAD_DIM
    w = {}
    w["wq"] = bf(w_qkv[:, :qw].T)
    w["wk"] = bf(w_qkv[:, qw:qw + kw].T)
    w["wv"] = bf(w_qkv[:, qw + kw:].T)
    w["gq"] = _col(q_norm_g, GQA_HEAD_DIM)
    w["gk"] = _col(k_norm_g, GQA_HEAD_DIM)
    w["wout_a"] = bf(w_out[:FOURIER_WIDTH])
    w["wout_b"] = bf(w_out[FOURIER_WIDTH:])
    return w


def kernel(x, c, ctx, c_ctx, ada_w, ada_b, norm_mix_g, norm_ffn_g, ev_w_in, ev_q_norm_g, ev_w_qb, ev_kv_norm_g,
           ev_w_kvb, ev_w_out, od_w_qkv, od_q_norm_g, od_k_norm_g, od_w_out, router_w, router_b, exp_w_gate,
           exp_w_up, exp_w_down, final_norm_g):
    B, S, d = x.shape
    C = ctx.shape[1]
    depth = ada_w.shape[0]
    lay = _Layout(B, S, C)
    assert B + 1 <= 16
    tabs = _tables(lay)

    cond = jnp.concatenate([c, c_ctx[None, :], jnp.zeros((16 - B - 1, d), F32)], axis=0)
    mods = _modulation(cond, ada_w, ada_b).reshape(depth, 16, 6, d)

    z = jnp.concatenate([x.reshape(B * S, d), ctx.reshape(B * C, d)], axis=0)

    rwt = router_w.T.astype(F32)
    rwh = rwt.astype(BF16)
    rwl = (rwt - rwh.astype(F32)).astype(BF16)
    rbias = jnp.broadcast_to(router_b.astype(F32)[:, None], (N_EXPERTS, TM))
    rw_nat = jnp.concatenate([router_w, jnp.zeros((d, 128 - N_EXPERTS), F32)], axis=1).astype(BF16)
    wg_all = exp_w_gate.astype(BF16)
    wu_all = exp_w_up.astype(BF16)
    wd_all = exp_w_down.astype(BF16)
    ntile_max = lay.NT * TM // TMM + N_CLASSES

    zin = (z,)
    for i in range(depth):
        mod = mods[i]
        g_mix = norm_mix_g[i][None, :]
        g_ffn = norm_ffn_g[i][None, :]
        j = i // 2
        if i % 2 == 0:
            w = _even_weights(ev_w_in[j], ev_q_norm_g[j], ev_w_qb[j], ev_kv_norm_g[j], ev_w_kvb[j], ev_w_out[j])
            abt, qt, k, vt, *znew = _pre_even(lay, zin, mod, g_mix, w, tabs)
            ot = _attention(lay, qt, k, vt, n_kv=MLA_HEADS, grp=1, dq=DK, kb=8)
            src_a, src_a_ctx = _seq_dft(lay, abt, tabs["cn"], tabs["sn"], tabs["cctx"], tabs["sctx"])
            src_b, blk_a, blk_b = ot, 0, 0
        else:
            w = _odd_weights(od_w_qkv[j], od_q_norm_g[j], od_k_norm_g[j], od_w_out[j])
            qt, k, vt, *znew = _pre_odd(lay, zin, mod, g_mix, w, tabs)
            ot = _attention(lay, qt, k, vt, n_kv=GQA_KV_HEADS, grp=GQA_GROUP, dq=GQA_HEAD_DIM, kb=4)
            src_a, src_a_ctx, src_b, blk_a, blk_b = ot, None, ot, 0, 1
        z = znew[0] if znew else zin[0]
        z1, h2, info, cnt = _post(lay, z, mod, src_a, src_a_ctx, src_b, blk_a, blk_b, w["wout_a"], w["wout_b"],
                                  g_ffn, rwh, rwl, rbias, tabs["ut"])
        pos, ea, eb, nused, nvalid, zfill, pad_lo, pad_hi = _plan(info, cnt, lay.NT, ntile_max)
        tok3 = _invperm(pos, pad_lo, pad_hi, ntile_max * TMM).reshape(ntile_max, 1, TMM)
        xs = _dispatch(lay, zfill, pos.reshape(lay.NT, 1, TM), h2, ntile_max * TMM)
        ys = _experts(lay, i, tok3, xs, ea, eb, nused, nvalid, rw_nat, wg_all, wu_all, wd_all)
        zin = (z1, ys, mod)
    out = _final(lay, zin[0], zin[1], zin[2], final_norm_g[None, :])
    return out.reshape(B, S, d)
```

```python
import functools
import math

import jax
import jax.numpy as jnp
from jax import lax
from jax.experimental import pallas as pl
from jax.experimental.pallas import tpu as pltpu

F32 = jnp.float32
BF16 = jnp.bfloat16

TM = 1024
TQ = 256
TKC = 512
TMM = 256
TN_DFT = 512
EPS = 1e-6
LOG2E = 1.4426950408889634
ROPE_BASE = 10000.0
GRID_W = 64

FOURIER_GROUPS = 4
FOURIER_GROUP_DIM = 128
FOURIER_WIDTH = 512
MLA_HEADS = 8
MLA_NOPE_DIM = 64
MLA_ROPE_DIM = 32
MLA_V_DIM = 64
MLA_QK_DIM = 96
MLA_Q_RANK = 256
MLA_KV_RANK = 128
MLA_SCALE = MLA_QK_DIM ** -0.5
GQA_HEADS = 16
GQA_KV_HEADS = 4
GQA_HEAD_DIM = 64
GQA_GROUP = 4
GQA_SCALE = GQA_HEAD_DIM ** -0.5
N_EXPERTS = 16
N_GROUPS = 4
N_CLASSES = 24
D_EXPERT = 512
DK = 128
DV = 64
VMEM_LIMIT = 56 * 1024 * 1024

_PAIR_LO = (0, 0, 0, 1, 1, 2)
_PAIR_HI = (1, 2, 3, 2, 3, 3)


def _nt(a, b):
    return lax.dot_general(a, b, (((1,), (1,)), ((), ())), preferred_element_type=F32)


def _tn(a, b):
    return lax.dot_general(a, b, (((0,), (0,)), ((), ())), preferred_element_type=F32)


def _dot(a, b):
    return jnp.dot(a, b, preferred_element_type=F32)


class _Layout:
    def __init__(self, B, S, C):
        assert C == TQ and TM % C == 0 and (B * C) % TM == 0 and S % TM == 0 and S % TN_DFT == 0 and S % TKC == 0
        self.B, self.S, self.C = B, S, C
        self.SB = S // TM
        self.NL = B * self.SB
        self.NT = self.NL + B * C // TM
        self.TOKP = self.NT * TM
        self.CB = B * S // C

    def mrow(self, u):
        return jnp.where(u < self.NL, u // self.SB, self.B)

    def ropeblk(self, u):
        return jnp.where(u < self.NL, 1 + u % self.SB, 0)


def _cparams():
    return pltpu.CompilerParams(vmem_limit_bytes=VMEM_LIMIT)


def _full(a):
    return pl.BlockSpec(a.shape, lambda *_: (0,) * a.ndim)


def _mod_kernel(s_ref, w_ref, b_ref, o_ref):
    s = s_ref[...]
    a = s * jax.nn.sigmoid(s)
    ah = a.astype(BF16)
    al = (a - ah.astype(F32)).astype(BF16)
    w = w_ref[0]
    wh = w.astype(BF16)
    wl = (w - wh.astype(F32)).astype(BF16)
    o_ref[0] = _dot(ah, wh) + _dot(ah, wl) + _dot(al, wh) + b_ref[0]


def _modulation(cond, ada_w, ada_b):
    depth, d, w6 = ada_w.shape
    tn = 768
    return pl.pallas_call(
        _mod_kernel,
        grid=(depth, w6 // tn),
        in_specs=[
            pl.BlockSpec((16, d), lambda i, j: (0, 0)),
            pl.BlockSpec((1, d, tn), lambda i, j: (i, 0, j)),
            pl.BlockSpec((1, 1, tn), lambda i, j: (i, 0, j)),
        ],
        out_specs=pl.BlockSpec((1, 16, tn), lambda i, j: (i, 0, j)),
        out_shape=jax.ShapeDtypeStruct((depth, 16, w6), F32),
        compiler_params=_cparams(),
        name="modulation",
    )(cond, ada_w, ada_b.reshape(depth, 1, w6))


def _norm_mod(z, g, sc, sh):
    ms = jnp.mean(z * z, axis=-1, keepdims=True)
    return (z * lax.rsqrt(ms + EPS) * g) * (1.0 + sc) + sh


def _zin_specs(lay, zin, d):
    row = pl.BlockSpec((TM, d), lambda u: (u, 0))
    zout = [row], [jax.ShapeDtypeStruct((lay.TOKP, d), F32)]
    if len(zin) == 2:
        lat = pl.BlockSpec((TM, d), lambda u: (jnp.minimum(u, lay.NL - 1), 0))
        ctx = pl.BlockSpec((TM, d), lambda u: (jnp.maximum(u - lay.NL, 0), 0))
        return [lat, ctx], *zout
    return [row, row, pl.BlockSpec((1, 6, d), lambda u: (lay.mrow(u), 0, 0))], *zout


def _zin_value(zin_refs, zout_refs, nl):
    zo_ref = zout_refs[0]
    if len(zin_refs) == 2:
        lat_ref, ctx_ref = zin_refs

        @pl.when(pl.program_id(0) < nl)
        def _():
            zo_ref[...] = lat_ref[...]

        @pl.when(pl.program_id(0) >= nl)
        def _():
            zo_ref[...] = ctx_ref[...]

        return zo_ref[...]
    z_ref, y_ref, modp_ref = zin_refs
    z = z_ref[...] + modp_ref[0, 5:6, :] * y_ref[...]
    zo_ref[...] = z
    return z


def _pre_even_kernel(*refs, n_zin, nl):
    (mod_ref, g_ref, wf_ref, wqa_ref, wkvat_ref, wkva_ref, wpr_ref, gq_ref, wqb_ref, gkvc_ref, gkvr_ref, wv_ref,
     wk_ref, cs_ref, ct_ref, st_ref, t1_ref, t2_ref, ab_ref, q_ref, k_ref, v_ref) = refs[n_zin:n_zin + 22]
    z = _zin_value(refs[:n_zin], refs[n_zin + 22:], nl)
    h = _norm_mod(z, g_ref[...], mod_ref[0, 1:2, :], mod_ref[0, 0:1, :])
    hb = h.astype(BF16)

    ft = _nt(wf_ref[...], hb).astype(BF16)
    gd = FOURIER_GROUP_DIM
    for g in range(FOURIER_GROUPS):
        ab = _dot(cs_ref[...], ft[g * gd:(g + 1) * gd, :])
        ab_ref[g * gd:(g + 1) * gd, :] = ab[:gd].astype(BF16)
        ab_ref[FOURIER_WIDTH + g * gd:FOURIER_WIDTH + (g + 1) * gd, :] = ab[gd:].astype(BF16)

    pq = _nt(wqa_ref[...], hb)
    rs = lax.rsqrt(jnp.mean(pq * pq, axis=0, keepdims=True) + EPS)
    qn = (pq * rs * gq_ref[...]).astype(BF16)
    qt = _dot(wqb_ref[...], qn) * (MLA_SCALE * LOG2E)
    cos = ct_ref[...]
    sin = st_ref[...]
    hr = MLA_ROPE_DIM // 2
    for hd in range(MLA_HEADS):
        o = hd * MLA_QK_DIM
        r = hd * DK
        x1 = qt[o + MLA_NOPE_DIM:o + MLA_NOPE_DIM + hr, :]
        x2 = qt[o + MLA_NOPE_DIM + hr:o + MLA_QK_DIM, :]
        q_ref[r:r + MLA_NOPE_DIM, :] = qt[o:o + MLA_NOPE_DIM, :].astype(BF16)
        q_ref[r + MLA_NOPE_DIM:r + MLA_NOPE_DIM + hr, :] = (x1 * cos - x2 * sin).astype(BF16)
        q_ref[r + MLA_NOPE_DIM + hr:r + MLA_QK_DIM, :] = (x1 * sin + x2 * cos).astype(BF16)
        q_ref[r + MLA_QK_DIM:r + DK, :] = jnp.zeros((DK - MLA_QK_DIM, TM), BF16)

    pkvt = _nt(wkvat_ref[...], hb)
    rst = lax.rsqrt(jnp.mean(pkvt * pkvt, axis=0, keepdims=True) + EPS)
    kvnt = (pkvt * rst * gkvc_ref[...]).astype(BF16)
    v_ref[...] = _dot(wv_ref[...], kvnt).astype(BF16)

    pkv = _dot(hb, wkva_ref[...])
    rsn = lax.rsqrt(jnp.mean(pkv * pkv, axis=-1, keepdims=True) + EPS)
    kvn = (pkv * rsn * gkvr_ref[...]).astype(BF16)
    pr = _dot(hb, wpr_ref[...])
    lhs = jnp.concatenate([kvn, (pr * t1_ref[...]).astype(BF16), (pr * t2_ref[...]).astype(BF16)], axis=1)
    kf = _dot(lhs, wk_ref[...])
    for hd in range(MLA_HEADS):
        k_ref[hd] = kf[:, hd * DK:(hd + 1) * DK].astype(BF16)


def _pre_even(lay, zin, mod, g_mix, w, tabs):
    d = mod.shape[-1]
    tokp = lay.TOKP
    col = lambda rows: pl.BlockSpec((rows, TM), lambda u: (0, u))
    hr = MLA_ROPE_DIM // 2
    zin_specs, zout_specs, zout_shape = _zin_specs(lay, zin, d)
    in_specs = zin_specs + [
        pl.BlockSpec((1, 6, d), lambda u: (lay.mrow(u), 0, 0)),
        _full(g_mix),
        _full(w["wf"]), _full(w["wqa"]), _full(w["wkvat"]), _full(w["wkva"]), _full(w["wpr"]),
        _full(w["gq"]), _full(w["wqb"]), _full(w["gkvc"]), _full(w["gkvr"]), _full(w["wv"]), _full(w["wk"]),
        _full(tabs["cs"]),
        pl.BlockSpec((hr, TM), lambda u: (0, lay.ropeblk(u))),
        pl.BlockSpec((hr, TM), lambda u: (0, lay.ropeblk(u))),
        pl.BlockSpec((TM, 128), lambda u: (lay.ropeblk(u), 0)),
        pl.BlockSpec((TM, 128), lambda u: (lay.ropeblk(u), 0)),
    ]
    out_specs = [
        col(2 * FOURIER_WIDTH),
        col(MLA_HEADS * DK),
        pl.BlockSpec((MLA_HEADS, TM, DK), lambda u: (0, u, 0)),
        col(MLA_HEADS * DV),
    ]
    out_shape = [
        jax.ShapeDtypeStruct((2 * FOURIER_WIDTH, tokp), BF16),
        jax.ShapeDtypeStruct((MLA_HEADS * DK, tokp), BF16),
        jax.ShapeDtypeStruct((MLA_HEADS, tokp, DK), BF16),
        jax.ShapeDtypeStruct((MLA_HEADS * DV, tokp), BF16),
    ]
    return pl.pallas_call(
        functools.partial(_pre_even_kernel, n_zin=len(zin), nl=lay.NL),
        grid=(lay.NT,), in_specs=in_specs, out_specs=out_specs + zout_specs, out_shape=out_shape + zout_shape,
        compiler_params=_cparams(), name="pre_even",
    )(*zin, mod, g_mix, w["wf"], w["wqa"], w["wkvat"], w["wkva"], w["wpr"], w["gq"], w["wqb"], w["gkvc"],
      w["gkvr"], w["wv"], w["wk"], tabs["cs"], tabs["mla_ct"], tabs["mla_st"], tabs["mla_t1"], tabs["mla_t2"])


def _pre_odd_kernel(*refs, n_zin, nl):
    (mod_ref, g_ref, wq_ref, wk_ref, wv_ref, gq_ref, gk_ref, ct_ref, st_ref,
     q_ref, k_ref, v_ref) = refs[n_zin:n_zin + 12]
    z = _zin_value(refs[:n_zin], refs[n_zin + 12:], nl)
    h = _norm_mod(z, g_ref[...], mod_ref[0, 1:2, :], mod_ref[0, 0:1, :])
    hb = h.astype(BF16)
    cos = ct_ref[...]
    sin = st_ref[...]
    hd2 = GQA_HEAD_DIM // 2

    def norm_rope(xt, gain, scale):
        rs = lax.rsqrt(jnp.mean(xt * xt, axis=0, keepdims=True) + EPS)
        xn = xt * rs * gain
        x1 = xn[:hd2]
        x2 = xn[hd2:]
        return (x1 * cos - x2 * sin) * scale, (x1 * sin + x2 * cos) * scale

    qt = _nt(wq_ref[...], hb)
    for hd in range(GQA_HEADS):
        o = hd * GQA_HEAD_DIM
        r1, r2 = norm_rope(qt[o:o + GQA_HEAD_DIM], gq_ref[...], GQA_SCALE * LOG2E)
        q_ref[o:o + hd2, :] = r1.astype(BF16)
        q_ref[o + hd2:o + GQA_HEAD_DIM, :] = r2.astype(BF16)

    kt = _nt(wk_ref[...], hb)
    zero = jnp.zeros((DK - GQA_HEAD_DIM, TM), F32)
    for hd in range(GQA_KV_HEADS):
        o = hd * GQA_HEAD_DIM
        r1, r2 = norm_rope(kt[o:o + GQA_HEAD_DIM], gk_ref[...], 1.0)
        khead = jnp.concatenate([r1, r2, zero], axis=0)
        k_ref[hd] = khead.T.astype(BF16)

    v_ref[...] = _nt(wv_ref[...], hb).astype(BF16)


def _pre_odd(lay, zin, mod, g_mix, w, tabs):
    d = mod.shape[-1]
    tokp = lay.TOKP
    col = lambda rows: pl.BlockSpec((rows, TM), lambda u: (0, u))
    hd2 = GQA_HEAD_DIM // 2
    zin_specs, zout_specs, zout_shape = _zin_specs(lay, zin, d)
    in_specs = zin_specs + [
        pl.BlockSpec((1, 6, d), lambda u: (lay.mrow(u), 0, 0)),
        _full(g_mix), _full(w["wq"]), _full(w["wk"]), _full(w["wv"]), _full(w["gq"]), _full(w["gk"]),
        pl.BlockSpec((hd2, TM), lambda u: (0, lay.ropeblk(u))),
        pl.BlockSpec((hd2, TM), lambda u: (0, lay.ropeblk(u))),
    ]
    out_specs = [
        col(GQA_HEADS * GQA_HEAD_DIM),
        pl.BlockSpec((GQA_KV_HEADS, TM, DK), lambda u: (0, u, 0)),
        col(GQA_KV_HEADS * DV),
    ]
    out_shape = [
        jax.ShapeDtypeStruct((GQA_HEADS * GQA_HEAD_DIM, tokp), BF16),
        jax.ShapeDtypeStruct((GQA_KV_HEADS, tokp, DK), BF16),
        jax.ShapeDtypeStruct((GQA_KV_HEADS * DV, tokp), BF16),
    ]
    return pl.pallas_call(
        functools.partial(_pre_odd_kernel, n_zin=len(zin), nl=lay.NL),
        grid=(lay.NT,), in_specs=in_specs, out_specs=out_specs + zout_specs, out_shape=out_shape + zout_shape,
        compiler_params=_cparams(), name="pre_odd",
    )(*zin, mod, g_mix, w["wq"], w["wk"], w["wv"], w["gq"], w["gk"], tabs["gqa_ct"], tabs["gqa_st"])


def _attn_kernel(q_ref, kc_ref, kl_ref, vc_ref, vl_ref, o_ref, s0_ref, s1_ref, *, nq, kb, grp, dq, S, C):
    qi = pl.program_id(2)
    hb = kb * grp
    sbufs = (s0_ref, s1_ref)
    nchunk = S // TKC

    def scores(hh, buf, with_latent):
        kv = hh // grp
        q = q_ref[hh * dq:(hh + 1) * dq, :]
        sc = _dot(kc_ref[kv, :, :dq], q)
        buf[0:C, :] = sc
        m = jnp.max(sc, axis=0, keepdims=True)
        if with_latent:
            for c in range(nchunk):
                s = _dot(kl_ref[kv, c * TKC:(c + 1) * TKC, :dq], q)
                buf[C + c * TKC:C + (c + 1) * TKC, :] = s
                m = jnp.maximum(m, jnp.max(s, axis=0, keepdims=True))
        return m

    def values(hh, buf, m, with_latent):
        kv = hh // grp
        vrows = slice(kv * DV, (kv + 1) * DV)
        p = jnp.exp2(buf[0:C, :] - m)
        l = jnp.sum(p, axis=0, keepdims=True)
        o = _dot(vc_ref[vrows, :], p.astype(BF16))
        if with_latent:
            for c in range(nchunk):
                p = jnp.exp2(buf[C + c * TKC:C + (c + 1) * TKC, :] - m)
                l = l + jnp.sum(p, axis=0, keepdims=True)
                o = o + _dot(vl_ref[vrows, c * TKC:(c + 1) * TKC], p.astype(BF16))
        o_ref[hh * DV:(hh + 1) * DV, :] = (o * (1.0 / l)).astype(o_ref.dtype)

    def run(with_latent):
        m = scores(0, sbufs[0], with_latent)
        for hh in range(hb):
            m_next = scores(hh + 1, sbufs[(hh + 1) % 2], with_latent) if hh + 1 < hb else None
            values(hh, sbufs[hh % 2], m, with_latent)
            m = m_next

    @pl.when(qi < nq)
    def _():
        run(True)

    @pl.when(qi == nq)
    def _():
        run(False)


def _attention(lay, qt, k, vt, *, n_kv, grp, dq, kb):
    B, S, C = lay.B, lay.S, lay.C
    hb = kb * grp
    nq = S // TQ

    def qcol(b, qi):
        return jnp.where(qi == nq, lay.CB + b, b * nq + qi)

    in_specs = [
        pl.BlockSpec((hb * dq, TQ), lambda b, h, qi: (h, qcol(b, qi))),
        pl.BlockSpec((kb, C, DK), lambda b, h, qi: (h, lay.CB + b, 0)),
        pl.BlockSpec((kb, S, DK), lambda b, h, qi: (h, b, 0)),
        pl.BlockSpec((kb * DV, C), lambda b, h, qi: (h, lay.CB + b)),
        pl.BlockSpec((kb * DV, S), lambda b, h, qi: (h, b)),
    ]
    return pl.pallas_call(
        functools.partial(_attn_kernel, nq=nq, kb=kb, grp=grp, dq=dq, S=S, C=C),
        grid=(B, n_kv // kb, nq + 1),
        in_specs=in_specs,
        out_specs=pl.BlockSpec((hb * DV, TQ), lambda b, h, qi: (h, qcol(b, qi))),
        out_shape=jax.ShapeDtypeStruct((n_kv * grp * DV, lay.TOKP), BF16),
        scratch_shapes=[pltpu.VMEM((C + S, TQ), F32), pltpu.VMEM((C + S, TQ), F32)],
        compiler_params=_cparams(),
        name="attention",
    )(qt, k, k, vt, vt)


def _dft_kernel(a_ref, b_ref, c_ref, s_ref, o_ref):
    o_ref[...] = (_dot(a_ref[...], c_ref[...]) - _dot(b_ref[...], s_ref[...])).astype(o_ref.dtype)


def _seq_dft(lay, abt, cn, sn, cc, sc):
    B, S, C = lay.B, lay.S, lay.C
    fw = FOURIER_WIDTH
    nj = S // TN_DFT
    ylat = pl.pallas_call(
        _dft_kernel,
        grid=(nj, B),
        in_specs=[
            pl.BlockSpec((fw, S), lambda j, b: (0, b)),
            pl.BlockSpec((fw, S), lambda j, b: (1, b)),
            pl.BlockSpec((S, TN_DFT), lambda j, b: (0, j)),
            pl.BlockSpec((S, TN_DFT), lambda j, b: (0, j)),
        ],
        out_specs=pl.BlockSpec((fw, TN_DFT), lambda j, b: (0, b * nj + j)),
        out_shape=jax.ShapeDtypeStruct((fw, B * S), BF16),
        compiler_params=_cparams(),
        name="seq_dft",
    )(abt, abt, cn, sn)
    yctx = pl.pallas_call(
        _dft_kernel,
        grid=(B,),
        in_specs=[
            pl.BlockSpec((fw, C), lambda b: (0, lay.CB + b)),
            pl.BlockSpec((fw, C), lambda b: (1, lay.CB + b)),
            pl.BlockSpec((C, C), lambda b: (0, 0)),
            pl.BlockSpec((C, C), lambda b: (0, 0)),
        ],
        out_specs=pl.BlockSpec((fw, C), lambda b: (0, b)),
        out_shape=jax.ShapeDtypeStruct((fw, B * C), BF16),
        compiler_params=_cparams(),
        name="ctx_dft",
    )(abt, abt, cc, sc)
    return ylat, yctx


def _route_rows(lg, rbias):
    s = jax.nn.sigmoid(lg)
    sel = s + rbias
    rows = [sel[e:e + 1, :] for e in range(N_EXPERTS)]
    best = None
    bg = None
    for g in range(N_GROUPS):
        v = rows[4 * g:4 * g + 4]
        sc = None
        for i in range(4):
            for j in range(i + 1, 4):
                ps = v[i] + v[j]
                sc = ps if sc is None else jnp.maximum(sc, ps)
        if best is None:
            best, bg = sc, jnp.zeros_like(sc)
        else:
            upd = sc > best
            bg = jnp.where(upd, float(g), bg)
            best = jnp.where(upd, sc, best)
    v = []
    for i in range(4):
        acc = rows[i]
        for g in range(1, N_GROUPS):
            acc = jnp.where(bg == float(g), rows[4 * g + i], acc)
        v.append(acc)
    i0 = jnp.zeros_like(bg)
    b0 = v[0]
    for i in range(1, 4):
        upd = v[i] > b0
        i0 = jnp.where(upd, float(i), i0)
        b0 = jnp.where(upd, v[i], b0)
    w = [jnp.where(i0 == float(i), -jnp.inf, v[i]) for i in range(4)]
    i1 = jnp.zeros_like(bg)
    b1 = w[0]
    for i in range(1, 4):
        upd = w[i] > b1
        i1 = jnp.where(upd, float(i), i1)
        b1 = jnp.where(upd, w[i], b1)
    lo = jnp.minimum(i0, i1)
    hi = jnp.maximum(i0, i1)
    pbase = jnp.where(lo == 0.0, 0.0, jnp.where(lo == 1.0, 3.0, 5.0))
    return bg * 6.0 + pbase + hi - lo - 1.0


def _post_kernel(z_ref, mod_ref, a_ref, actx_ref, b_ref, wa_ref, wb_ref, g_ref, rwh_ref, rwl_ref, rb_ref, ut_ref,
                 zo_ref, h2_ref, info_ref, cnt_ref, *, nl_split):
    a = a_ref[...]
    if nl_split is not None:
        a = jnp.where(pl.program_id(0) < nl_split, a, actx_ref[...])
    attn = _tn(a, wa_ref[...]) + _tn(b_ref[...], wb_ref[...])
    x1 = z_ref[...] + mod_ref[0, 2:3, :] * attn
    zo_ref[...] = x1
    h = _norm_mod(x1, g_ref[...], mod_ref[0, 4:5, :], mod_ref[0, 3:4, :])
    h2_ref[...] = h
    hh = h.astype(BF16)
    hl = (h - hh.astype(F32)).astype(BF16)
    lg = _nt(rwh_ref[...], hh) + _nt(rwh_ref[...], hl) + _nt(rwl_ref[...], hh)
    cls = _route_rows(lg, rb_ref[...])
    ohf = (lax.broadcasted_iota(jnp.int32, (32, TM), 0).astype(F32) == cls).astype(F32)
    rank_all = _dot(ohf.astype(BF16), ut_ref[...])
    info_ref[0:1, :] = cls
    info_ref[1:2, :] = jnp.sum(ohf * rank_all, axis=0, keepdims=True)
    info_ref[2:8, :] = jnp.zeros((6, TM), F32)
    cnt_ref[0] = jnp.broadcast_to(jnp.sum(ohf, axis=1, keepdims=True), (32, 128))


def _post(lay, z, mod, src_a, src_a_ctx, src_b, blk_a, blk_b, wa, wb, g_ffn, rwh, rwl, rbias, ut):
    d = z.shape[1]
    nt = lay.NT
    fw = FOURIER_WIDTH
    if src_a_ctx is None:
        nl_split = None
        src_a_ctx = jnp.zeros((fw, TM), BF16)
        spec_a = pl.BlockSpec((fw, TM), lambda u: (blk_a, u))
        spec_actx = pl.BlockSpec((fw, TM), lambda u: (0, 0))
    else:
        nl_split = lay.NL
        spec_a = pl.BlockSpec((fw, TM), lambda u: (blk_a, jnp.minimum(u, lay.NL - 1)))
        spec_actx = pl.BlockSpec((fw, TM), lambda u: (0, jnp.maximum(u - lay.NL, 0)))
    in_specs = [
        pl.BlockSpec((TM, d), lambda u: (u, 0)),
        pl.BlockSpec((1, 6, d), lambda u: (lay.mrow(u), 0, 0)),
        spec_a,
        spec_actx,
        pl.BlockSpec((fw, TM), lambda u: (blk_b, u)),
        _full(wa), _full(wb), _full(g_ffn), _full(rwh), _full(rwl), _full(rbias), _full(ut),
    ]
    out_specs = [
        pl.BlockSpec((TM, d), lambda u: (u, 0)),
        pl.BlockSpec((TM, d), lambda u: (u, 0)),
        pl.BlockSpec((8, TM), lambda u: (0, u)),
        pl.BlockSpec((1, 32, 128), lambda u: (u, 0, 0)),
    ]
    out_shape = [
        jax.ShapeDtypeStruct((lay.TOKP, d), F32),
        jax.ShapeDtypeStruct((lay.TOKP, d), F32),
        jax.ShapeDtypeStruct((8, lay.TOKP), F32),
        jax.ShapeDtypeStruct((nt, 32, 128), F32),
    ]
    return pl.pallas_call(
        functools.partial(_post_kernel, nl_split=nl_split),
        grid=(nt,), in_specs=in_specs, out_specs=out_specs, out_shape=out_shape,
        compiler_params=_cparams(), name="post",
    )(z, mod, src_a, src_a_ctx, src_b, wa, wb, g_ffn, rwh, rwl, rbias, ut)


def _invperm_kernel(pos_ref, lo_ref, hi_ref, tok_ref):
    def clear(p, c):
        tok_ref[p] = 0
        return c
    for k in range(lo_ref.shape[0]):
        lax.fori_loop(lo_ref[k], hi_ref[k], clear, 0)

    def put(t, c):
        tok_ref[pos_ref[t]] = t
        return c
    lax.fori_loop(0, pos_ref.shape[0], put, 0, unroll=16)


def _invperm(pos, pad_lo, pad_hi, pmax):
    smem = pl.BlockSpec(memory_space=pltpu.SMEM)
    return pl.pallas_call(
        _invperm_kernel,
        in_specs=[smem, smem, smem],
        out_specs=smem,
        out_shape=jax.ShapeDtypeStruct((pmax,), jnp.int32),
        name="moe_invperm",
    )(pos, pad_lo, pad_hi)


def _dispatch_kernel(zf_ref, pos_ref, h_ref, xs_ref, zbuf, zsem, sem):
    u = pl.program_id(0)
    ntile = zf_ref.shape[0]

    def fill(i):
        return pltpu.make_async_copy(zbuf, xs_ref.at[pl.ds(pl.multiple_of(i * TMM, TMM), TMM)], zsem)

    @pl.when(u == 0)
    def _():
        zbuf[...] = jnp.zeros(zbuf.shape, F32)

        def start(i, c):
            @pl.when(zf_ref[i] != 0)
            def _():
                fill(i).start()
            return c

        def wait(i, c):
            @pl.when(zf_ref[i] != 0)
            def _():
                fill(i).wait()
            return c

        lax.fori_loop(0, ntile, start, 0)
        lax.fori_loop(0, ntile, wait, 0)

    for r in range(TM):
        pltpu.make_async_copy(h_ref.at[pl.ds(r, 1)], xs_ref.at[pl.ds(pos_ref[0, 0, r], 1)], sem).start(priority=r % 2)
    pltpu.make_async_copy(h_ref, xs_ref.at[pl.ds(0, TM)], sem).wait()


def _dispatch(lay, zfill, pos3, h2, pmax):
    d = h2.shape[1]
    grid_spec = pltpu.PrefetchScalarGridSpec(
        num_scalar_prefetch=1,
        grid=(lay.NT,),
        in_specs=[
            pl.BlockSpec((1, 1, TM), lambda u, *_: (u, 0, 0), memory_space=pltpu.SMEM),
            pl.BlockSpec((TM, d), lambda u, *_: (u, 0)),
        ],
        out_specs=pl.BlockSpec(memory_space=pl.ANY),
        scratch_shapes=[pltpu.VMEM((TMM, d), F32), pltpu.SemaphoreType.DMA(()), pltpu.SemaphoreType.DMA(())],
    )
    return pl.pallas_call(
        _dispatch_kernel, grid_spec=grid_spec,
        out_shape=jax.ShapeDtypeStruct((pmax, d), F32),
        compiler_params=_cparams(), name="moe_dispatch",
    )(zfill, pos3, h2)


def _expert_kernel(ea_ref, eb_ref, nu_ref, nv_ref, tok_ref, x_ref, rw_ref, wga_ref, wua_ref, wda_ref,
                   wgb_ref, wub_ref, wdb_ref, ys_ref, ybuf, ssem):
    i = pl.program_id(0)
    nu = nu_ref[0]
    ntok = ys_ref.shape[0] - TMM

    def scatter_start(n):
        for r in range(TMM):
            dst = jnp.where(r < n, tok_ref[0, 0, r], ntok + r)
            pltpu.make_async_copy(ybuf.at[pl.ds(r, 1)], ys_ref.at[pl.ds(dst, 1)], ssem).start(priority=r % 2)

    def scatter_wait():
        pltpu.make_async_copy(ybuf, ys_ref.at[pl.ds(0, TMM)], ssem).wait()

    @pl.when(i < nu)
    def _():
        @pl.when(i == 0)
        def _():
            ybuf[...] = jnp.zeros(ybuf.shape, F32)
            spare = pltpu.make_async_copy(ybuf, ys_ref.at[pl.ds(ntok, TMM)], ssem)
            spare.start()
            spare.wait()

        xb = x_ref[...].astype(BF16)
        s = jax.nn.sigmoid(_dot(xb, rw_ref[...]))
        lane = lax.broadcasted_iota(jnp.int32, s.shape, 1)
        sa = jnp.sum(jnp.where(lane == ea_ref[i], s, 0.0), axis=1, keepdims=True)
        sb = jnp.sum(jnp.where(lane == eb_ref[i], s, 0.0), axis=1, keepdims=True)
        inv = 1.0 / (sa + sb)

        def hidden(wg_ref, wu_ref, gate):
            g = _dot(xb, wg_ref[0, 0])
            u = _dot(xb, wu_ref[0, 0])
            return (g * jax.nn.sigmoid(g) * u * gate).astype(BF16)

        y = (_dot(hidden(wga_ref, wua_ref, sa * inv), wda_ref[0, 0])
             + _dot(hidden(wgb_ref, wub_ref, sb * inv), wdb_ref[0, 0]))

        @pl.when(i > 0)
        def _():
            scatter_wait()

        ybuf[...] = y
        scatter_start(nv_ref[i])

        @pl.when(i == nu - 1)
        def _():
            scatter_wait()


def _experts(lay, layer, tok3, xs, ea, eb, nused, nvalid, rw, wg, wu, wd):
    ntile = tok3.shape[0]
    d = wd.shape[3]
    w_in = lambda sel: pl.BlockSpec((1, 1, d, D_EXPERT), lambda i, ea, eb, nu, nv: (layer, sel(ea, eb)[i], 0, 0))
    w_out = lambda sel: pl.BlockSpec((1, 1, D_EXPERT, d), lambda i, ea, eb, nu, nv: (layer, sel(ea, eb)[i], 0, 0))
    first = lambda a, b: a
    second = lambda a, b: b
    grid_spec = pltpu.PrefetchScalarGridSpec(
        num_scalar_prefetch=4,
        grid=(ntile,),
        in_specs=[
            pl.BlockSpec((1, 1, TMM), lambda i, *_: (i, 0, 0), memory_space=pltpu.SMEM),
            pl.BlockSpec((TMM, d), lambda i, ea, eb, nu, nv: (jnp.minimum(i, nu[0] - 1), 0)),
            pl.BlockSpec(rw.shape, lambda i, *_: (0, 0)),
            w_in(first), w_in(first), w_out(first),
            w_in(second), w_in(second), w_out(second),
        ],
        out_specs=pl.BlockSpec(memory_space=pl.ANY),
        scratch_shapes=[pltpu.VMEM((TMM, d), F32), pltpu.SemaphoreType.DMA(())],
    )
    return pl.pallas_call(
        _expert_kernel, grid_spec=grid_spec,
        out_shape=jax.ShapeDtypeStruct((lay.TOKP + TMM, d), F32),
        compiler_params=_cparams(), name="moe_experts",
    )(ea, eb, nused, nvalid, tok3, xs, rw, wg, wu, wd, wg, wu, wd)


def _final_kernel(z_ref, y_ref, mod_ref, gf_ref, o_ref):
    x2 = z_ref[...] + mod_ref[0, 5:6, :] * y_ref[...]
    ms = jnp.mean(x2 * x2, axis=-1, keepdims=True)
    o_ref[...] = x2 * lax.rsqrt(ms + EPS) * gf_ref[...]


def _final(lay, z, ys, mod, g_final):
    d = z.shape[1]
    return pl.pallas_call(
        _final_kernel,
        grid=(lay.NL,),
        in_specs=[
            pl.BlockSpec((TM, d), lambda u: (u, 0)),
            pl.BlockSpec((TM, d), lambda u: (u, 0)),
            pl.BlockSpec((1, 6, d), lambda u: (lay.mrow(u), 0, 0)),
            _full(g_final),
        ],
        out_specs=pl.BlockSpec((TM, d), lambda u: (u, 0)),
        out_shape=jax.ShapeDtypeStruct((lay.NL * TM, d), F32),
        compiler_params=_cparams(), name="final_norm",
    )(z, ys, mod, g_final)


def _plan(info, cnt, nt, ntile_max):
    cls = info[0].astype(jnp.int32).reshape(nt, TM)
    rank = info[1].astype(jnp.int32).reshape(nt, TM)
    cnt = cnt[:, :N_CLASSES, 0].astype(jnp.int32)
    tot = jnp.sum(cnt, axis=0)
    ntile_c = (tot + TMM - 1) // TMM
    tile_end = jnp.cumsum(ntile_c)
    class_off = (tile_end - ntile_c) * TMM
    tile_off = jnp.cumsum(cnt, axis=0) - cnt
    base = class_off[None, :] + tile_off
    onehot = cls[:, :, None] == jnp.arange(N_CLASSES, dtype=jnp.int32)[None, None, :]
    pos = jnp.sum(jnp.where(onehot, base[:, None, :], 0), axis=-1) + rank
    nused = tile_end[-1]
    ti = jnp.minimum(jnp.arange(ntile_max, dtype=jnp.int32), nused - 1)
    tcls = jnp.sum((ti[:, None] >= tile_end[None, :]).astype(jnp.int32), axis=1)
    onec = tcls[:, None] == jnp.arange(N_CLASSES, dtype=jnp.int32)[None, :]
    class_end = jnp.sum(jnp.where(onec, (class_off + tot)[None, :], 0), axis=1)
    nvalid = jnp.clip(class_end - ti * TMM, 0, TMM)
    tiles = jnp.arange(ntile_max, dtype=jnp.int32)
    zfill = ((tiles >= nused) | (nvalid < TMM)).astype(jnp.int32)
    grp, pair = tcls // 6, tcls % 6
    lo = jnp.asarray(_PAIR_LO, jnp.int32)
    hi = jnp.asarray(_PAIR_HI, jnp.int32)
    onep = pair[:, None] == jnp.arange(6, dtype=jnp.int32)[None, :]
    ea = 4 * grp + jnp.sum(jnp.where(onep, lo[None, :], 0), axis=1)
    eb = 4 * grp + jnp.sum(jnp.where(onep, hi[None, :], 0), axis=1)
    pad_lo = jnp.concatenate([class_off + tot, (nused * TMM).reshape(1)])
    pad_hi = jnp.concatenate([tile_end * TMM, jnp.full((1,), ntile_max * TMM, jnp.int32)])
    return pos.reshape(nt * TM), ea, eb, nused.reshape(1), nvalid, zfill, pad_lo, pad_hi


def _dft_mats(n, scale):
    r = 64 if (n % 64 == 0 and n > 64) else 1
    k = lax.broadcasted_iota(jnp.int32, (1, n), 1)

    def cs(rows, step):
        j = lax.broadcasted_iota(jnp.int32, (rows, 1), 0) * step
        ang = ((j * k) % n).astype(F32) * (2.0 * math.pi / n)
        return jnp.cos(ang), jnp.sin(ang)

    ca, sa = cs(n // r, r)
    if r == 1:
        c, s = ca, sa
    else:
        cb, sb = cs(r, 1)
        c = (ca[:, None, :] * cb[None, :, :] - sa[:, None, :] * sb[None, :, :]).reshape(n, n)
        s = (sa[:, None, :] * cb[None, :, :] + ca[:, None, :] * sb[None, :, :]).reshape(n, n)
    return (c * scale).astype(BF16), (s * scale).astype(BF16)


def _rope_angles(S, dim):
    rows = S // GRID_W
    row_id = jnp.repeat(jnp.arange(rows, dtype=F32), GRID_W)
    col_id = jnp.tile(jnp.arange(GRID_W, dtype=F32), rows)
    n_freq = dim // 4
    inv = ROPE_BASE ** (-jnp.arange(n_freq, dtype=F32) / n_freq)
    return jnp.concatenate([row_id[:, None] * inv, col_id[:, None] * inv], axis=-1)


def _tables(lay):
    S, C = lay.S, lay.C
    tabs = {}
    cc, sc = _dft_mats(FOURIER_GROUP_DIM, FOURIER_GROUP_DIM ** -0.5)
    tabs["cs"] = jnp.concatenate([cc, sc], axis=0)
    tabs["cn"], tabs["sn"] = _dft_mats(S, S ** -0.5)
    tabs["cctx"], tabs["sctx"] = _dft_mats(C, C ** -0.5)
    for name, dim in (("mla", MLA_ROPE_DIM), ("gqa", GQA_HEAD_DIM)):
        ang = _rope_angles(S, dim)
        cos = jnp.concatenate([jnp.ones((TM, dim // 2), F32), jnp.cos(ang)], axis=0)
        sin = jnp.concatenate([jnp.zeros((TM, dim // 2), F32), jnp.sin(ang)], axis=0)
        tabs[name + "_ct"] = cos.T
        tabs[name + "_st"] = sin.T
        if name == "mla":
            pad = jnp.zeros((TM + S, 128 - dim), F32)
            tabs["mla_t1"] = jnp.concatenate([cos, cos, pad], axis=1)
            tabs["mla_t2"] = jnp.concatenate([sin, -sin, pad], axis=1)
    tabs["ut"] = (lax.broadcasted_iota(jnp.int32, (TM, TM), 0)
                  < lax.broadcasted_iota(jnp.int32, (TM, TM), 1)).astype(BF16)
    return tabs


def _col(g, rows):
    return jnp.broadcast_to(g.astype(F32)[:, None], (rows, TM))


def _even_weights(w_in, q_norm_g, w_qb, kv_norm_g, w_kvb, w_out):
    d = w_in.shape[0]
    o1 = FOURIER_WIDTH
    o2 = o1 + MLA_Q_RANK
    o3 = o2 + MLA_KV_RANK
    bf = lambda a: a.astype(BF16)
    w = {}
    w["wf"] = bf(w_in[:, :o1].T)
    w["wqa"] = bf(w_in[:, o1:o2].T)
    w["wkvat"] = bf(w_in[:, o2:o3].T)
    w["wkva"] = bf(w_in[:, o2:o3])
    w["wpr"] = bf(jnp.concatenate([w_in[:, o3:], jnp.zeros((d, 128 - MLA_ROPE_DIM), F32)], axis=1))
    w["gq"] = _col(q_norm_g, MLA_Q_RANK)
    w["wqb"] = bf(w_qb.T)
    w["gkvc"] = _col(kv_norm_g, MLA_KV_RANK)
    w["gkvr"] = kv_norm_g.astype(F32)[None, :]
    kvb = w_kvb.reshape(MLA_KV_RANK, MLA_HEADS, MLA_NOPE_DIM + MLA_V_DIM)
    w["wv"] = bf(kvb[:, :, MLA_NOPE_DIM:].reshape(MLA_KV_RANK, MLA_HEADS * MLA_V_DIM).T)
    w1 = jnp.concatenate([kvb[:, :, :MLA_NOPE_DIM],
                          jnp.zeros((MLA_KV_RANK, MLA_HEADS, DK - MLA_NOPE_DIM), F32)], axis=2)
    w1 = w1.reshape(MLA_KV_RANK, MLA_HEADS * DK)
    eye = jnp.eye(MLA_ROPE_DIM, dtype=F32)
    place = jnp.zeros((128, DK), F32).at[:MLA_ROPE_DIM, MLA_NOPE_DIM:MLA_QK_DIM].set(eye)
    swap = jnp.roll(eye, MLA_ROPE_DIM // 2, axis=1)
    place_s = jnp.zeros((128, DK), F32).at[:MLA_ROPE_DIM, MLA_NOPE_DIM:MLA_QK_DIM].set(swap)
    w["wk"] = bf(jnp.concatenate([w1, jnp.tile(place, (1, MLA_HEADS)), jnp.tile(place_s, (1, MLA_HEADS))], axis=0))
    w["wout_a"] = bf(w_out[:o1])
    w["wout_b"] = bf(w_out[o1:])
    return w


def _odd_weights(w_qkv, q_norm_g, k_norm_g, w_out):
    bf = lambda a: a.astype(BF16)
    qw = GQA_HEADS * GQA_HEAD_DIM
    kw = GQA_KV_HEADS * GQA_HEAD_DIM
    w = {}
    w["wq"] = bf(w_qkv[:, :qw].T)
    w["wk"] = bf(w_qkv[:, qw:qw + kw].T)
    w["wv"] = bf(w_qkv[:, qw + kw:].T)
    w["gq"] = _col(q_norm_g, GQA_HEAD_DIM)
    w["gk"] = _col(k_norm_g, GQA_HEAD_DIM)
    w["wout_a"] = bf(w_out[:FOURIER_WIDTH])
    w["wout_b"] = bf(w_out[FOURIER_WIDTH:])
    return w


def kernel(x, c, ctx, c_ctx, ada_w, ada_b, norm_mix_g, norm_ffn_g, ev_w_in, ev_q_norm_g, ev_w_qb, ev_kv_norm_g,
           ev_w_kvb, ev_w_out, od_w_qkv, od_q_norm_g, od_k_norm_g, od_w_out, router_w, router_b, exp_w_gate,
           exp_w_up, exp_w_down, final_norm_g):
    B, S, d = x.shape
    C = ctx.shape[1]
    depth = ada_w.shape[0]
    lay = _Layout(B, S, C)
    assert B + 1 <= 16
    tabs = _tables(lay)

    cond = jnp.concatenate([c, c_ctx[None, :], jnp.zeros((16 - B - 1, d), F32)], axis=0)
    mods = _modulation(cond, ada_w, ada_b).reshape(depth, 16, 6, d)

    rwt = router_w.T.astype(F32)
    rwh = rwt.astype(BF16)
    rwl = (rwt - rwh.astype(F32)).astype(BF16)
    rbias = jnp.broadcast_to(router_b.astype(F32)[:, None], (N_EXPERTS, TM))
    rw_nat = jnp.concatenate([router_w, jnp.zeros((d, 128 - N_EXPERTS), F32)], axis=1).astype(BF16)
    wg_all = exp_w_gate.astype(BF16)
    wu_all = exp_w_up.astype(BF16)
    wd_all = exp_w_down.astype(BF16)
    ntile_max = lay.NT * TM // TMM + N_CLASSES

    zin = (x.reshape(B * S, d), ctx.reshape(B * C, d))
    for i in range(depth):
        mod = mods[i]
        g_mix = norm_mix_g[i][None, :]
        g_ffn = norm_ffn_g[i][None, :]
        j = i // 2
        if i % 2 == 0:
            w = _even_weights(ev_w_in[j], ev_q_norm_g[j], ev_w_qb[j], ev_kv_norm_g[j], ev_w_kvb[j], ev_w_out[j])
            abt, qt, k, vt, z = _pre_even(lay, zin, mod, g_mix, w, tabs)
            ot = _attention(lay, qt, k, vt, n_kv=MLA_HEADS, grp=1, dq=DK, kb=8)
            src_a, src_a_ctx = _seq_dft(lay, abt, tabs["cn"], tabs["sn"], tabs["cctx"], tabs["sctx"])
            src_b, blk_a, blk_b = ot, 0, 0
        else:
            w = _odd_weights(od_w_qkv[j], od_q_norm_g[j], od_k_norm_g[j], od_w_out[j])
            qt, k, vt, z = _pre_odd(lay, zin, mod, g_mix, w, tabs)
            ot = _attention(lay, qt, k, vt, n_kv=GQA_KV_HEADS, grp=GQA_GROUP, dq=GQA_HEAD_DIM, kb=4)
            src_a, src_a_ctx, src_b, blk_a, blk_b = ot, None, ot, 0, 1
        z1, h2, info, cnt = _post(lay, z, mod, src_a, src_a_ctx, src_b, blk_a, blk_b, w["wout_a"], w["wout_b"],
                                  g_ffn, rwh, rwl, rbias, tabs["ut"])
        pos, ea, eb, nused, nvalid, zfill, pad_lo, pad_hi = _plan(info, cnt, lay.NT, ntile_max)
        tok3 = _invperm(pos, pad_lo, pad_hi, ntile_max * TMM).reshape(ntile_max, 1, TMM)
        xs = _dispatch(lay, zfill, pos.reshape(lay.NT, 1, TM), h2, ntile_max * TMM)
        ys = _experts(lay, i, tok3, xs, ea, eb, nused, nvalid, rw_nat, wg_all, wu_all, wd_all)
        zin = (z1, ys, mod)
    out = _final(lay, zin[0], zin[1], zin[2], final_norm_g[None, :])
    return out.reshape(B, S, d)
```

```python
import functools
import math

import jax
import jax.numpy as jnp
from jax import lax
from jax.experimental import pallas as pl
from jax.experimental.pallas import tpu as pltpu

F32 = jnp.float32
BF16 = jnp.bfloat16

TM = 1024
TQ = 256
TKC = 512
TMM = 256
TN_DFT = 512
EPS = 1e-6
LOG2E = 1.4426950408889634
ROPE_BASE = 10000.0
GRID_W = 64

FOURIER_GROUPS = 4
FOURIER_GROUP_DIM = 128
FOURIER_WIDTH = 512
MLA_HEADS = 8
MLA_NOPE_DIM = 64
MLA_ROPE_DIM = 32
MLA_V_DIM = 64
MLA_QK_DIM = 96
MLA_Q_RANK = 256
MLA_KV_RANK = 128
MLA_SCALE = MLA_QK_DIM ** -0.5
GQA_HEADS = 16
GQA_KV_HEADS = 4
GQA_HEAD_DIM = 64
GQA_GROUP = 4
GQA_SCALE = GQA_HEAD_DIM ** -0.5
N_EXPERTS = 16
N_GROUPS = 4
N_CLASSES = 24
D_EXPERT = 512
DK = 128
DV = 64
VMEM_LIMIT = 56 * 1024 * 1024

_PAIR_LO = (0, 0, 0, 1, 1, 2)
_PAIR_HI = (1, 2, 3, 2, 3, 3)


def _nt(a, b):
    return lax.dot_general(a, b, (((1,), (1,)), ((), ())), preferred_element_type=F32)


def _tn(a, b):
    return lax.dot_general(a, b, (((0,), (0,)), ((), ())), preferred_element_type=F32)


def _dot(a, b):
    return jnp.dot(a, b, preferred_element_type=F32)


class _Layout:
    def __init__(self, B, S, C):
        assert C == TQ and TM % C == 0 and (B * C) % TM == 0 and S % TM == 0 and S % TN_DFT == 0 and S % TKC == 0
        self.B, self.S, self.C = B, S, C
        self.SB = S // TM
        self.NL = B * self.SB
        self.NT = self.NL + B * C // TM
        self.TOKP = self.NT * TM
        self.CB = B * S // C

    def mrow(self, u):
        return jnp.where(u < self.NL, u // self.SB, self.B)

    def ropeblk(self, u):
        return jnp.where(u < self.NL, 1 + u % self.SB, 0)


def _cparams():
    return pltpu.CompilerParams(vmem_limit_bytes=VMEM_LIMIT)


def _full(a):
    return pl.BlockSpec(a.shape, lambda *_: (0,) * a.ndim)


def _mod_kernel(s_ref, w_ref, b_ref, o_ref):
    s = s_ref[...]
    a = s * jax.nn.sigmoid(s)
    ah = a.astype(BF16)
    al = (a - ah.astype(F32)).astype(BF16)
    w = w_ref[0]
    wh = w.astype(BF16)
    wl = (w - wh.astype(F32)).astype(BF16)
    o_ref[0] = _dot(ah, wh) + _dot(ah, wl) + _dot(al, wh) + b_ref[0]


def _modulation(cond, ada_w, ada_b):
    depth, d, w6 = ada_w.shape
    tn = 768
    return pl.pallas_call(
        _mod_kernel,
        grid=(depth, w6 // tn),
        in_specs=[
            pl.BlockSpec((16, d), lambda i, j: (0, 0)),
            pl.BlockSpec((1, d, tn), lambda i, j: (i, 0, j)),
            pl.BlockSpec((1, 1, tn), lambda i, j: (i, 0, j)),
        ],
        out_specs=pl.BlockSpec((1, 16, tn), lambda i, j: (i, 0, j)),
        out_shape=jax.ShapeDtypeStruct((depth, 16, w6), F32),
        compiler_params=_cparams(),
        name="modulation",
    )(cond, ada_w, ada_b.reshape(depth, 1, w6))


def _norm_mod(z, g, sc, sh):
    ms = jnp.mean(z * z, axis=-1, keepdims=True)
    return (z * lax.rsqrt(ms + EPS) * g) * (1.0 + sc) + sh


def _zin_specs(lay, zin, d):
    row = pl.BlockSpec((TM, d), lambda u: (u, 0))
    zout = [row], [jax.ShapeDtypeStruct((lay.TOKP, d), F32)]
    if len(zin) == 2:
        lat = pl.BlockSpec((TM, d), lambda u: (jnp.minimum(u, lay.NL - 1), 0))
        ctx = pl.BlockSpec((TM, d), lambda u: (jnp.maximum(u - lay.NL, 0), 0))
        return [lat, ctx], *zout
    return [row, row, pl.BlockSpec((1, 6, d), lambda u: (lay.mrow(u), 0, 0))], *zout


def _zin_value(zin_refs, zout_refs, nl):
    zo_ref = zout_refs[0]
    if len(zin_refs) == 2:
        lat_ref, ctx_ref = zin_refs

        @pl.when(pl.program_id(0) < nl)
        def _():
            zo_ref[...] = lat_ref[...]

        @pl.when(pl.program_id(0) >= nl)
        def _():
            zo_ref[...] = ctx_ref[...]

        return zo_ref[...]
    z_ref, y_ref, modp_ref = zin_refs
    z = z_ref[...] + modp_ref[0, 5:6, :] * y_ref[...]
    zo_ref[...] = z
    return z


def _pre_even_kernel(*refs, n_zin, nl):
    (mod_ref, g_ref, wf_ref, wqa_ref, wkvat_ref, wkva_ref, wpr_ref, gq_ref, wqb_ref, gkvc_ref, gkvr_ref, wv_ref,
     wk_ref, cs_ref, ct_ref, st_ref, t1_ref, t2_ref, ab_ref, q_ref, k_ref, v_ref) = refs[n_zin:n_zin + 22]
    z = _zin_value(refs[:n_zin], refs[n_zin + 22:], nl)
    h = _norm_mod(z, g_ref[...], mod_ref[0, 1:2, :], mod_ref[0, 0:1, :])
    hb = h.astype(BF16)

    ft = _nt(wf_ref[...], hb).astype(BF16)
    gd = FOURIER_GROUP_DIM
    for g in range(FOURIER_GROUPS):
        ab = _dot(cs_ref[...], ft[g * gd:(g + 1) * gd, :])
        ab_ref[g * gd:(g + 1) * gd, :] = ab[:gd].astype(BF16)
        ab_ref[FOURIER_WIDTH + g * gd:FOURIER_WIDTH + (g + 1) * gd, :] = ab[gd:].astype(BF16)

    pq = _nt(wqa_ref[...], hb)
    rs = lax.rsqrt(jnp.mean(pq * pq, axis=0, keepdims=True) + EPS)
    qn = (pq * rs * gq_ref[...]).astype(BF16)
    qt = _dot(wqb_ref[...], qn) * (MLA_SCALE * LOG2E)
    cos = ct_ref[...]
    sin = st_ref[...]
    hr = MLA_ROPE_DIM // 2
    for hd in range(MLA_HEADS):
        o = hd * MLA_QK_DIM
        r = hd * DK
        x1 = qt[o + MLA_NOPE_DIM:o + MLA_NOPE_DIM + hr, :]
        x2 = qt[o + MLA_NOPE_DIM + hr:o + MLA_QK_DIM, :]
        q_ref[r:r + MLA_NOPE_DIM, :] = qt[o:o + MLA_NOPE_DIM, :].astype(BF16)
        q_ref[r + MLA_NOPE_DIM:r + MLA_NOPE_DIM + hr, :] = (x1 * cos - x2 * sin).astype(BF16)
        q_ref[r + MLA_NOPE_DIM + hr:r + MLA_QK_DIM, :] = (x1 * sin + x2 * cos).astype(BF16)
        q_ref[r + MLA_QK_DIM:r + DK, :] = jnp.zeros((DK - MLA_QK_DIM, TM), BF16)

    pkvt = _nt(wkvat_ref[...], hb)
    rst = lax.rsqrt(jnp.mean(pkvt * pkvt, axis=0, keepdims=True) + EPS)
    kvnt = (pkvt * rst * gkvc_ref[...]).astype(BF16)
    v_ref[...] = _dot(wv_ref[...], kvnt).astype(BF16)

    pkv = _dot(hb, wkva_ref[...])
    rsn = lax.rsqrt(jnp.mean(pkv * pkv, axis=-1, keepdims=True) + EPS)
    kvn = (pkv * rsn * gkvr_ref[...]).astype(BF16)
    pr = _dot(hb, wpr_ref[...])
    lhs = jnp.concatenate([kvn, (pr * t1_ref[...]).astype(BF16), (pr * t2_ref[...]).astype(BF16)], axis=1)
    kf = _dot(lhs, wk_ref[...])
    for hd in range(MLA_HEADS):
        k_ref[hd] = kf[:, hd * DK:(hd + 1) * DK].astype(BF16)


def _pre_even(lay, zin, mod, g_mix, w, tabs):
    d = mod.shape[-1]
    tokp = lay.TOKP
    col = lambda rows: pl.BlockSpec((rows, TM), lambda u: (0, u))
    hr = MLA_ROPE_DIM // 2
    zin_specs, zout_specs, zout_shape = _zin_specs(lay, zin, d)
    in_specs = zin_specs + [
        pl.BlockSpec((1, 6, d), lambda u: (lay.mrow(u), 0, 0)),
        _full(g_mix),
        _full(w["wf"]), _full(w["wqa"]), _full(w["wkvat"]), _full(w["wkva"]), _full(w["wpr"]),
        _full(w["gq"]), _full(w["wqb"]), _full(w["gkvc"]), _full(w["gkvr"]), _full(w["wv"]), _full(w["wk"]),
        _full(tabs["cs"]),
        pl.BlockSpec((hr, TM), lambda u: (0, lay.ropeblk(u))),
        pl.BlockSpec((hr, TM), lambda u: (0, lay.ropeblk(u))),
        pl.BlockSpec((TM, 128), lambda u: (lay.ropeblk(u), 0)),
        pl.BlockSpec((TM, 128), lambda u: (lay.ropeblk(u), 0)),
    ]
    out_specs = [
        col(2 * FOURIER_WIDTH),
        col(MLA_HEADS * DK),
        pl.BlockSpec((MLA_HEADS, TM, DK), lambda u: (0, u, 0)),
        col(MLA_HEADS * DV),
    ]
    out_shape = [
        jax.ShapeDtypeStruct((2 * FOURIER_WIDTH, tokp), BF16),
        jax.ShapeDtypeStruct((MLA_HEADS * DK, tokp), BF16),
        jax.ShapeDtypeStruct((MLA_HEADS, tokp, DK), BF16),
        jax.ShapeDtypeStruct((MLA_HEADS * DV, tokp), BF16),
    ]
    return pl.pallas_call(
        functools.partial(_pre_even_kernel, n_zin=len(zin), nl=lay.NL),
        grid=(lay.NT,), in_specs=in_specs, out_specs=out_specs + zout_specs, out_shape=out_shape + zout_shape,
        compiler_params=_cparams(), name="pre_even",
    )(*zin, mod, g_mix, w["wf"], w["wqa"], w["wkvat"], w["wkva"], w["wpr"], w["gq"], w["wqb"], w["gkvc"],
      w["gkvr"], w["wv"], w["wk"], tabs["cs"], tabs["mla_ct"], tabs["mla_st"], tabs["mla_t1"], tabs["mla_t2"])


def _pre_odd_kernel(*refs, n_zin, nl):
    (mod_ref, g_ref, wq_ref, wk_ref, wv_ref, gq_ref, gk_ref, ct_ref, st_ref,
     q_ref, k_ref, v_ref) = refs[n_zin:n_zin + 12]
    z = _zin_value(refs[:n_zin], refs[n_zin + 12:], nl)
    h = _norm_mod(z, g_ref[...], mod_ref[0, 1:2, :], mod_ref[0, 0:1, :])
    hb = h.astype(BF16)
    cos = ct_ref[...]
    sin = st_ref[...]
    hd2 = GQA_HEAD_DIM // 2

    def norm_rope(xt, gain, scale):
        rs = lax.rsqrt(jnp.mean(xt * xt, axis=0, keepdims=True) + EPS)
        xn = xt * rs * gain
        x1 = xn[:hd2]
        x2 = xn[hd2:]
        return (x1 * cos - x2 * sin) * scale, (x1 * sin + x2 * cos) * scale

    qt = _nt(wq_ref[...], hb)
    for hd in range(GQA_HEADS):
        o = hd * GQA_HEAD_DIM
        r1, r2 = norm_rope(qt[o:o + GQA_HEAD_DIM], gq_ref[...], GQA_SCALE * LOG2E)
        q_ref[o:o + hd2, :] = r1.astype(BF16)
        q_ref[o + hd2:o + GQA_HEAD_DIM, :] = r2.astype(BF16)

    kt = _nt(wk_ref[...], hb)
    zero = jnp.zeros((DK - GQA_HEAD_DIM, TM), F32)
    for hd in range(GQA_KV_HEADS):
        o = hd * GQA_HEAD_DIM
        r1, r2 = norm_rope(kt[o:o + GQA_HEAD_DIM], gk_ref[...], 1.0)
        khead = jnp.concatenate([r1, r2, zero], axis=0)
        k_ref[hd] = khead.T.astype(BF16)

    v_ref[...] = _nt(wv_ref[...], hb).astype(BF16)


def _pre_odd(lay, zin, mod, g_mix, w, tabs):
    d = mod.shape[-1]
    tokp = lay.TOKP
    col = lambda rows: pl.BlockSpec((rows, TM), lambda u: (0, u))
    hd2 = GQA_HEAD_DIM // 2
    zin_specs, zout_specs, zout_shape = _zin_specs(lay, zin, d)
    in_specs = zin_specs + [
        pl.BlockSpec((1, 6, d), lambda u: (lay.mrow(u), 0, 0)),
        _full(g_mix), _full(w["wq"]), _full(w["wk"]), _full(w["wv"]), _full(w["gq"]), _full(w["gk"]),
        pl.BlockSpec((hd2, TM), lambda u: (0, lay.ropeblk(u))),
        pl.BlockSpec((hd2, TM), lambda u: (0, lay.ropeblk(u))),
    ]
    out_specs = [
        col(GQA_HEADS * GQA_HEAD_DIM),
        pl.BlockSpec((GQA_KV_HEADS, TM, DK), lambda u: (0, u, 0)),
        col(GQA_KV_HEADS * DV),
    ]
    out_shape = [
        jax.ShapeDtypeStruct((GQA_HEADS * GQA_HEAD_DIM, tokp), BF16),
        jax.ShapeDtypeStruct((GQA_KV_HEADS, tokp, DK), BF16),
        jax.ShapeDtypeStruct((GQA_KV_HEADS * DV, tokp), BF16),
    ]
    return pl.pallas_call(
        functools.partial(_pre_odd_kernel, n_zin=len(zin), nl=lay.NL),
        grid=(lay.NT,), in_specs=in_specs, out_specs=out_specs + zout_specs, out_shape=out_shape + zout_shape,
        compiler_params=_cparams(), name="pre_odd",
    )(*zin, mod, g_mix, w["wq"], w["wk"], w["wv"], w["gq"], w["gk"], tabs["gqa_ct"], tabs["gqa_st"])


def _attn_kernel(q_ref, kc_ref, kl_ref, vc_ref, vl_ref, o_ref, s0_ref, s1_ref, *, nq, kb, grp, dq, S, C):
    qi = pl.program_id(2)
    hb = kb * grp
    sbufs = (s0_ref, s1_ref)
    nchunk = S // TKC

    def scores(hh, buf, with_latent):
        kv = hh // grp
        q = q_ref[hh * dq:(hh + 1) * dq, :]
        sc = _dot(kc_ref[kv, :, :dq], q)
        buf[0:C, :] = sc
        m = jnp.max(sc, axis=0, keepdims=True)
        if with_latent:
            for c in range(nchunk):
                s = _dot(kl_ref[kv, c * TKC:(c + 1) * TKC, :dq], q)
                buf[C + c * TKC:C + (c + 1) * TKC, :] = s
                m = jnp.maximum(m, jnp.max(s, axis=0, keepdims=True))
        return m

    def values(hh, buf, m, with_latent):
        kv = hh // grp
        vrows = slice(kv * DV, (kv + 1) * DV)
        p = jnp.exp2(buf[0:C, :] - m)
        l = jnp.sum(p, axis=0, keepdims=True)
        o = _dot(vc_ref[vrows, :], p.astype(BF16))
        if with_latent:
            for c in range(nchunk):
                p = jnp.exp2(buf[C + c * TKC:C + (c + 1) * TKC, :] - m)
                l = l + jnp.sum(p, axis=0, keepdims=True)
                o = o + _dot(vl_ref[vrows, c * TKC:(c + 1) * TKC], p.astype(BF16))
        o_ref[hh * DV:(hh + 1) * DV, :] = (o * (1.0 / l)).astype(o_ref.dtype)

    def run(with_latent):
        m = scores(0, sbufs[0], with_latent)
        for hh in range(hb):
            m_next = scores(hh + 1, sbufs[(hh + 1) % 2], with_latent) if hh + 1 < hb else None
            values(hh, sbufs[hh % 2], m, with_latent)
            m = m_next

    @pl.when(qi < nq)
    def _():
        run(True)

    @pl.when(qi == nq)
    def _():
        run(False)


def _attention(lay, qt, k, vt, *, n_kv, grp, dq, kb):
    B, S, C = lay.B, lay.S, lay.C
    hb = kb * grp
    nq = S // TQ

    def qcol(b, qi):
        return jnp.where(qi == nq, lay.CB + b, b * nq + qi)

    in_specs = [
        pl.BlockSpec((hb * dq, TQ), lambda b, h, qi: (h, qcol(b, qi))),
        pl.BlockSpec((kb, C, DK), lambda b, h, qi: (h, lay.CB + b, 0)),
        pl.BlockSpec((kb, S, DK), lambda b, h, qi: (h, b, 0)),
        pl.BlockSpec((kb * DV, C), lambda b, h, qi: (h, lay.CB + b)),
        pl.BlockSpec((kb * DV, S), lambda b, h, qi: (h, b)),
    ]
    return pl.pallas_call(
        functools.partial(_attn_kernel, nq=nq, kb=kb, grp=grp, dq=dq, S=S, C=C),
        grid=(B, n_kv // kb, nq + 1),
        in_specs=in_specs,
        out_specs=pl.BlockSpec((hb * DV, TQ), lambda b, h, qi: (h, qcol(b, qi))),
        out_shape=jax.ShapeDtypeStruct((n_kv * grp * DV, lay.TOKP), BF16),
        scratch_shapes=[pltpu.VMEM((C + S, TQ), F32), pltpu.VMEM((C + S, TQ), F32)],
        compiler_params=_cparams(),
        name="attention",
    )(qt, k, k, vt, vt)


def _dft_kernel(a_ref, b_ref, c_ref, s_ref, o_ref):
    o_ref[...] = (_dot(a_ref[...], c_ref[...]) - _dot(b_ref[...], s_ref[...])).astype(o_ref.dtype)


def _seq_dft(lay, abt, cn, sn, cc, sc):
    B, S, C = lay.B, lay.S, lay.C
    fw = FOURIER_WIDTH
    nj = S // TN_DFT
    ylat = pl.pallas_call(
        _dft_kernel,
        grid=(nj, B),
        in_specs=[
            pl.BlockSpec((fw, S), lambda j, b: (0, b)),
            pl.BlockSpec((fw, S), lambda j, b: (1, b)),
            pl.BlockSpec((S, TN_DFT), lambda j, b: (0, j)),
            pl.BlockSpec((S, TN_DFT), lambda j, b: (0, j)),
        ],
        out_specs=pl.BlockSpec((fw, TN_DFT), lambda j, b: (0, b * nj + j)),
        out_shape=jax.ShapeDtypeStruct((fw, B * S), BF16),
        compiler_params=_cparams(),
        name="seq_dft",
    )(abt, abt, cn, sn)
    yctx = pl.pallas_call(
        _dft_kernel,
        grid=(B,),
        in_specs=[
            pl.BlockSpec((fw, C), lambda b: (0, lay.CB + b)),
            pl.BlockSpec((fw, C), lambda b: (1, lay.CB + b)),
            pl.BlockSpec((C, C), lambda b: (0, 0)),
            pl.BlockSpec((C, C), lambda b: (0, 0)),
        ],
        out_specs=pl.BlockSpec((fw, C), lambda b: (0, b)),
        out_shape=jax.ShapeDtypeStruct((fw, B * C), BF16),
        compiler_params=_cparams(),
        name="ctx_dft",
    )(abt, abt, cc, sc)
    return ylat, yctx


def _route_rows(lg, rbias):
    s = jax.nn.sigmoid(lg)
    sel = s + rbias
    rows = [sel[e:e + 1, :] for e in range(N_EXPERTS)]
    best = None
    bg = None
    for g in range(N_GROUPS):
        v = rows[4 * g:4 * g + 4]
        sc = None
        for i in range(4):
            for j in range(i + 1, 4):
                ps = v[i] + v[j]
                sc = ps if sc is None else jnp.maximum(sc, ps)
        if best is None:
            best, bg = sc, jnp.zeros_like(sc)
        else:
            upd = sc > best
            bg = jnp.where(upd, float(g), bg)
            best = jnp.where(upd, sc, best)
    v = []
    for i in range(4):
        acc = rows[i]
        for g in range(1, N_GROUPS):
            acc = jnp.where(bg == float(g), rows[4 * g + i], acc)
        v.append(acc)
    i0 = jnp.zeros_like(bg)
    b0 = v[0]
    for i in range(1, 4):
        upd = v[i] > b0
        i0 = jnp.where(upd, float(i), i0)
        b0 = jnp.where(upd, v[i], b0)
    w = [jnp.where(i0 == float(i), -jnp.inf, v[i]) for i in range(4)]
    i1 = jnp.zeros_like(bg)
    b1 = w[0]
    for i in range(1, 4):
        upd = w[i] > b1
        i1 = jnp.where(upd, float(i), i1)
        b1 = jnp.where(upd, w[i], b1)
    lo = jnp.minimum(i0, i1)
    hi = jnp.maximum(i0, i1)
    pbase = jnp.where(lo == 0.0, 0.0, jnp.where(lo == 1.0, 3.0, 5.0))
    return bg * 6.0 + pbase + hi - lo - 1.0


def _post_kernel(z_ref, mod_ref, a_ref, actx_ref, b_ref, wa_ref, wb_ref, g_ref, rwh_ref, rwl_ref, rb_ref, ut_ref,
                 zo_ref, h2_ref, info_ref, cnt_ref, *, nl_split):
    a = a_ref[...]
    if nl_split is not None:
        a = jnp.where(pl.program_id(0) < nl_split, a, actx_ref[...])
    attn = _tn(a, wa_ref[...]) + _tn(b_ref[...], wb_ref[...])
    x1 = z_ref[...] + mod_ref[0, 2:3, :] * attn
    zo_ref[...] = x1
    h = _norm_mod(x1, g_ref[...], mod_ref[0, 4:5, :], mod_ref[0, 3:4, :])
    h2_ref[...] = h
    hh = h.astype(BF16)
    hl = (h - hh.astype(F32)).astype(BF16)
    lg = _nt(rwh_ref[...], hh) + _nt(rwh_ref[...], hl) + _nt(rwl_ref[...], hh)
    cls = _route_rows(lg, rb_ref[...])
    ohf = (lax.broadcasted_iota(jnp.int32, (32, TM), 0).astype(F32) == cls).astype(F32)
    rank_all = _dot(ohf.astype(BF16), ut_ref[...])
    info_ref[0:1, :] = cls
    info_ref[1:2, :] = jnp.sum(ohf * rank_all, axis=0, keepdims=True)
    info_ref[2:8, :] = jnp.zeros((6, TM), F32)
    cnt_ref[0] = jnp.broadcast_to(jnp.sum(ohf, axis=1, keepdims=True), (32, 128))


def _post(lay, z, mod, src_a, src_a_ctx, src_b, blk_a, blk_b, wa, wb, g_ffn, rwh, rwl, rbias, ut):
    d = z.shape[1]
    nt = lay.NT
    fw = FOURIER_WIDTH
    if src_a_ctx is None:
        nl_split = None
        src_a_ctx = jnp.zeros((fw, TM), BF16)
        spec_a = pl.BlockSpec((fw, TM), lambda u: (blk_a, u))
        spec_actx = pl.BlockSpec((fw, TM), lambda u: (0, 0))
    else:
        nl_split = lay.NL
        spec_a = pl.BlockSpec((fw, TM), lambda u: (blk_a, jnp.minimum(u, lay.NL - 1)))
        spec_actx = pl.BlockSpec((fw, TM), lambda u: (0, jnp.maximum(u - lay.NL, 0)))
    in_specs = [
        pl.BlockSpec((TM, d), lambda u: (u, 0)),
        pl.BlockSpec((1, 6, d), lambda u: (lay.mrow(u), 0, 0)),
        spec_a,
        spec_actx,
        pl.BlockSpec((fw, TM), lambda u: (blk_b, u)),
        _full(wa), _full(wb), _full(g_ffn), _full(rwh), _full(rwl), _full(rbias), _full(ut),
    ]
    out_specs = [
        pl.BlockSpec((TM, d), lambda u: (u, 0)),
        pl.BlockSpec((TM, d), lambda u: (u, 0)),
        pl.BlockSpec((8, TM), lambda u: (0, u)),
        pl.BlockSpec((1, 32, 128), lambda u: (u, 0, 0)),
    ]
    out_shape = [
        jax.ShapeDtypeStruct((lay.TOKP, d), F32),
        jax.ShapeDtypeStruct((lay.TOKP, d), F32),
        jax.ShapeDtypeStruct((8, lay.TOKP), F32),
        jax.ShapeDtypeStruct((nt, 32, 128), F32),
    ]
    return pl.pallas_call(
        functools.partial(_post_kernel, nl_split=nl_split),
        grid=(nt,), in_specs=in_specs, out_specs=out_specs, out_shape=out_shape,
        compiler_params=_cparams(), name="post",
    )(z, mod, src_a, src_a_ctx, src_b, wa, wb, g_ffn, rwh, rwl, rbias, ut)


def _dispatch_kernel(zf_ref, lo_ref, hi_ref, pos_ref, h_ref, xs_ref, tok_ref, zbuf, zsem, sem):
    u = pl.program_id(0)
    ntile = zf_ref.shape[0]

    @pl.when(u == 0)
    def _():
        def clear(p, c):
            tok_ref[p] = 0
            return c
        for k in range(lo_ref.shape[0]):
            lax.fori_loop(lo_ref[k], hi_ref[k], clear, 0)

    def put(r, c):
        tok_ref[pos_ref[0, 0, r]] = u * TM + r
        return c
    lax.fori_loop(0, TM, put, 0, unroll=16)

    def fill(i):
        return pltpu.make_async_copy(zbuf, xs_ref.at[pl.ds(pl.multiple_of(i * TMM, TMM), TMM)], zsem)

    @pl.when(u == 0)
    def _():
        zbuf[...] = jnp.zeros(zbuf.shape, F32)

        def start(i, c):
            @pl.when(zf_ref[i] != 0)
            def _():
                fill(i).start()
            return c

        def wait(i, c):
            @pl.when(zf_ref[i] != 0)
            def _():
                fill(i).wait()
            return c

        lax.fori_loop(0, ntile, start, 0)
        lax.fori_loop(0, ntile, wait, 0)

    for r in range(TM):
        pltpu.make_async_copy(h_ref.at[pl.ds(r, 1)], xs_ref.at[pl.ds(pos_ref[0, 0, r], 1)], sem).start(priority=r % 2)
    pltpu.make_async_copy(h_ref, xs_ref.at[pl.ds(0, TM)], sem).wait()


def _dispatch(lay, zfill, pad_lo, pad_hi, pos3, h2, pmax):
    d = h2.shape[1]
    grid_spec = pltpu.PrefetchScalarGridSpec(
        num_scalar_prefetch=3,
        grid=(lay.NT,),
        in_specs=[
            pl.BlockSpec((1, 1, TM), lambda u, *_: (u, 0, 0), memory_space=pltpu.SMEM),
            pl.BlockSpec((TM, d), lambda u, *_: (u, 0)),
        ],
        out_specs=[pl.BlockSpec(memory_space=pl.ANY), pl.BlockSpec(memory_space=pltpu.SMEM)],
        scratch_shapes=[pltpu.VMEM((TMM, d), F32), pltpu.SemaphoreType.DMA(()), pltpu.SemaphoreType.DMA(())],
    )
    return pl.pallas_call(
        _dispatch_kernel, grid_spec=grid_spec,
        out_shape=[jax.ShapeDtypeStruct((pmax, d), F32), jax.ShapeDtypeStruct((pmax,), jnp.int32)],
        compiler_params=_cparams(), name="moe_dispatch",
    )(zfill, pad_lo, pad_hi, pos3, h2)


def _expert_kernel(ea_ref, eb_ref, nu_ref, nv_ref, tok_ref, x_ref, rw_ref, wga_ref, wua_ref, wda_ref,
                   wgb_ref, wub_ref, wdb_ref, ys_ref, ybuf, ssem):
    i = pl.program_id(0)
    nu = nu_ref[0]
    ntok = ys_ref.shape[0] - TMM

    def scatter_start(n):
        for r in range(TMM):
            dst = jnp.where(r < n, tok_ref[0, 0, r], ntok + r)
            pltpu.make_async_copy(ybuf.at[pl.ds(r, 1)], ys_ref.at[pl.ds(dst, 1)], ssem).start(priority=r % 2)

    def scatter_wait():
        pltpu.make_async_copy(ybuf, ys_ref.at[pl.ds(0, TMM)], ssem).wait()

    @pl.when(i < nu)
    def _():
        @pl.when(i == 0)
        def _():
            ybuf[...] = jnp.zeros(ybuf.shape, F32)
            spare = pltpu.make_async_copy(ybuf, ys_ref.at[pl.ds(ntok, TMM)], ssem)
            spare.start()
            spare.wait()

        xb = x_ref[...].astype(BF16)
        s = jax.nn.sigmoid(_dot(xb, rw_ref[...]))
        lane = lax.broadcasted_iota(jnp.int32, s.shape, 1)
        sa = jnp.sum(jnp.where(lane == ea_ref[i], s, 0.0), axis=1, keepdims=True)
        sb = jnp.sum(jnp.where(lane == eb_ref[i], s, 0.0), axis=1, keepdims=True)
        inv = 1.0 / (sa + sb)

        def hidden(wg_ref, wu_ref, gate):
            g = _dot(xb, wg_ref[0, 0])
            u = _dot(xb, wu_ref[0, 0])
            return (g * jax.nn.sigmoid(g) * u * gate).astype(BF16)

        y = (_dot(hidden(wga_ref, wua_ref, sa * inv), wda_ref[0, 0])
             + _dot(hidden(wgb_ref, wub_ref, sb * inv), wdb_ref[0, 0]))

        @pl.when(i > 0)
        def _():
            scatter_wait()

        ybuf[...] = y
        scatter_start(nv_ref[i])

        @pl.when(i == nu - 1)
        def _():
            scatter_wait()


def _experts(lay, layer, tok3, xs, ea, eb, nused, nvalid, rw, wg, wu, wd):
    ntile = tok3.shape[0]
    d = wd.shape[3]
    w_in = lambda sel: pl.BlockSpec((1, 1, d, D_EXPERT), lambda i, ea, eb, nu, nv: (layer, sel(ea, eb)[i], 0, 0))
    w_out = lambda sel: pl.BlockSpec((1, 1, D_EXPERT, d), lambda i, ea, eb, nu, nv: (layer, sel(ea, eb)[i], 0, 0))
    first = lambda a, b: a
    second = lambda a, b: b
    grid_spec = pltpu.PrefetchScalarGridSpec(
        num_scalar_prefetch=4,
        grid=(ntile,),
        in_specs=[
            pl.BlockSpec((1, 1, TMM), lambda i, *_: (i, 0, 0), memory_space=pltpu.SMEM),
            pl.BlockSpec((TMM, d), lambda i, ea, eb, nu, nv: (jnp.minimum(i, nu[0] - 1), 0)),
            pl.BlockSpec(rw.shape, lambda i, *_: (0, 0)),
            w_in(first), w_in(first), w_out(first),
            w_in(second), w_in(second), w_out(second),
        ],
        out_specs=pl.BlockSpec(memory_space=pl.ANY),
        scratch_shapes=[pltpu.VMEM((TMM, d), F32), pltpu.SemaphoreType.DMA(())],
    )
    return pl.pallas_call(
        _expert_kernel, grid_spec=grid_spec,
        out_shape=jax.ShapeDtypeStruct((lay.TOKP + TMM, d), F32),
        compiler_params=_cparams(), name="moe_experts",
    )(ea, eb, nused, nvalid, tok3, xs, rw, wg, wu, wd, wg, wu, wd)


def _final_kernel(z_ref, y_ref, mod_ref, gf_ref, o_ref):
    x2 = z_ref[...] + mod_ref[0, 5:6, :] * y_ref[...]
    ms = jnp.mean(x2 * x2, axis=-1, keepdims=True)
    o_ref[...] = x2 * lax.rsqrt(ms + EPS) * gf_ref[...]


def _final(lay, z, ys, mod, g_final):
    d = z.shape[1]
    return pl.pallas_call(
        _final_kernel,
        grid=(lay.NL,),
        in_specs=[
            pl.BlockSpec((TM, d), lambda u: (u, 0)),
            pl.BlockSpec((TM, d), lambda u: (u, 0)),
            pl.BlockSpec((1, 6, d), lambda u: (lay.mrow(u), 0, 0)),
            _full(g_final),
        ],
        out_specs=pl.BlockSpec((TM, d), lambda u: (u, 0)),
        out_shape=jax.ShapeDtypeStruct((lay.NL * TM, d), F32),
        compiler_params=_cparams(), name="final_norm",
    )(z, ys, mod, g_final)


def _plan(info, cnt, nt, ntile_max):
    cls = info[0].astype(jnp.int32).reshape(nt, TM)
    rank = info[1].astype(jnp.int32).reshape(nt, TM)
    cnt = cnt[:, :N_CLASSES, 0].astype(jnp.int32)
    tot = jnp.sum(cnt, axis=0)
    ntile_c = (tot + TMM - 1) // TMM
    tile_end = jnp.cumsum(ntile_c)
    class_off = (tile_end - ntile_c) * TMM
    tile_off = jnp.cumsum(cnt, axis=0) - cnt
    base = class_off[None, :] + tile_off
    onehot = cls[:, :, None] == jnp.arange(N_CLASSES, dtype=jnp.int32)[None, None, :]
    pos = jnp.sum(jnp.where(onehot, base[:, None, :], 0), axis=-1) + rank
    nused = tile_end[-1]
    ti = jnp.minimum(jnp.arange(ntile_max, dtype=jnp.int32), nused - 1)
    tcls = jnp.sum((ti[:, None] >= tile_end[None, :]).astype(jnp.int32), axis=1)
    onec = tcls[:, None] == jnp.arange(N_CLASSES, dtype=jnp.int32)[None, :]
    class_end = jnp.sum(jnp.where(onec, (class_off + tot)[None, :], 0), axis=1)
    nvalid = jnp.clip(class_end - ti * TMM, 0, TMM)
    tiles = jnp.arange(ntile_max, dtype=jnp.int32)
    zfill = ((tiles >= nused) | (nvalid < TMM)).astype(jnp.int32)
    grp, pair = tcls // 6, tcls % 6
    lo = jnp.asarray(_PAIR_LO, jnp.int32)
    hi = jnp.asarray(_PAIR_HI, jnp.int32)
    onep = pair[:, None] == jnp.arange(6, dtype=jnp.int32)[None, :]
    ea = 4 * grp + jnp.sum(jnp.where(onep, lo[None, :], 0), axis=1)
    eb = 4 * grp + jnp.sum(jnp.where(onep, hi[None, :], 0), axis=1)
    pad_lo = jnp.concatenate([class_off + tot, (nused * TMM).reshape(1)])
    pad_hi = jnp.concatenate([tile_end * TMM, jnp.full((1,), ntile_max * TMM, jnp.int32)])
    return pos.reshape(nt * TM), ea, eb, nused.reshape(1), nvalid, zfill, pad_lo, pad_hi


def _dft_mats(n, scale):
    r = 64 if (n % 64 == 0 and n > 64) else 1
    k = lax.broadcasted_iota(jnp.int32, (1, n), 1)

    def cs(rows, step):
        j = lax.broadcasted_iota(jnp.int32, (rows, 1), 0) * step
        ang = ((j * k) % n).astype(F32) * (2.0 * math.pi / n)
        return jnp.cos(ang), jnp.sin(ang)

    ca, sa = cs(n // r, r)
    if r == 1:
        c, s = ca, sa
    else:
        cb, sb = cs(r, 1)
        c = (ca[:, None, :] * cb[None, :, :] - sa[:, None, :] * sb[None, :, :]).reshape(n, n)
        s = (sa[:, None, :] * cb[None, :, :] + ca[:, None, :] * sb[None, :, :]).reshape(n, n)
    return (c * scale).astype(BF16), (s * scale).astype(BF16)


def _rope_angles(S, dim):
    rows = S // GRID_W
    row_id = jnp.repeat(jnp.arange(rows, dtype=F32), GRID_W)
    col_id = jnp.tile(jnp.arange(GRID_W, dtype=F32), rows)
    n_freq = dim // 4
    inv = ROPE_BASE ** (-jnp.arange(n_freq, dtype=F32) / n_freq)
    return jnp.concatenate([row_id[:, None] * inv, col_id[:, None] * inv], axis=-1)


def _tables(lay):
    S, C = lay.S, lay.C
    tabs = {}
    cc, sc = _dft_mats(FOURIER_GROUP_DIM, FOURIER_GROUP_DIM ** -0.5)
    tabs["cs"] = jnp.concatenate([cc, sc], axis=0)
    tabs["cn"], tabs["sn"] = _dft_mats(S, S ** -0.5)
    tabs["cctx"], tabs["sctx"] = _dft_mats(C, C ** -0.5)
    for name, dim in (("mla", MLA_ROPE_DIM), ("gqa", GQA_HEAD_DIM)):
        ang = _rope_angles(S, dim)
        cos = jnp.concatenate([jnp.ones((TM, dim // 2), F32), jnp.cos(ang)], axis=0)
        sin = jnp.concatenate([jnp.zeros((TM, dim // 2), F32), jnp.sin(ang)], axis=0)
        tabs[name + "_ct"] = cos.T
        tabs[name + "_st"] = sin.T
        if name == "mla":
            pad = jnp.zeros((TM + S, 128 - dim), F32)
            tabs["mla_t1"] = jnp.concatenate([cos, cos, pad], axis=1)
            tabs["mla_t2"] = jnp.concatenate([sin, -sin, pad], axis=1)
    tabs["ut"] = (lax.broadcasted_iota(jnp.int32, (TM, TM), 0)
                  < lax.broadcasted_iota(jnp.int32, (TM, TM), 1)).astype(BF16)
    return tabs


def _col(g, rows):
    return jnp.broadcast_to(g.astype(F32)[:, None], (rows, TM))


def _even_weights(w_in, q_norm_g, w_qb, kv_norm_g, w_kvb, w_out):
    d = w_in.shape[0]
    o1 = FOURIER_WIDTH
    o2 = o1 + MLA_Q_RANK
    o3 = o2 + MLA_KV_RANK
    bf = lambda a: a.astype(BF16)
    w = {}
    w["wf"] = bf(w_in[:, :o1].T)
    w["wqa"] = bf(w_in[:, o1:o2].T)
    w["wkvat"] = bf(w_in[:, o2:o3].T)
    w["wkva"] = bf(w_in[:, o2:o3])
    w["wpr"] = bf(jnp.concatenate([w_in[:, o3:], jnp.zeros((d, 128 - MLA_ROPE_DIM), F32)], axis=1))
    w["gq"] = _col(q_norm_g, MLA_Q_RANK)
    w["wqb"] = bf(w_qb.T)
    w["gkvc"] = _col(kv_norm_g, MLA_KV_RANK)
    w["gkvr"] = kv_norm_g.astype(F32)[None, :]
    kvb = w_kvb.reshape(MLA_KV_RANK, MLA_HEADS, MLA_NOPE_DIM + MLA_V_DIM)
    w["wv"] = bf(kvb[:, :, MLA_NOPE_DIM:].reshape(MLA_KV_RANK, MLA_HEADS * MLA_V_DIM).T)
    w1 = jnp.concatenate([kvb[:, :, :MLA_NOPE_DIM],
                          jnp.zeros((MLA_KV_RANK, MLA_HEADS, DK - MLA_NOPE_DIM), F32)], axis=2)
    w1 = w1.reshape(MLA_KV_RANK, MLA_HEADS * DK)
    eye = jnp.eye(MLA_ROPE_DIM, dtype=F32)
    place = jnp.zeros((128, DK), F32).at[:MLA_ROPE_DIM, MLA_NOPE_DIM:MLA_QK_DIM].set(eye)
    swap = jnp.roll(eye, MLA_ROPE_DIM // 2, axis=1)
    place_s = jnp.zeros((128, DK), F32).at[:MLA_ROPE_DIM, MLA_NOPE_DIM:MLA_QK_DIM].set(swap)
    w["wk"] = bf(jnp.concatenate([w1, jnp.tile(place, (1, MLA_HEADS)), jnp.tile(place_s, (1, MLA_HEADS))], axis=0))
    w["wout_a"] = bf(w_out[:o1])
    w["wout_b"] = bf(w_out[o1:])
    return w


def _odd_weights(w_qkv, q_norm_g, k_norm_g, w_out):
    bf = lambda a: a.astype(BF16)
    qw = GQA_HEADS * GQA_HEAD_DIM
    kw = GQA_KV_HEADS * GQA_HEAD_DIM
    w = {}
    w["wq"] = bf(w_qkv[:, :qw].T)
    w["wk"] = bf(w_qkv[:, qw:qw + kw].T)
    w["wv"] = bf(w_qkv[:, qw + kw:].T)
    w["gq"] = _col(q_norm_g, GQA_HEAD_DIM)
    w["gk"] = _col(k_norm_g, GQA_HEAD_DIM)
    w["wout_a"] = bf(w_out[:FOURIER_WIDTH])
    w["wout_b"] = bf(w_out[FOURIER_WIDTH:])
    return w


def kernel(x, c, ctx, c_ctx, ada_w, ada_b, norm_mix_g, norm_ffn_g, ev_w_in, ev_q_norm_g, ev_w_qb, ev_kv_norm_g,
           ev_w_kvb, ev_w_out, od_w_qkv, od_q_norm_g, od_k_norm_g, od_w_out, router_w, router_b, exp_w_gate,
           exp_w_up, exp_w_down, final_norm_g):
    B, S, d = x.shape
    C = ctx.shape[1]
    depth = ada_w.shape[0]
    lay = _Layout(B, S, C)
    assert B + 1 <= 16
    tabs = _tables(lay)

    cond = jnp.concatenate([c, c_ctx[None, :], jnp.zeros((16 - B - 1, d), F32)], axis=0)
    mods = _modulation(cond, ada_w, ada_b).reshape(depth, 16, 6, d)

    rwt = router_w.T.astype(F32)
    rwh = rwt.astype(BF16)
    rwl = (rwt - rwh.astype(F32)).astype(BF16)
    rbias = jnp.broadcast_to(router_b.astype(F32)[:, None], (N_EXPERTS, TM))
    rw_nat = jnp.concatenate([router_w, jnp.zeros((d, 128 - N_EXPERTS), F32)], axis=1).astype(BF16)
    wg_all = exp_w_gate.astype(BF16)
    wu_all = exp_w_up.astype(BF16)
    wd_all = exp_w_down.astype(BF16)
    ntile_max = lay.NT * TM // TMM + N_CLASSES

    zin = (x.reshape(B * S, d), ctx.reshape(B * C, d))
    for i in range(depth):
        mod = mods[i]
        g_mix = norm_mix_g[i][None, :]
        g_ffn = norm_ffn_g[i][None, :]
        j = i // 2
        if i % 2 == 0:
            w = _even_weights(ev_w_in[j], ev_q_norm_g[j], ev_w_qb[j], ev_kv_norm_g[j], ev_w_kvb[j], ev_w_out[j])
            abt, qt, k, vt, z = _pre_even(lay, zin, mod, g_mix, w, tabs)
            ot = _attention(lay, qt, k, vt, n_kv=MLA_HEADS, grp=1, dq=DK, kb=8)
            src_a, src_a_ctx = _seq_dft(lay, abt, tabs["cn"], tabs["sn"], tabs["cctx"], tabs["sctx"])
            src_b, blk_a, blk_b = ot, 0, 0
        else:
            w = _odd_weights(od_w_qkv[j], od_q_norm_g[j], od_k_norm_g[j], od_w_out[j])
            qt, k, vt, z = _pre_odd(lay, zin, mod, g_mix, w, tabs)
            ot = _attention(lay, qt, k, vt, n_kv=GQA_KV_HEADS, grp=GQA_GROUP, dq=GQA_HEAD_DIM, kb=4)
            src_a, src_a_ctx, src_b, blk_a, blk_b = ot, None, ot, 0, 1
        z1, h2, info, cnt = _post(lay, z, mod, src_a, src_a_ctx, src_b, blk_a, blk_b, w["wout_a"], w["wout_b"],
                                  g_ffn, rwh, rwl, rbias, tabs["ut"])
        pos, ea, eb, nused, nvalid, zfill, pad_lo, pad_hi = _plan(info, cnt, lay.NT, ntile_max)
        xs, tok = _dispatch(lay, zfill, pad_lo, pad_hi, pos.reshape(lay.NT, 1, TM), h2, ntile_max * TMM)
        tok3 = tok.reshape(ntile_max, 1, TMM)
        ys = _experts(lay, i, tok3, xs, ea, eb, nused, nvalid, rw_nat, wg_all, wu_all, wd_all)
        zin = (z1, ys, mod)
    out = _final(lay, zin[0], zin[1], zin[2], final_norm_g[None, :])
    return out.reshape(B, S, d)
```
